```python
import jax, jax.numpy as jnp
from jax import lax
import numpy as np

D_MODEL = 1024
BATCH = 32
SEQ = 2048
DEPTH = 4

CHUNK = 64
N_A_LAYERS = DEPTH // 2
N_B_LAYERS = DEPTH - N_A_LAYERS
A_HEADS = 8
A_DQK = D_MODEL // 16
A_DV = D_MODEL // 8
A_QK = A_HEADS * A_DQK
A_V = A_HEADS * A_DV
A_IN = 2 * A_QK + 2 * A_V + 2 * A_HEADS
B_HEADS = 16
B_DH = D_MODEL // B_HEADS
B_DIM = B_HEADS * B_DH
LEFT_CHUNKS = 8
BAND = (LEFT_CHUNKS + 1) * CHUNK
MAX_REL = 256
N_REL = 2 * MAX_REL + 1
D_FF = -(-8 * D_MODEL // (3 * 256)) * 256
EPS = 1e-6
NEG_INIT = -1e30

kernel_name = "yoco_mlstm_chunked_relpos_attn_trunk"


def rmsnorm(x, g):
    x32 = x.astype(jnp.float32)
    y = x32 * lax.rsqrt(jnp.mean(x32 * x32, axis=-1, keepdims=True) + EPS) * g.astype(jnp.float32)
    return y.astype(x.dtype)


def swiglu(h, w_gate, w_up, w_down):
    return (jax.nn.silu(h @ w_gate) * (h @ w_up)) @ w_down


def mlstm_mixer(h, w_in, b_gate, g_head, w_out):
    B, S, _ = h.shape
    nc = S // CHUNK
    proj = h @ w_in
    q = proj[..., :A_QK].astype(jnp.float32).reshape(B, S, A_HEADS, A_DQK)
    k = proj[..., A_QK:2 * A_QK].astype(jnp.float32).reshape(B, S, A_HEADS, A_DQK) * (A_DQK ** -0.5)
    v = proj[..., 2 * A_QK:2 * A_QK + A_V].astype(jnp.float32).reshape(B, S, A_HEADS, A_DV)
    o_pre = proj[..., 2 * A_QK + A_V:2 * A_QK + 2 * A_V]
    gates = proj[..., 2 * A_QK + 2 * A_V:].astype(jnp.float32) + b_gate.astype(jnp.float32)
    i_log = gates[..., :A_HEADS]
    f_log = jax.nn.log_sigmoid(gates[..., A_HEADS:])

    def to_chunks(t):
        return t.reshape(B, nc, CHUNK, A_HEADS, -1).transpose(1, 0, 3, 2, 4)

    def gate_chunks(t):
        return t.reshape(B, nc, CHUNK, A_HEADS).transpose(1, 0, 3, 2)

    tri = jnp.tril(jnp.ones((CHUNK, CHUNK), dtype=bool))

    def step(carry, xs):
        C, n, m = carry
        qc, kc, vc, ic, fc = xs
        b = jnp.cumsum(fc, axis=-1)
        logD = b[..., :, None] - b[..., None, :] + ic[..., None, :]
        logD = jnp.where(tri, logD, -jnp.inf)
        inter = b + m[..., None]
        m_j = jnp.maximum(inter, jnp.max(logD, axis=-1))
        D = jnp.exp(logD - m_j[..., None])
        w_inter = jnp.exp(inter - m_j)
        s = jnp.einsum('bhld,bhsd->bhls', qc, kc) * D
        num = jnp.einsum('bhls,bhsv->bhlv', s, vc) + w_inter[..., None] * jnp.einsum('bhvd,bhld->bhlv', C, qc)
        den = jnp.sum(s, axis=-1) + w_inter * jnp.einsum('bhd,bhld->bhl', n, qc)
        h_out = num / jnp.maximum(jnp.abs(den), jnp.exp(-m_j))[..., None]
        bL = b[..., -1]
        a = bL[..., None] - b + ic
        m_new = jnp.maximum(bL + m, jnp.max(a, axis=-1))
        wa = jnp.exp(a - m_new[..., None])
        decay = jnp.exp(bL + m - m_new)
        C_new = decay[..., None, None] * C + jnp.einsum('bhs,bhsv,bhsd->bhvd', wa, vc, kc)
        n_new = decay[..., None] * n + jnp.einsum('bhs,bhsd->bhd', wa, kc)
        return (C_new, n_new, m_new), h_out

    init = (jnp.zeros((B, A_HEADS, A_DV, A_DQK), jnp.float32),
            jnp.zeros((B, A_HEADS, A_DQK), jnp.float32),
            jnp.full((B, A_HEADS), NEG_INIT, jnp.float32))
    _, hs = lax.scan(step, init, (to_chunks(q), to_chunks(k), to_chunks(v),
                                  gate_chunks(i_log), gate_chunks(f_log)))
    hs = hs.transpose(1, 0, 3, 2, 4).reshape(B, S, A_HEADS, A_DV)
    hs = hs * lax.rsqrt(jnp.mean(hs * hs, axis=-1, keepdims=True) + EPS)
    hs = hs.reshape(B, S, A_V) * g_head.astype(jnp.float32)
    hs = jax.nn.sigmoid(o_pre.astype(jnp.float32)) * hs
    return hs.astype(h.dtype) @ w_out


def chunked_relpos_attention(h, w_q, rel_table, w_o, k_sh, v_sh):
    B, S, _ = h.shape
    nc = S // CHUNK
    q = (h @ w_q).reshape(B, S, B_HEADS, B_DH) * (B_DH ** -0.5)
    q_chunks = q.reshape(B, nc, CHUNK, B_HEADS, B_DH).transpose(1, 0, 2, 3, 4)
    pad = ((0, 0), (LEFT_CHUNKS * CHUNK, 0), (0, 0), (0, 0))
    k_pad = jnp.pad(k_sh, pad)
    v_pad = jnp.pad(v_sh, pad)
    qi = jnp.arange(CHUNK)[:, None]
    kj = jnp.arange(BAND)[None, :]
    dist = LEFT_CHUNKS * CHUNK + qi - kj
    rel_idx = jnp.clip(dist, -MAX_REL, MAX_REL) + MAX_REL
    bias = rel_table.astype(jnp.float32)[:, rel_idx]

    def one_chunk(args):
        c, qc = args
        start = c * CHUNK
        kb = lax.dynamic_slice_in_dim(k_pad, start, BAND, axis=1)
        vb = lax.dynamic_slice_in_dim(v_pad, start, BAND, axis=1)
        s = jnp.einsum('bqhd,bkhd->bhqk', qc, kb).astype(jnp.float32) + bias
        valid = (start - LEFT_CHUNKS * CHUNK + jnp.arange(BAND)) >= 0
        s = jnp.where(valid, s, -jnp.inf)
        p = jax.nn.softmax(s, axis=-1)
        return jnp.einsum('bhqk,bkhd->bqhd', p.astype(vb.dtype), vb)

    out = lax.map(one_chunk, (jnp.arange(nc), q_chunks))
    out = out.transpose(1, 0, 2, 3, 4).reshape(B, S, B_DIM)
    return out @ w_o


def setup_inputs(seed: int = 0) -> dict:
    key = jax.random.key(seed)
    ks = jax.random.split(key, 20)
    nrm = jax.random.normal
    f32 = jnp.float32
    x = nrm(ks[0], (BATCH, SEQ, D_MODEL), f32)
    a_w_in = nrm(ks[1], (N_A_LAYERS, D_MODEL, A_IN), f32) * D_MODEL ** -0.5
    a_b_gate = jnp.concatenate([0.1 * nrm(ks[2], (N_A_LAYERS, A_HEADS), f32),
                                3.0 + 0.5 * nrm(ks[3], (N_A_LAYERS, A_HEADS), f32)], axis=-1)
    a_g_head = 1.0 + 0.05 * nrm(ks[4], (N_A_LAYERS, A_V), f32)
    a_w_out = nrm(ks[5], (N_A_LAYERS, A_V, D_MODEL), f32) * A_V ** -0.5
    b_w_q = nrm(ks[6], (N_B_LAYERS, D_MODEL, B_DIM), f32) * D_MODEL ** -0.5
    b_rel_bias = 0.1 * nrm(ks[7], (N_B_LAYERS, B_HEADS, N_REL), f32)
    b_w_o = nrm(ks[8], (N_B_LAYERS, B_DIM, D_MODEL), f32) * B_DIM ** -0.5
    kv_norm_g = 1.0 + 0.05 * nrm(ks[9], (D_MODEL,), f32)
    w_kv = nrm(ks[10], (D_MODEL, 2 * B_DIM), f32) * D_MODEL ** -0.5
    norm_mix_g = 1.0 + 0.05 * nrm(ks[11], (DEPTH, D_MODEL), f32)
    norm_ffn_g = 1.0 + 0.05 * nrm(ks[12], (DEPTH, D_MODEL), f32)
    ffn_w_gate = nrm(ks[13], (DEPTH, D_MODEL, D_FF), f32) * D_MODEL ** -0.5
    ffn_w_up = nrm(ks[14], (DEPTH, D_MODEL, D_FF), f32) * D_MODEL ** -0.5
    ffn_w_down = nrm(ks[15], (DEPTH, D_FF, D_MODEL), f32) * D_FF ** -0.5
    final_norm_g = 1.0 + 0.05 * nrm(ks[16], (D_MODEL,), f32)
    return {"x": x, "a_w_in": a_w_in, "a_b_gate": a_b_gate, "a_g_head": a_g_head,
            "a_w_out": a_w_out, "b_w_q": b_w_q, "b_rel_bias": b_rel_bias, "b_w_o": b_w_o,
            "kv_norm_g": kv_norm_g, "w_kv": w_kv, "norm_mix_g": norm_mix_g,
            "norm_ffn_g": norm_ffn_g, "ffn_w_gate": ffn_w_gate, "ffn_w_up": ffn_w_up,
            "ffn_w_down": ffn_w_down, "final_norm_g": final_norm_g}


def reference(x, a_w_in, a_b_gate, a_g_head, a_w_out, b_w_q, b_rel_bias, b_w_o,
              kv_norm_g, w_kv, norm_mix_g, norm_ffn_g, ffn_w_gate, ffn_w_up,
              ffn_w_down, final_norm_g):
    B, S, _ = x.shape
    h = x
    k_sh = None
    v_sh = None
    for l in range(DEPTH):
        if l == N_A_LAYERS:
            kv = rmsnorm(h, kv_norm_g) @ w_kv
            k_sh = kv[..., :B_DIM].reshape(B, S, B_HEADS, B_DH)
            v_sh = kv[..., B_DIM:].reshape(B, S, B_HEADS, B_DH)
        hn = rmsnorm(h, norm_mix_g[l])
        if l < N_A_LAYERS:
            h = h + mlstm_mixer(hn, a_w_in[l], a_b_gate[l], a_g_head[l], a_w_out[l])
        else:
            j = l - N_A_LAYERS
            h = h + chunked_relpos_attention(hn, b_w_q[j], b_rel_bias[j], b_w_o[j], k_sh, v_sh)
        h = h + swiglu(rmsnorm(h, norm_ffn_g[l]), ffn_w_gate[l], ffn_w_up[l], ffn_w_down[l])
    return rmsnorm(h, final_norm_g)
```

```python
import functools

import jax
import jax.numpy as jnp
from jax import lax
from jax.experimental import pallas as pl
from jax.experimental.pallas import tpu as pltpu

F32 = jnp.float32
BF16 = jnp.bfloat16

D_MODEL = 1024
CHUNK = 64
A_HEADS = 8
A_DQK = 64
A_DV = 128
A_QK = A_HEADS * A_DQK
A_V = A_HEADS * A_DV
B_HEADS = 16
B_DH = 64
LEFT_CHUNKS = 8
LEFT = LEFT_CHUNKS * CHUNK
BAND = LEFT + CHUNK
MAX_REL = 256
D_FF = 2816
EPS = 1e-6
NEG_INIT = -1e30

LANES = 128
MXU_COLS = 256

ROW_TILE = 512
A_SUPER = 128
FF_TILE = 256
VMEM_LIMIT = 56 * 1024 * 1024

A_GATE_OFF = 2 * A_QK + 2 * A_V
A_IN_PAD = A_GATE_OFF + LANES


def _rmsnorm(x, g):
    return x * lax.rsqrt(jnp.mean(x * x, axis=-1, keepdims=True) + EPS) * g


def _dot(a, b):
    return jnp.dot(a, b, preferred_element_type=F32)


def _dot_nt(a, b):
    return lax.dot_general(a, b, (((1,), (1,)), ((), ())), preferred_element_type=F32)


def _dot_tn(a, b):
    return lax.dot_general(a, b, (((0,), (0,)), ((), ())), preferred_element_type=F32)


def _const_spec(shape):
    nd = len(shape)
    return pl.BlockSpec(shape, lambda *_: (0,) * nd, pipeline_mode=pl.Buffered(1))


def _ffn_kernel(x_ref, g_ref, wg_ref, wu_ref, wd_ref, gf_ref, o_ref, *, final_norm):
    x = x_ref[...]
    xn = _rmsnorm(x, g_ref[...]).astype(BF16)
    o_ref[...] = x
    for j in range(D_FF // FF_TILE):
        cols = slice(j * FF_TILE, (j + 1) * FF_TILE)
        hg = _dot(xn, wg_ref[:, cols])
        hu = _dot(xn, wu_ref[:, cols])
        act = (hg * jax.nn.sigmoid(hg)) * hu
        o_ref[...] += _dot(act.astype(BF16), wd_ref[cols, :])
    if final_norm:
        o_ref[...] = _rmsnorm(o_ref[...], gf_ref[...])


def _ffn(h2d, g, wg, wu, wd, gf, final_norm):
    n_tok = h2d.shape[0]
    row = pl.BlockSpec((ROW_TILE, D_MODEL), lambda i: (i, 0))
    return pl.pallas_call(
        functools.partial(_ffn_kernel, final_norm=final_norm),
        grid=(n_tok // ROW_TILE,),
        in_specs=[row, _const_spec((1, D_MODEL)), _const_spec((D_MODEL, D_FF)),
                  _const_spec((D_MODEL, D_FF)), _const_spec((D_FF, D_MODEL)),
                  _const_spec((1, D_MODEL))],
        out_specs=row,
        out_shape=jax.ShapeDtypeStruct(h2d.shape, F32),
        compiler_params=pltpu.CompilerParams(
            dimension_semantics=("arbitrary",), vmem_limit_bytes=VMEM_LIMIT),
        name="ffn",
    )(h2d, g, wg, wu, wd, gf)


def _log_sigmoid(x):
    return jnp.minimum(x, 0.0) - jnp.log1p(jnp.exp(-jnp.abs(x)))


def _mlstm_kernel(x_ref, g_ref, win_ref, bg_ref, gh_ref, wout_ref, o_ref,
                  q_s, k_s, v_s, og_s, hs_s, ct_s, m_s):
    t = pl.program_id(1)

    @pl.when(t == 0)
    def _():
        ct_s[...] = jnp.zeros_like(ct_s)
        m_s[...] = jnp.full_like(m_s, NEG_INIT)

    x = x_ref[0]
    xn = _rmsnorm(x, g_ref[...]).astype(BF16)
    q_s[...] = _dot(xn, win_ref[:, 0:A_QK]).astype(BF16)
    k_s[...] = (_dot(xn, win_ref[:, A_QK:2 * A_QK]) * (A_DQK ** -0.5)).astype(BF16)
    v_s[...] = _dot(xn, win_ref[:, 2 * A_QK:2 * A_QK + A_V]).astype(BF16)
    og_s[...] = jax.nn.sigmoid(_dot(xn, win_ref[:, 2 * A_QK + A_V:A_GATE_OFF]))
    gates = _dot(xn, win_ref[:, A_GATE_OFF:A_IN_PAD]) + bg_ref[...]
    f_log = _log_sigmoid(gates)

    rows = lax.broadcasted_iota(jnp.int32, (ROW_TILE, LANES), 0) % A_SUPER
    b_cum = f_log
    shift = 1
    while shift < A_SUPER:
        rolled = pltpu.roll(b_cum, shift, axis=0)
        b_cum = b_cum + jnp.where(rows >= shift, rolled, 0.0)
        shift *= 2
    gates_t = gates.T
    b_cum_t = b_cum.T

    lane = lax.broadcasted_iota(jnp.int32, (A_SUPER, LANES), 1)
    row_i = lax.broadcasted_iota(jnp.int32, (A_SUPER, A_SUPER), 0)
    col_i = lax.broadcasted_iota(jnp.int32, (A_SUPER, A_SUPER), 1)
    tril = row_i >= col_i
    ones_col = jnp.where(lane == 0, 1.0, 0.0).astype(BF16)

    for c in range(ROW_TILE // A_SUPER):
        r = slice(c * A_SUPER, (c + 1) * A_SUPER)
        for h in range(A_HEADS):
            p, half = divmod(h, 2)
            own = (lane < A_DQK) if half == 0 else (lane >= A_DQK)
            pl_ = slice(p * LANES, (p + 1) * LANES)
            hl = slice(h * A_DV, (h + 1) * A_DV)
            qp = q_s[r, pl_]
            kp = k_s[r, pl_]
            qh = jnp.where(own, qp, jnp.zeros_like(qp))
            v_h = v_s[r, hl]

            b_col = b_cum[r, A_HEADS + h:A_HEADS + h + 1]
            i_col = gates[r, h:h + 1]
            b_row = b_cum_t[A_HEADS + h:A_HEADS + h + 1, r]
            i_row = gates_t[h:h + 1, r]
            b_last = b_row[:, A_SUPER - 1:A_SUPER]
            m_prev = m_s[h:h + 1, 0:1]

            log_d = jnp.where(tril, (b_col - b_row) + i_row, -jnp.inf)
            inter = b_col + m_prev
            m_j = jnp.maximum(inter, jnp.max(log_d, axis=1, keepdims=True))
            d_mat = jnp.exp(log_d - m_j)
            w_inter = jnp.exp(inter - m_j)

            s = _dot_nt(qh, kp) * d_mat
            qc = _dot(qh, ct_s[p].astype(BF16))
            num = _dot(s.astype(BF16), v_h) + w_inter * qc[:, 0:A_DV]
            den = jnp.sum(s, axis=1, keepdims=True) + w_inter * qc[:, A_DV:A_DV + 1]
            h_out = num / jnp.maximum(jnp.abs(den), jnp.exp(-m_j))

            h_out = h_out * lax.rsqrt(jnp.mean(h_out * h_out, axis=1, keepdims=True) + EPS)
            hs_s[r, hl] = (og_s[r, hl] * (h_out * gh_ref[:, hl])).astype(BF16)

            a_col = (b_last - b_col) + i_col
            m_new = jnp.maximum(b_last + m_prev, jnp.max(a_col, axis=0, keepdims=True))
            wa = jnp.exp(a_col - m_new)
            decay = jnp.exp(b_last + m_prev - m_new)
            kw = jnp.where(own, kp.astype(F32) * wa, 0.0).astype(BF16)
            v_ext = jnp.concatenate([v_h, ones_col], axis=1)
            upd = _dot_tn(kw, v_ext)
            hr = slice(half * A_DQK, (half + 1) * A_DQK)
            ct_s[p, hr, :] = decay * ct_s[p, hr, :] + upd[hr, :]
            m_s[h:h + 1, :] = jnp.broadcast_to(m_new, (1, LANES))

    o_ref[0] = x + _dot(hs_s[...], wout_ref[...])


def _mlstm_layer(h, g, w_in, b_gate, g_head, w_out):
    bsz, seq, _ = h.shape
    row = pl.BlockSpec((1, ROW_TILE, D_MODEL), lambda b, t: (b, t, 0))
    return pl.pallas_call(
        _mlstm_kernel,
        grid=(bsz, seq // ROW_TILE),
        in_specs=[row, _const_spec((1, D_MODEL)), _const_spec((D_MODEL, A_IN_PAD)),
                  _const_spec((1, LANES)), _const_spec((1, A_V)), _const_spec((A_V, D_MODEL))],
        out_specs=row,
        out_shape=jax.ShapeDtypeStruct(h.shape, F32),
        scratch_shapes=[
            pltpu.VMEM((ROW_TILE, A_QK), BF16),
            pltpu.VMEM((ROW_TILE, A_QK), BF16),
            pltpu.VMEM((ROW_TILE, A_V), BF16),
            pltpu.VMEM((ROW_TILE, A_V), F32),
            pltpu.VMEM((ROW_TILE, A_V), BF16),
            pltpu.VMEM((A_HEADS // 2, LANES, MXU_COLS), F32),
            pltpu.VMEM((A_HEADS, LANES), F32),
        ],
        compiler_params=pltpu.CompilerParams(
            dimension_semantics=("arbitrary", "arbitrary"), vmem_limit_bytes=VMEM_LIMIT),
        name="mlstm",
    )(h, g, w_in, b_gate, g_head, w_out)


def _kv_kernel(x_ref, g_ref, w_ref, k_ref, v_ref):
    t = pl.program_id(1)

    @pl.when(t == 0)
    def _():
        k_ref[...] = jnp.zeros_like(k_ref)
        v_ref[...] = jnp.zeros_like(v_ref)

    @pl.when(t > 0)
    def _():
        xn = _rmsnorm(x_ref[0], g_ref[...]).astype(BF16)
        k_ref[0] = _dot(xn, w_ref[:, 0:D_MODEL]).astype(BF16)
        v_ref[0] = _dot(xn, w_ref[:, D_MODEL:2 * D_MODEL]).astype(BF16)


def _kv_proj(h, g, w_kv):
    bsz, seq, _ = h.shape
    assert LEFT == ROW_TILE
    out = pl.BlockSpec((1, ROW_TILE, D_MODEL), lambda b, t: (b, t, 0))
    shape = jax.ShapeDtypeStruct((bsz, seq + LEFT, D_MODEL), BF16)
    return pl.pallas_call(
        _kv_kernel,
        grid=(bsz, seq // ROW_TILE + 1),
        in_specs=[pl.BlockSpec((1, ROW_TILE, D_MODEL), lambda b, t: (b, jnp.maximum(t - 1, 0), 0)),
                  _const_spec((1, D_MODEL)), _const_spec((D_MODEL, 2 * D_MODEL))],
        out_specs=[out, out],
        out_shape=[shape, shape],
        compiler_params=pltpu.CompilerParams(
            dimension_semantics=("arbitrary", "arbitrary"), vmem_limit_bytes=VMEM_LIMIT),
        name="kv_proj",
    )(h, g, w_kv)


def _attn_kernel(x_ref, g_ref, wq_ref, wo_ref, bias_ref, k_ref, v_ref, o_ref, q_s, a_s):
    t = pl.program_id(1)
    x = x_ref[0]
    xn = _rmsnorm(x, g_ref[...]).astype(BF16)
    q_s[...] = (_dot(xn, wq_ref[...]) * (B_DH ** -0.5)).astype(BF16)

    lane = lax.broadcasted_iota(jnp.int32, (CHUNK, LANES), 1)
    lo = lane < B_DH
    key_i = lax.broadcasted_iota(jnp.int32, (CHUNK, BAND), 1)

    def chunk_body(c, carry):
        start = pl.multiple_of((t * (ROW_TILE // CHUNK) + c) * CHUNK, CHUNK)
        r = pl.ds(pl.multiple_of(c * CHUNK, CHUNK), CHUNK)
        valid = key_i >= (LEFT - start)
        for p in range(B_HEADS // 2):
            pl_ = slice(p * LANES, (p + 1) * LANES)
            qp = q_s[r, pl_]
            kp = k_ref[0, pl.ds(start, BAND), pl_]
            vp = v_ref[0, pl.ds(start, BAND), pl_]
            outs = []
            for half in range(2):
                own = lo if half == 0 else jnp.logical_not(lo)
                qh = jnp.where(own, qp, jnp.zeros_like(qp))
                s = _dot_nt(qh, kp) + bias_ref[2 * p + half]
                s = jnp.where(valid, s, -jnp.inf)
                e = jnp.exp(s - jnp.max(s, axis=1, keepdims=True))
                o = _dot(e.astype(BF16), vp)
                outs.append(o / jnp.sum(e, axis=1, keepdims=True))
            a_s[r, pl_] = jnp.where(lo, outs[0], outs[1]).astype(BF16)
        return carry

    lax.fori_loop(0, ROW_TILE // CHUNK, chunk_body, 0)
    o_ref[0] = x + _dot(a_s[...], wo_ref[...])


def _attn_layer(h, g, w_q, w_o, bias, k_pad, v_pad):
    bsz, seq, _ = h.shape
    row = pl.BlockSpec((1, ROW_TILE, D_MODEL), lambda b, t: (b, t, 0))
    kv = pl.BlockSpec((1, seq + LEFT, D_MODEL), lambda b, t: (b, 0, 0))
    return pl.pallas_call(
        _attn_kernel,
        grid=(bsz, seq // ROW_TILE),
        in_specs=[row, _const_spec((1, D_MODEL)), _const_spec((D_MODEL, D_MODEL)),
                  _const_spec((D_MODEL, D_MODEL)), _const_spec((B_HEADS, CHUNK, BAND)), kv, kv],
        out_specs=row,
        out_shape=jax.ShapeDtypeStruct(h.shape, F32),
        scratch_shapes=[pltpu.VMEM((ROW_TILE, D_MODEL), BF16),
                        pltpu.VMEM((ROW_TILE, D_MODEL), BF16)],
        compiler_params=pltpu.CompilerParams(
            dimension_semantics=("arbitrary", "arbitrary"), vmem_limit_bytes=VMEM_LIMIT),
        name="attn",
    )(h, g, w_q, w_o, bias, k_pad, v_pad)


def _rel_bias(rel_table):
    qi = jnp.arange(CHUNK)[:, None]
    kj = jnp.arange(BAND)[None, :]
    idx = jnp.clip(LEFT + qi - kj, -MAX_REL, MAX_REL) + MAX_REL
    return rel_table.astype(F32)[:, idx]


def kernel(x, a_w_in, a_b_gate, a_g_head, a_w_out, b_w_q, b_rel_bias, b_w_o, kv_norm_g, w_kv,
           norm_mix_g, norm_ffn_g, ffn_w_gate, ffn_w_up, ffn_w_down, final_norm_g):
    bsz, seq, d = x.shape
    assert d == D_MODEL and seq % ROW_TILE == 0
    depth = norm_mix_g.shape[0]
    n_a = a_w_in.shape[0]

    def row(v):
        return v.reshape(1, -1).astype(F32)

    h = x
    k_pad = v_pad = None
    for l in range(depth):
        if l == n_a:
            k_pad, v_pad = _kv_proj(h, row(kv_norm_g), w_kv.astype(BF16))
        if l < n_a:
            w_in = jnp.pad(a_w_in[l], ((0, 0), (0, A_IN_PAD - a_w_in.shape[2]))).astype(BF16)
            b_gate = jnp.pad(a_b_gate[l].astype(F32), (0, LANES - 2 * A_HEADS)).reshape(1, LANES)
            h = _mlstm_layer(h, row(norm_mix_g[l]), w_in, b_gate, row(a_g_head[l]),
                             a_w_out[l].astype(BF16))
        else:
            j = l - n_a
            h = _attn_layer(h, row(norm_mix_g[l]), b_w_q[j].astype(BF16), b_w_o[j].astype(BF16),
                            _rel_bias(b_rel_bias[j]), k_pad, v_pad)
        h = _ffn(h.reshape(bsz * seq, d), row(norm_ffn_g[l]), ffn_w_gate[l].astype(BF16),
                 ffn_w_up[l].astype(BF16), ffn_w_down[l].astype(BF16), row(final_norm_g),
                 final_norm=(l == depth - 1)).reshape(bsz, seq, d)
    return h
```

```python
import functools

import jax
import jax.numpy as jnp
from jax import lax
from jax.experimental import pallas as pl
from jax.experimental.pallas import tpu as pltpu

F32 = jnp.float32
BF16 = jnp.bfloat16

D_MODEL = 1024
CHUNK = 64
A_HEADS = 8
A_DQK = 64
A_DV = 128
A_QK = A_HEADS * A_DQK
A_V = A_HEADS * A_DV
B_HEADS = 16
B_DH = 64
LEFT_CHUNKS = 8
LEFT = LEFT_CHUNKS * CHUNK
BAND = LEFT + CHUNK
MAX_REL = 256
D_FF = 2816
EPS = 1e-6
NEG_INIT = -1e30

LANES = 128
MXU_COLS = 256

ROW_TILE = 512
A_SUPER = 128
FF_TILE = 256
VMEM_LIMIT = 56 * 1024 * 1024

A_GATE_OFF = 2 * A_QK + 2 * A_V
A_IN_PAD = A_GATE_OFF + LANES


def _rmsnorm(x, g):
    return x * lax.rsqrt(jnp.mean(x * x, axis=-1, keepdims=True) + EPS) * g


def _dot(a, b):
    return jnp.dot(a, b, preferred_element_type=F32)


def _dot_nt(a, b):
    return lax.dot_general(a, b, (((1,), (1,)), ((), ())), preferred_element_type=F32)


def _dot_tn(a, b):
    return lax.dot_general(a, b, (((0,), (0,)), ((), ())), preferred_element_type=F32)


def _const_spec(shape):
    nd = len(shape)
    return pl.BlockSpec(shape, lambda *_: (0,) * nd, pipeline_mode=pl.Buffered(1))


def _ffn_kernel(x_ref, g_ref, wg_ref, wu_ref, wd_ref, gf_ref, o_ref, *, final_norm):
    x = x_ref[...]
    xn = _rmsnorm(x, g_ref[...]).astype(BF16)
    o_ref[...] = x
    for j in range(D_FF // FF_TILE):
        cols = slice(j * FF_TILE, (j + 1) * FF_TILE)
        hg = _dot(xn, wg_ref[:, cols])
        hu = _dot(xn, wu_ref[:, cols])
        act = (hg * jax.nn.sigmoid(hg)) * hu
        o_ref[...] += _dot(act.astype(BF16), wd_ref[cols, :])
    if final_norm:
        o_ref[...] = _rmsnorm(o_ref[...], gf_ref[...])


def _ffn(h2d, g, wg, wu, wd, gf, final_norm):
    n_tok = h2d.shape[0]
    row = pl.BlockSpec((ROW_TILE, D_MODEL), lambda i: (i, 0))
    return pl.pallas_call(
        functools.partial(_ffn_kernel, final_norm=final_norm),
        grid=(n_tok // ROW_TILE,),
        in_specs=[row, _const_spec((1, D_MODEL)), _const_spec((D_MODEL, D_FF)),
                  _const_spec((D_MODEL, D_FF)), _const_spec((D_FF, D_MODEL)),
                  _const_spec((1, D_MODEL))],
        out_specs=row,
        out_shape=jax.ShapeDtypeStruct(h2d.shape, F32),
        compiler_params=pltpu.CompilerParams(
            dimension_semantics=("arbitrary",), vmem_limit_bytes=VMEM_LIMIT),
        name="ffn",
    )(h2d, g, wg, wu, wd, gf)


def _log_sigmoid(x):
    return jnp.minimum(x, 0.0) - jnp.log1p(jnp.exp(-jnp.abs(x)))


def _mlstm_kernel(x_ref, g_ref, win_ref, bg_ref, gh_ref, wout_ref, o_ref,
                  q_s, k_s, v_s, og_s, hs_s, ct_s, m_s):
    t = pl.program_id(1)

    @pl.when(t == 0)
    def _():
        ct_s[...] = jnp.zeros_like(ct_s)
        m_s[...] = jnp.full_like(m_s, NEG_INIT)

    x = x_ref[0]
    xn = _rmsnorm(x, g_ref[...]).astype(BF16)
    q_s[...] = _dot(xn, win_ref[:, 0:A_QK]).astype(BF16)
    k_s[...] = (_dot(xn, win_ref[:, A_QK:2 * A_QK]) * (A_DQK ** -0.5)).astype(BF16)
    v_s[...] = _dot(xn, win_ref[:, 2 * A_QK:2 * A_QK + A_V]).astype(BF16)
    og_s[...] = jax.nn.sigmoid(_dot(xn, win_ref[:, 2 * A_QK + A_V:A_GATE_OFF]))
    gates = _dot(xn, win_ref[:, A_GATE_OFF:A_IN_PAD]) + bg_ref[...]
    f_log = _log_sigmoid(gates)

    rows = lax.broadcasted_iota(jnp.int32, (ROW_TILE, LANES), 0) % A_SUPER
    b_cum = f_log
    shift = 1
    while shift < A_SUPER:
        rolled = pltpu.roll(b_cum, shift, axis=0)
        b_cum = b_cum + jnp.where(rows >= shift, rolled, 0.0)
        shift *= 2
    gates_t = gates.T
    b_cum_t = b_cum.T

    lane = lax.broadcasted_iota(jnp.int32, (A_SUPER, LANES), 1)
    row_i = lax.broadcasted_iota(jnp.int32, (A_SUPER, A_SUPER), 0)
    col_i = lax.broadcasted_iota(jnp.int32, (A_SUPER, A_SUPER), 1)
    tril = row_i >= col_i
    ones_col = jnp.where(lane == 0, 1.0, 0.0).astype(BF16)

    for c in range(ROW_TILE // A_SUPER):
        r = slice(c * A_SUPER, (c + 1) * A_SUPER)
        for h in range(A_HEADS):
            p, half = divmod(h, 2)
            own = (lane < A_DQK) if half == 0 else (lane >= A_DQK)
            pl_ = slice(p * LANES, (p + 1) * LANES)
            hl = slice(h * A_DV, (h + 1) * A_DV)
            qp = q_s[r, pl_]
            kp = k_s[r, pl_]
            qh = jnp.where(own, qp, jnp.zeros_like(qp))
            v_h = v_s[r, hl]

            b_col = b_cum[r, A_HEADS + h:A_HEADS + h + 1]
            i_col = gates[r, h:h + 1]
            b_row = b_cum_t[A_HEADS + h:A_HEADS + h + 1, r]
            i_row = gates_t[h:h + 1, r]
            b_last = b_row[:, A_SUPER - 1:A_SUPER]
            m_prev = m_s[h:h + 1, 0:1]

            log_d = jnp.where(tril, (b_col - b_row) + i_row, -jnp.inf)
            inter = b_col + m_prev
            m_j = jnp.maximum(inter, jnp.max(log_d, axis=1, keepdims=True))
            d_mat = jnp.exp(log_d - m_j)
            w_inter = jnp.exp(inter - m_j)

            s = _dot_nt(qh, kp) * d_mat
            qc = _dot(qh, ct_s[p].astype(BF16))
            num = _dot(s.astype(BF16), v_h) + w_inter * qc[:, 0:A_DV]
            den = jnp.sum(s, axis=1, keepdims=True) + w_inter * qc[:, A_DV:A_DV + 1]
            h_out = num / jnp.maximum(jnp.abs(den), jnp.exp(-m_j))

            h_out = h_out * lax.rsqrt(jnp.mean(h_out * h_out, axis=1, keepdims=True) + EPS)
            hs_s[r, hl] = (og_s[r, hl] * (h_out * gh_ref[:, hl])).astype(BF16)

            a_col = (b_last - b_col) + i_col
            m_new = jnp.maximum(b_last + m_prev, jnp.max(a_col, axis=0, keepdims=True))
            wa = jnp.exp(a_col - m_new)
            decay = jnp.exp(b_last + m_prev - m_new)
            kw = jnp.where(own, kp.astype(F32) * wa, 0.0).astype(BF16)
            v_ext = jnp.concatenate([v_h, ones_col], axis=1)
            upd = _dot_tn(kw, v_ext)
            hr = slice(half * A_DQK, (half + 1) * A_DQK)
            ct_s[p, hr, :] = decay * ct_s[p, hr, :] + upd[hr, :]
            m_s[h:h + 1, :] = jnp.broadcast_to(m_new, (1, LANES))

    o_ref[0] = x + _dot(hs_s[...], wout_ref[...])


def _mlstm_layer(h, g, w_in, b_gate, g_head, w_out):
    bsz, seq, _ = h.shape
    row = pl.BlockSpec((1, ROW_TILE, D_MODEL), lambda b, t: (b, t, 0))
    return pl.pallas_call(
        _mlstm_kernel,
        grid=(bsz, seq // ROW_TILE),
        in_specs=[row, _const_spec((1, D_MODEL)), _const_spec((D_MODEL, A_IN_PAD)),
                  _const_spec((1, LANES)), _const_spec((1, A_V)), _const_spec((A_V, D_MODEL))],
        out_specs=row,
        out_shape=jax.ShapeDtypeStruct(h.shape, F32),
        scratch_shapes=[
            pltpu.VMEM((ROW_TILE, A_QK), BF16),
            pltpu.VMEM((ROW_TILE, A_QK), BF16),
            pltpu.VMEM((ROW_TILE, A_V), BF16),
            pltpu.VMEM((ROW_TILE, A_V), F32),
            pltpu.VMEM((ROW_TILE, A_V), BF16),
            pltpu.VMEM((A_HEADS // 2, LANES, MXU_COLS), F32),
            pltpu.VMEM((A_HEADS, LANES), F32),
        ],
        compiler_params=pltpu.CompilerParams(
            dimension_semantics=("arbitrary", "arbitrary"), vmem_limit_bytes=VMEM_LIMIT),
        name="mlstm",
    )(h, g, w_in, b_gate, g_head, w_out)


def _kv_kernel(x_ref, g_ref, w_ref, k_ref, v_ref):
    t = pl.program_id(1)

    @pl.when(t == 0)
    def _():
        k_ref[...] = jnp.zeros_like(k_ref)
        v_ref[...] = jnp.zeros_like(v_ref)

    @pl.when(t > 0)
    def _():
        xn = _rmsnorm(x_ref[0], g_ref[...]).astype(BF16)
        k_ref[0] = _dot(xn, w_ref[:, 0:D_MODEL]).astype(BF16)
        v_ref[0] = _dot(xn, w_ref[:, D_MODEL:2 * D_MODEL]).astype(BF16)


def _kv_proj(h, g, w_kv):
    bsz, seq, _ = h.shape
    assert LEFT == ROW_TILE
    out = pl.BlockSpec((1, ROW_TILE, D_MODEL), lambda b, t: (b, t, 0))
    shape = jax.ShapeDtypeStruct((bsz, seq + LEFT, D_MODEL), BF16)
    return pl.pallas_call(
        _kv_kernel,
        grid=(bsz, seq // ROW_TILE + 1),
        in_specs=[pl.BlockSpec((1, ROW_TILE, D_MODEL), lambda b, t: (b, jnp.maximum(t - 1, 0), 0)),
                  _const_spec((1, D_MODEL)), _const_spec((D_MODEL, 2 * D_MODEL))],
        out_specs=[out, out],
        out_shape=[shape, shape],
        compiler_params=pltpu.CompilerParams(
            dimension_semantics=("arbitrary", "arbitrary"), vmem_limit_bytes=VMEM_LIMIT),
        name="kv_proj",
    )(h, g, w_kv)


Q_BLOCK = 2 * CHUNK
K_BLOCK = LEFT + Q_BLOCK


def _attn_kernel(x_ref, g_ref, wq_ref, wo_ref, bias_ref, k_ref, v_ref, o_ref,
                 q_s, a_s, s_s, p_s, m_s, l_s):
    t = pl.program_id(1)
    x = x_ref[0]
    xn = _rmsnorm(x, g_ref[...]).astype(BF16)
    q_s[...] = (_dot(xn, wq_ref[...]) * (B_DH ** -0.5)).astype(BF16)

    lane = lax.broadcasted_iota(jnp.int32, (Q_BLOCK, LANES), 1)
    lo = lane < B_DH
    n_tiles = K_BLOCK // LANES

    def block_body(i, carry, *, masked):
        start = pl.multiple_of(t * ROW_TILE + i * Q_BLOCK, Q_BLOCK)
        r = pl.ds(pl.multiple_of(i * Q_BLOCK, Q_BLOCK), Q_BLOCK)
        band = pl.ds(start, K_BLOCK)
        if masked:
            key_i = lax.broadcasted_iota(jnp.int32, (Q_BLOCK, K_BLOCK), 1)
            valid = key_i >= (LEFT - start)
        for p in range(B_HEADS // 2):
            pl_ = slice(p * LANES, (p + 1) * LANES)
            qp = q_s[r, pl_]
            kp = k_ref[0, band, pl_]
            for half in range(2):
                h = 2 * p + half
                own = lo if half == 0 else jnp.logical_not(lo)
                qh = jnp.where(own, qp, jnp.zeros_like(qp))
                s = _dot_nt(qh, kp) + bias_ref[h]
                if masked:
                    s = jnp.where(valid, s, -jnp.inf)
                s_s[h] = s
                m_s[h] = jnp.broadcast_to(jnp.max(s, axis=1, keepdims=True), (Q_BLOCK, LANES))
        for h in range(B_HEADS):
            m = m_s[h]
            acc = jnp.zeros((Q_BLOCK, LANES), F32)
            for j in range(n_tiles):
                tl = slice(j * LANES, (j + 1) * LANES)
                e = jnp.exp(s_s[h, :, tl] - m)
                acc = acc + e
                p_s[h, :, tl] = e.astype(BF16)
            l_s[h] = jnp.broadcast_to(jnp.sum(acc, axis=1, keepdims=True), (Q_BLOCK, LANES))
        for p in range(B_HEADS // 2):
            pl_ = slice(p * LANES, (p + 1) * LANES)
            vp = v_ref[0, band, pl_]
            o0 = _dot(p_s[2 * p], vp) / l_s[2 * p]
            o1 = _dot(p_s[2 * p + 1], vp) / l_s[2 * p + 1]
            a_s[r, pl_] = jnp.where(lo, o0, o1).astype(BF16)
        return carry

    n_blocks = ROW_TILE // Q_BLOCK

    @pl.when(t == 0)
    def _():
        lax.fori_loop(0, n_blocks, functools.partial(block_body, masked=True), 0)

    @pl.when(t > 0)
    def _():
        lax.fori_loop(0, n_blocks, functools.partial(block_body, masked=False), 0)

    o_ref[0] = x + _dot(a_s[...], wo_ref[...])


def _attn_layer(h, g, w_q, w_o, bias, k_pad, v_pad):
    bsz, seq, _ = h.shape
    row = pl.BlockSpec((1, ROW_TILE, D_MODEL), lambda b, t: (b, t, 0))
    kv = pl.BlockSpec((1, seq + LEFT, D_MODEL), lambda b, t: (b, 0, 0))
    return pl.pallas_call(
        _attn_kernel,
        grid=(bsz, seq // ROW_TILE),
        in_specs=[row, _const_spec((1, D_MODEL)), _const_spec((D_MODEL, D_MODEL)),
                  _const_spec((D_MODEL, D_MODEL)), _const_spec((B_HEADS, Q_BLOCK, K_BLOCK)), kv, kv],
        out_specs=row,
        out_shape=jax.ShapeDtypeStruct(h.shape, F32),
        scratch_shapes=[pltpu.VMEM((ROW_TILE, D_MODEL), BF16),
                        pltpu.VMEM((ROW_TILE, D_MODEL), BF16),
                        pltpu.VMEM((B_HEADS, Q_BLOCK, K_BLOCK), F32),
                        pltpu.VMEM((B_HEADS, Q_BLOCK, K_BLOCK), BF16),
                        pltpu.VMEM((B_HEADS, Q_BLOCK, LANES), F32),
                        pltpu.VMEM((B_HEADS, Q_BLOCK, LANES), F32)],
        compiler_params=pltpu.CompilerParams(
            dimension_semantics=("arbitrary", "arbitrary"), vmem_limit_bytes=VMEM_LIMIT),
        name="attn",
    )(h, g, w_q, w_o, bias, k_pad, v_pad)


def _rel_bias(rel_table):
    n_heads = rel_table.shape[0]
    tab = rel_table.astype(F32)
    span = BAND + CHUNK - 1
    n_var = MAX_REL + CHUNK
    base = jnp.concatenate(
        [tab[:, MAX_REL - (CHUNK - 1):2 * MAX_REL + 1],
         jnp.broadcast_to(tab[:, 2 * MAX_REL:], (n_heads, span - n_var))], axis=1)
    rev = base[:, ::-1]
    padded = jnp.pad(rev, ((0, 0), (0, 1)))
    rows = jnp.tile(padded, (1, CHUNK))[:, :CHUNK * span].reshape(n_heads, CHUNK, span)
    band = rows[:, :, CHUNK - 1:CHUNK - 1 + BAND]
    ninf = jnp.full((n_heads, CHUNK, CHUNK), -jnp.inf, F32)
    return jnp.concatenate([jnp.concatenate([band, ninf], axis=2),
                            jnp.concatenate([ninf, band], axis=2)], axis=1)


def kernel(x, a_w_in, a_b_gate, a_g_head, a_w_out, b_w_q, b_rel_bias, b_w_o, kv_norm_g, w_kv,
           norm_mix_g, norm_ffn_g, ffn_w_gate, ffn_w_up, ffn_w_down, final_norm_g):
    bsz, seq, d = x.shape
    assert d == D_MODEL and seq % ROW_TILE == 0
    depth = norm_mix_g.shape[0]
    n_a = a_w_in.shape[0]

    def row(v):
        return v.reshape(1, -1).astype(F32)

    h = x
    k_pad = v_pad = None
    for l in range(depth):
        if l == n_a:
            k_pad, v_pad = _kv_proj(h, row(kv_norm_g), w_kv.astype(BF16))
        if l < n_a:
            w_in = jnp.pad(a_w_in[l], ((0, 0), (0, A_IN_PAD - a_w_in.shape[2]))).astype(BF16)
            b_gate = jnp.pad(a_b_gate[l].astype(F32), (0, LANES - 2 * A_HEADS)).reshape(1, LANES)
            h = _mlstm_layer(h, row(norm_mix_g[l]), w_in, b_gate, row(a_g_head[l]),
                             a_w_out[l].astype(BF16))
        else:
            j = l - n_a
            h = _attn_layer(h, row(norm_mix_g[l]), b_w_q[j].astype(BF16), b_w_o[j].astype(BF16),
                            _rel_bias(b_rel_bias[j]), k_pad, v_pad)
        h = _ffn(h.reshape(bsz * seq, d), row(norm_ffn_g[l]), ffn_w_gate[l].astype(BF16),
                 ffn_w_up[l].astype(BF16), ffn_w_down[l].astype(BF16), row(final_norm_g),
                 final_norm=(l == depth - 1)).reshape(bsz, seq, d)
    return h
```

```python
import functools

import jax
import jax.numpy as jnp
from jax import lax
from jax.experimental import pallas as pl
from jax.experimental.pallas import tpu as pltpu

F32 = jnp.float32
BF16 = jnp.bfloat16

D_MODEL = 1024
CHUNK = 64
A_HEADS = 8
A_DQK = 64
A_DV = 128
A_QK = A_HEADS * A_DQK
A_V = A_HEADS * A_DV
B_HEADS = 16
B_DH = 64
LEFT_CHUNKS = 8
LEFT = LEFT_CHUNKS * CHUNK
BAND = LEFT + CHUNK
MAX_REL = 256
D_FF = 2816
EPS = 1e-6
NEG_INIT = -1e30

LANES = 128
MXU_COLS = 256

ROW_TILE = 512
A_SUPER = 128
FF_TILE = 256
VMEM_LIMIT = 56 * 1024 * 1024

A_GATE_OFF = 2 * A_QK + 2 * A_V


def _rmsnorm(x, g):
    return x * lax.rsqrt(jnp.mean(x * x, axis=-1, keepdims=True) + EPS) * g


def _dot(a, b):
    return jnp.dot(a, b, preferred_element_type=F32)


def _dot_nt(a, b):
    return lax.dot_general(a, b, (((1,), (1,)), ((), ())), preferred_element_type=F32)


def _dot_tn(a, b):
    return lax.dot_general(a, b, (((0,), (0,)), ((), ())), preferred_element_type=F32)


def _const_spec(shape):
    nd = len(shape)
    return pl.BlockSpec(shape, lambda *_: (0,) * nd, pipeline_mode=pl.Buffered(1))


def _ffn_kernel(x_ref, g_ref, wg_ref, wu_ref, wd_ref, gf_ref, o_ref, *, final_norm):
    x = x_ref[...]
    xn = _rmsnorm(x, g_ref[...]).astype(BF16)
    o_ref[...] = x
    for j in range(D_FF // FF_TILE):
        cols = slice(j * FF_TILE, (j + 1) * FF_TILE)
        hg = _dot(xn, wg_ref[:, cols])
        hu = _dot(xn, wu_ref[:, cols])
        act = (hg * jax.nn.sigmoid(hg)) * hu
        o_ref[...] += _dot(act.astype(BF16), wd_ref[cols, :])
    if final_norm:
        o_ref[...] = _rmsnorm(o_ref[...], gf_ref[...])


def _ffn(h2d, g, wg, wu, wd, gf, final_norm):
    n_tok = h2d.shape[0]
    row = pl.BlockSpec((ROW_TILE, D_MODEL), lambda i: (i, 0))
    return pl.pallas_call(
        functools.partial(_ffn_kernel, final_norm=final_norm),
        grid=(n_tok // ROW_TILE,),
        in_specs=[row, _const_spec((1, D_MODEL)), _const_spec((D_MODEL, D_FF)),
                  _const_spec((D_MODEL, D_FF)), _const_spec((D_FF, D_MODEL)),
                  _const_spec((1, D_MODEL))],
        out_specs=row,
        out_shape=jax.ShapeDtypeStruct(h2d.shape, F32),
        compiler_params=pltpu.CompilerParams(
            dimension_semantics=("arbitrary",), vmem_limit_bytes=VMEM_LIMIT),
        name="ffn",
    )(h2d, g, wg, wu, wd, gf)


A_CEXT = A_DV + 16
A_STEPS = ROW_TILE // A_SUPER


def _log_sigmoid(x):
    return jnp.minimum(x, 0.0) - jnp.log1p(jnp.exp(-jnp.abs(x)))


def _mlstm_kernel(x_ref, g_ref, wfm_ref, wk_ref, wg_ref, bg_ref, ghb_ref, wout_ref, o_ref,
                  qt_s, k_s, vt_s, ogt_s, hst_s, svt_s, up_s, st_s, c_s, m_s):
    t = pl.program_id(1)

    @pl.when(t == 0)
    def _():
        c_s[...] = jnp.zeros_like(c_s)
        m_s[...] = jnp.full_like(m_s, NEG_INIT)

    x = x_ref[0]
    xn = _rmsnorm(x, g_ref[...]).astype(BF16)
    qt_s[...] = _dot_nt(wfm_ref[0:A_QK, :], xn).astype(BF16)
    vt_s[...] = _dot_nt(wfm_ref[A_QK:A_QK + A_V, :], xn).astype(BF16)
    ogt_s[...] = jax.nn.sigmoid(_dot_nt(wfm_ref[A_QK + A_V:A_QK + 2 * A_V, :], xn))
    k_s[...] = (_dot(xn, wk_ref[...]) * (A_DQK ** -0.5)).astype(BF16)
    gates = _dot(xn, wg_ref[...]) + bg_ref[...]
    i_log = gates[:, 0:LANES]
    f_log = _log_sigmoid(gates[:, LANES:2 * LANES])

    rows = lax.broadcasted_iota(jnp.int32, (ROW_TILE, LANES), 0) & (A_SUPER - 1)
    b_cum = f_log
    shift = 1
    while shift < A_SUPER:
        rolled = pltpu.roll(b_cum, shift, axis=0)
        b_cum = b_cum + jnp.where(rows >= shift, rolled, 0.0)
        shift *= 2
    imb = i_log - b_cum
    b_t = b_cum.T
    imb_t = imb.T

    row_i = lax.broadcasted_iota(jnp.int32, (A_SUPER, A_SUPER), 0)
    col_i = lax.broadcasted_iota(jnp.int32, (A_SUPER, A_SUPER), 1)
    causal = col_i >= row_i
    ext_row = lax.broadcasted_iota(jnp.int32, (A_CEXT - A_DV, LANES), 0) == 0
    c_lane = lax.broadcasted_iota(jnp.int32, (A_CEXT, LANES), 1)

    for c in range(A_STEPS):
        r = slice(c * A_SUPER, (c + 1) * A_SUPER)
        q_masked = []
        for h in range(A_HEADS):
            p, half = divmod(h, 2)
            pr = slice(p * LANES, (p + 1) * LANES)
            hr = slice(h * A_DV, (h + 1) * A_DV)
            own_rows = (row_i < A_DQK) if half == 0 else (row_i >= A_DQK)
            qp = qt_s[pr, r]
            qtm = jnp.where(own_rows, qp, jnp.zeros_like(qp))
            q_masked.append(qtm)
            kp = k_s[r, pr]
            vt_h = vt_s[hr, r]
            b_row = b_t[h:h + 1, r]
            imb_row = imb_t[h:h + 1, r]
            imb_col = imb[r, h:h + 1]
            b_last = jnp.broadcast_to(b_row[:, A_SUPER - 1:A_SUPER], (1, LANES))

            log_dt = jnp.where(causal, b_row + imb_col, -jnp.inf)
            mrow = jnp.max(log_dt, axis=0, keepdims=True)
            st = _dot(kp, qtm) * jnp.exp(log_dt - mrow)
            svt_s[hr, r] = _dot(vt_h, st.astype(BF16))

            a_row = b_last + imb_row
            amax = jnp.broadcast_to(jnp.max(a_row, axis=1, keepdims=True), (1, LANES))
            wa = jnp.exp(a_row - amax)
            vw = (vt_h.astype(F32) * wa).astype(BF16)
            ext = jnp.where(ext_row, jnp.broadcast_to(wa, (A_CEXT - A_DV, LANES)), 0.0).astype(BF16)
            up_s[c, h] = _dot(jnp.concatenate([vw, ext], axis=0), kp)

            st_s[c * A_HEADS + h, 0:1, :] = mrow
            st_s[c * A_HEADS + h, 1:2, :] = jnp.sum(st, axis=0, keepdims=True)
            st_s[c * A_HEADS + h, 2:3, :] = amax
            st_s[c * A_HEADS + h, 3:4, :] = b_last

        for h in range(A_HEADS):
            half = h % 2
            hr = slice(h * A_DV, (h + 1) * A_DV)
            own_lanes = (c_lane < A_DQK) if half == 0 else (c_lane >= A_DQK)
            b_row = b_t[h:h + 1, r]
            mrow = st_s[c * A_HEADS + h, 0:1, :]
            rs = st_s[c * A_HEADS + h, 1:2, :]
            amax = st_s[c * A_HEADS + h, 2:3, :]
            b_last = st_s[c * A_HEADS + h, 3:4, :]
            m_prev = m_s[h]

            inter = b_row + m_prev
            m_j = jnp.maximum(inter, mrow)
            f_intra = jnp.exp(mrow - m_j)
            w_inter = jnp.exp(inter - m_j)
            qc = _dot(c_s[h].astype(BF16), q_masked[h])
            num = f_intra * svt_s[hr, r] + w_inter * qc[0:A_DV]
            den = f_intra * rs + w_inter * qc[A_DV:A_DV + 1]
            ht = num * (1.0 / jnp.maximum(jnp.abs(den), jnp.exp(-m_j)))
            hn = ht * lax.rsqrt(jnp.mean(ht * ht, axis=0, keepdims=True) + EPS)
            hst_s[hr, r] = (ogt_s[hr, r] * (hn * ghb_ref[hr, :])).astype(BF16)

            m_new = jnp.maximum(b_last + m_prev, amax)
            decay = jnp.exp(b_last + m_prev - m_new)
            grow = jnp.exp(amax - m_new)
            c_s[h] = jnp.where(own_lanes, decay * c_s[h] + grow * up_s[c, h], 0.0)
            m_s[h] = m_new

    o_ref[0] = x + _dot_tn(hst_s[...], wout_ref[...])


def _mlstm_layer(h, g, w_fm, w_k, w_g, b_g, g_head_b, w_out):
    bsz, seq, _ = h.shape
    row = pl.BlockSpec((1, ROW_TILE, D_MODEL), lambda b, t: (b, t, 0))
    return pl.pallas_call(
        _mlstm_kernel,
        grid=(bsz, seq // ROW_TILE),
        in_specs=[row, _const_spec((1, D_MODEL)), _const_spec((A_QK + 2 * A_V, D_MODEL)),
                  _const_spec((D_MODEL, A_QK)), _const_spec((D_MODEL, 2 * LANES)),
                  _const_spec((1, 2 * LANES)), _const_spec((A_V, LANES)), _const_spec((A_V, D_MODEL))],
        out_specs=row,
        out_shape=jax.ShapeDtypeStruct(h.shape, F32),
        scratch_shapes=[
            pltpu.VMEM((A_QK, ROW_TILE), BF16),
            pltpu.VMEM((ROW_TILE, A_QK), BF16),
            pltpu.VMEM((A_V, ROW_TILE), BF16),
            pltpu.VMEM((A_V, ROW_TILE), F32),
            pltpu.VMEM((A_V, ROW_TILE), BF16),
            pltpu.VMEM((A_V, ROW_TILE), F32),
            pltpu.VMEM((A_STEPS, A_HEADS, A_CEXT, LANES), F32),
            pltpu.VMEM((A_STEPS * A_HEADS, 8, LANES), F32),
            pltpu.VMEM((A_HEADS, A_CEXT, LANES), F32),
            pltpu.VMEM((A_HEADS, 1, LANES), F32),
        ],
        compiler_params=pltpu.CompilerParams(
            dimension_semantics=("arbitrary", "arbitrary"), vmem_limit_bytes=VMEM_LIMIT),
        name="mlstm",
    )(h, g, w_fm, w_k, w_g, b_g, g_head_b, w_out)


def _kv_kernel(x_ref, g_ref, w_ref, k_ref, v_ref):
    t = pl.program_id(1)

    @pl.when(t == 0)
    def _():
        k_ref[...] = jnp.zeros_like(k_ref)
        v_ref[...] = jnp.zeros_like(v_ref)

    @pl.when(t > 0)
    def _():
        xn = _rmsnorm(x_ref[0], g_ref[...]).astype(BF16)
        k_ref[0] = _dot(xn, w_ref[:, 0:D_MODEL]).astype(BF16)
        v_ref[0] = _dot(xn, w_ref[:, D_MODEL:2 * D_MODEL]).astype(BF16)


def _kv_proj(h, g, w_kv):
    bsz, seq, _ = h.shape
    assert LEFT == ROW_TILE
    out = pl.BlockSpec((1, ROW_TILE, D_MODEL), lambda b, t: (b, t, 0))
    shape = jax.ShapeDtypeStruct((bsz, seq + LEFT, D_MODEL), BF16)
    return pl.pallas_call(
        _kv_kernel,
        grid=(bsz, seq // ROW_TILE + 1),
        in_specs=[pl.BlockSpec((1, ROW_TILE, D_MODEL), lambda b, t: (b, jnp.maximum(t - 1, 0), 0)),
                  _const_spec((1, D_MODEL)), _const_spec((D_MODEL, 2 * D_MODEL))],
        out_specs=[out, out],
        out_shape=[shape, shape],
        compiler_params=pltpu.CompilerParams(
            dimension_semantics=("arbitrary", "arbitrary"), vmem_limit_bytes=VMEM_LIMIT),
        name="kv_proj",
    )(h, g, w_kv)


Q_BLOCK = 2 * CHUNK
K_BLOCK = LEFT + Q_BLOCK


def _attn_kernel(x_ref, g_ref, wq_ref, wo_ref, bias_ref, k_ref, v_ref, o_ref,
                 q_s, a_s, s_s, p_s, m_s, l_s):
    t = pl.program_id(1)
    x = x_ref[0]
    xn = _rmsnorm(x, g_ref[...]).astype(BF16)
    q_s[...] = (_dot(xn, wq_ref[...]) * (B_DH ** -0.5)).astype(BF16)

    lane = lax.broadcasted_iota(jnp.int32, (Q_BLOCK, LANES), 1)
    lo = lane < B_DH
    n_tiles = K_BLOCK // LANES

    def block_body(i, carry, *, masked):
        start = pl.multiple_of(t * ROW_TILE + i * Q_BLOCK, Q_BLOCK)
        r = pl.ds(pl.multiple_of(i * Q_BLOCK, Q_BLOCK), Q_BLOCK)
        band = pl.ds(start, K_BLOCK)
        if masked:
            key_i = lax.broadcasted_iota(jnp.int32, (Q_BLOCK, K_BLOCK), 1)
            valid = key_i >= (LEFT - start)
        for p in range(B_HEADS // 2):
            pl_ = slice(p * LANES, (p + 1) * LANES)
            qp = q_s[r, pl_]
            kp = k_ref[0, band, pl_]
            for half in range(2):
                h = 2 * p + half
                own = lo if half == 0 else jnp.logical_not(lo)
                qh = jnp.where(own, qp, jnp.zeros_like(qp))
                s = _dot_nt(qh, kp) + bias_ref[h]
                if masked:
                    s = jnp.where(valid, s, -jnp.inf)
                s_s[h] = s
                m_s[h] = jnp.broadcast_to(jnp.max(s, axis=1, keepdims=True), (Q_BLOCK, LANES))
        for h in range(B_HEADS):
            m = m_s[h]
            acc = jnp.zeros((Q_BLOCK, LANES), F32)
            for j in range(n_tiles):
                tl = slice(j * LANES, (j + 1) * LANES)
                e = jnp.exp(s_s[h, :, tl] - m)
                acc = acc + e
                p_s[h, :, tl] = e.astype(BF16)
            l_s[h] = jnp.broadcast_to(jnp.sum(acc, axis=1, keepdims=True), (Q_BLOCK, LANES))
        for p in range(B_HEADS // 2):
            pl_ = slice(p * LANES, (p + 1) * LANES)
            vp = v_ref[0, band, pl_]
            o0 = _dot(p_s[2 * p], vp) / l_s[2 * p]
            o1 = _dot(p_s[2 * p + 1], vp) / l_s[2 * p + 1]
            a_s[r, pl_] = jnp.where(lo, o0, o1).astype(BF16)
        return carry

    n_blocks = ROW_TILE // Q_BLOCK

    @pl.when(t == 0)
    def _():
        lax.fori_loop(0, n_blocks, functools.partial(block_body, masked=True), 0)

    @pl.when(t > 0)
    def _():
        lax.fori_loop(0, n_blocks, functools.partial(block_body, masked=False), 0)

    o_ref[0] = x + _dot(a_s[...], wo_ref[...])


def _attn_layer(h, g, w_q, w_o, bias, k_pad, v_pad):
    bsz, seq, _ = h.shape
    row = pl.BlockSpec((1, ROW_TILE, D_MODEL), lambda b, t: (b, t, 0))
    kv = pl.BlockSpec((1, seq + LEFT, D_MODEL), lambda b, t: (b, 0, 0))
    return pl.pallas_call(
        _attn_kernel,
        grid=(bsz, seq // ROW_TILE),
        in_specs=[row, _const_spec((1, D_MODEL)), _const_spec((D_MODEL, D_MODEL)),
                  _const_spec((D_MODEL, D_MODEL)), _const_spec((B_HEADS, Q_BLOCK, K_BLOCK)), kv, kv],
        out_specs=row,
        out_shape=jax.ShapeDtypeStruct(h.shape, F32),
        scratch_shapes=[pltpu.VMEM((ROW_TILE, D_MODEL), BF16),
                        pltpu.VMEM((ROW_TILE, D_MODEL), BF16),
                        pltpu.VMEM((B_HEADS, Q_BLOCK, K_BLOCK), F32),
                        pltpu.VMEM((B_HEADS, Q_BLOCK, K_BLOCK), BF16),
                        pltpu.VMEM((B_HEADS, Q_BLOCK, LANES), F32),
                        pltpu.VMEM((B_HEADS, Q_BLOCK, LANES), F32)],
        compiler_params=pltpu.CompilerParams(
            dimension_semantics=("arbitrary", "arbitrary"), vmem_limit_bytes=VMEM_LIMIT),
        name="attn",
    )(h, g, w_q, w_o, bias, k_pad, v_pad)


def _rel_bias(rel_table):
    n_heads = rel_table.shape[0]
    tab = rel_table.astype(F32)
    span = BAND + CHUNK - 1
    n_var = MAX_REL + CHUNK
    base = jnp.concatenate(
        [tab[:, MAX_REL - (CHUNK - 1):2 * MAX_REL + 1],
         jnp.broadcast_to(tab[:, 2 * MAX_REL:], (n_heads, span - n_var))], axis=1)
    rev = base[:, ::-1]
    padded = jnp.pad(rev, ((0, 0), (0, 1)))
    rows = jnp.tile(padded, (1, CHUNK))[:, :CHUNK * span].reshape(n_heads, CHUNK, span)
    band = rows[:, :, CHUNK - 1:CHUNK - 1 + BAND]
    ninf = jnp.full((n_heads, CHUNK, CHUNK), -jnp.inf, F32)
    return jnp.concatenate([jnp.concatenate([band, ninf], axis=2),
                            jnp.concatenate([ninf, band], axis=2)], axis=1)


def _mlstm_params(w_in, b_gate, g_head):
    w_q = w_in[:, 0:A_QK]
    w_k = w_in[:, A_QK:2 * A_QK]
    w_vo = w_in[:, 2 * A_QK:A_GATE_OFF]
    w_fm = jnp.concatenate([w_q, w_vo], axis=1).T.astype(BF16)
    pad = LANES - A_HEADS
    w_g = jnp.concatenate([jnp.pad(w_in[:, A_GATE_OFF:A_GATE_OFF + A_HEADS], ((0, 0), (0, pad))),
                           jnp.pad(w_in[:, A_GATE_OFF + A_HEADS:], ((0, 0), (0, pad)))], axis=1)
    b_g = jnp.concatenate([jnp.pad(b_gate[:A_HEADS], (0, pad)), jnp.pad(b_gate[A_HEADS:], (0, pad))])
    g_head_b = jnp.broadcast_to(g_head.astype(F32)[:, None], (A_V, LANES))
    return w_fm, w_k.astype(BF16), w_g.astype(BF16), b_g.astype(F32).reshape(1, 2 * LANES), g_head_b


def kernel(x, a_w_in, a_b_gate, a_g_head, a_w_out, b_w_q, b_rel_bias, b_w_o, kv_norm_g, w_kv,
           norm_mix_g, norm_ffn_g, ffn_w_gate, ffn_w_up, ffn_w_down, final_norm_g):
    bsz, seq, d = x.shape
    assert d == D_MODEL and seq % ROW_TILE == 0
    depth = norm_mix_g.shape[0]
    n_a = a_w_in.shape[0]

    def row(v):
        return v.reshape(1, -1).astype(F32)

    h = x
    k_pad = v_pad = None
    for l in range(depth):
        if l == n_a:
            k_pad, v_pad = _kv_proj(h, row(kv_norm_g), w_kv.astype(BF16))
        if l < n_a:
            h = _mlstm_layer(h, row(norm_mix_g[l]), *_mlstm_params(a_w_in[l], a_b_gate[l], a_g_head[l]),
                             a_w_out[l].astype(BF16))
        else:
            j = l - n_a
            h = _attn_layer(h, row(norm_mix_g[l]), b_w_q[j].astype(BF16), b_w_o[j].astype(BF16),
                            _rel_bias(b_rel_bias[j]), k_pad, v_pad)
        h = _ffn(h.reshape(bsz * seq, d), row(norm_ffn_g[l]), ffn_w_gate[l].astype(BF16),
                 ffn_w_up[l].astype(BF16), ffn_w_down[l].astype(BF16), row(final_norm_g),
                 final_norm=(l == depth - 1)).reshape(bsz, seq, d)
    return h
```

```python
import functools

import jax
import jax.numpy as jnp
from jax import lax
from jax.experimental import pallas as pl
from jax.experimental.pallas import tpu as pltpu

F32 = jnp.float32
BF16 = jnp.bfloat16

D_MODEL = 1024
CHUNK = 64
A_HEADS = 8
A_DQK = 64
A_DV = 128
A_QK = A_HEADS * A_DQK
A_V = A_HEADS * A_DV
B_HEADS = 16
B_DH = 64
LEFT_CHUNKS = 8
LEFT = LEFT_CHUNKS * CHUNK
BAND = LEFT + CHUNK
MAX_REL = 256
D_FF = 2816
EPS = 1e-6
NEG_INIT = -1e30

LANES = 128
MXU_COLS = 256

ROW_TILE = 512
A_SUPER = 128
FF_TILE = 256
VMEM_LIMIT = 56 * 1024 * 1024

A_GATE_OFF = 2 * A_QK + 2 * A_V


def _rmsnorm(x, g):
    return x * lax.rsqrt(jnp.mean(x * x, axis=-1, keepdims=True) + EPS) * g


def _dot(a, b):
    return jnp.dot(a, b, preferred_element_type=F32)


def _dot_nt(a, b):
    return lax.dot_general(a, b, (((1,), (1,)), ((), ())), preferred_element_type=F32)


def _dot_tn(a, b):
    return lax.dot_general(a, b, (((0,), (0,)), ((), ())), preferred_element_type=F32)


def _const_spec(shape):
    nd = len(shape)
    return pl.BlockSpec(shape, lambda *_: (0,) * nd, pipeline_mode=pl.Buffered(1))


def _ffn_kernel(x_ref, g_ref, wg_ref, wu_ref, wd_ref, gf_ref, o_ref, *, final_norm):
    x = x_ref[...]
    xn = _rmsnorm(x, g_ref[...]).astype(BF16)
    o_ref[...] = x
    for j in range(D_FF // FF_TILE):
        cols = slice(j * FF_TILE, (j + 1) * FF_TILE)
        hg = _dot(xn, wg_ref[:, cols])
        hu = _dot(xn, wu_ref[:, cols])
        act = (hg * jax.nn.sigmoid(hg)) * hu
        o_ref[...] += _dot(act.astype(BF16), wd_ref[cols, :])
    if final_norm:
        o_ref[...] = _rmsnorm(o_ref[...], gf_ref[...])


def _ffn(h2d, g, wg, wu, wd, gf, final_norm):
    n_tok = h2d.shape[0]
    row = pl.BlockSpec((ROW_TILE, D_MODEL), lambda i: (i, 0))
    return pl.pallas_call(
        functools.partial(_ffn_kernel, final_norm=final_norm),
        grid=(n_tok // ROW_TILE,),
        in_specs=[row, _const_spec((1, D_MODEL)), _const_spec((D_MODEL, D_FF)),
                  _const_spec((D_MODEL, D_FF)), _const_spec((D_FF, D_MODEL)),
                  _const_spec((1, D_MODEL))],
        out_specs=row,
        out_shape=jax.ShapeDtypeStruct(h2d.shape, F32),
        compiler_params=pltpu.CompilerParams(
            dimension_semantics=("arbitrary",), vmem_limit_bytes=VMEM_LIMIT),
        name="ffn",
    )(h2d, g, wg, wu, wd, gf)


A_CEXT = A_DV + 16
A_STEPS = ROW_TILE // A_SUPER


def _log_sigmoid(x):
    return jnp.minimum(x, 0.0) - jnp.log1p(jnp.exp(-jnp.abs(x)))


def _mlstm_kernel(x_ref, g_ref, wfm_ref, wk_ref, wg_ref, bg_ref, ghb_ref, wout_ref, o_ref,
                  qt_s, k_s, vt_s, ogt_s, hst_s, svt_s, up_s, st_s, c_s, m_s):
    t = pl.program_id(1)

    @pl.when(t == 0)
    def _():
        c_s[...] = jnp.zeros_like(c_s)
        m_s[...] = jnp.full_like(m_s, NEG_INIT)

    x = x_ref[0]
    xn = _rmsnorm(x, g_ref[...]).astype(BF16)
    qt_s[...] = _dot_nt(wfm_ref[0:A_QK, :], xn).astype(BF16)
    vt_s[...] = _dot_nt(wfm_ref[A_QK:A_QK + A_V, :], xn).astype(BF16)
    ogt_s[...] = jax.nn.sigmoid(_dot_nt(wfm_ref[A_QK + A_V:A_QK + 2 * A_V, :], xn))
    k_s[...] = (_dot(xn, wk_ref[...]) * (A_DQK ** -0.5)).astype(BF16)
    gates = _dot(xn, wg_ref[...]) + bg_ref[...]
    i_log = gates[:, 0:LANES]
    f_log = _log_sigmoid(gates[:, LANES:2 * LANES])

    rows = lax.broadcasted_iota(jnp.int32, (ROW_TILE, LANES), 0) & (A_SUPER - 1)
    b_cum = f_log
    shift = 1
    while shift < A_SUPER:
        rolled = pltpu.roll(b_cum, shift, axis=0)
        b_cum = b_cum + jnp.where(rows >= shift, rolled, 0.0)
        shift *= 2
    imb = i_log - b_cum
    b_t = b_cum.T
    imb_t = imb.T

    row_i = lax.broadcasted_iota(jnp.int32, (A_SUPER, A_SUPER), 0)
    col_i = lax.broadcasted_iota(jnp.int32, (A_SUPER, A_SUPER), 1)
    causal = col_i >= row_i
    ext_row = lax.broadcasted_iota(jnp.int32, (A_CEXT - A_DV, LANES), 0) == 0
    c_lane = lax.broadcasted_iota(jnp.int32, (A_CEXT, LANES), 1)

    for c in range(A_STEPS):
        r = slice(c * A_SUPER, (c + 1) * A_SUPER)
        q_masked = []
        for h in range(A_HEADS):
            p, half = divmod(h, 2)
            pr = slice(p * LANES, (p + 1) * LANES)
            hr = slice(h * A_DV, (h + 1) * A_DV)
            own_rows = (row_i < A_DQK) if half == 0 else (row_i >= A_DQK)
            qp = qt_s[pr, r]
            qtm = jnp.where(own_rows, qp, jnp.zeros_like(qp))
            q_masked.append(qtm)
            kp = k_s[r, pr]
            vt_h = vt_s[hr, r]
            b_row = b_t[h:h + 1, r]
            imb_row = imb_t[h:h + 1, r]
            imb_col = imb[r, h:h + 1]
            b_last = jnp.broadcast_to(b_row[:, A_SUPER - 1:A_SUPER], (1, LANES))

            log_dt = jnp.where(causal, b_row + imb_col, -jnp.inf)
            mrow = jnp.max(log_dt, axis=0, keepdims=True)
            st = _dot(kp, qtm) * jnp.exp(log_dt - mrow)
            svt_s[hr, r] = _dot(vt_h, st.astype(BF16))

            a_row = b_last + imb_row
            amax = jnp.broadcast_to(jnp.max(a_row, axis=1, keepdims=True), (1, LANES))
            wa = jnp.exp(a_row - amax)
            vw = (vt_h.astype(F32) * wa).astype(BF16)
            ext = jnp.where(ext_row, jnp.broadcast_to(wa, (A_CEXT - A_DV, LANES)), 0.0).astype(BF16)
            up_s[c, h] = _dot(jnp.concatenate([vw, ext], axis=0), kp)

            st_s[c * A_HEADS + h, 0:1, :] = mrow
            st_s[c * A_HEADS + h, 1:2, :] = jnp.sum(st, axis=0, keepdims=True)
            st_s[c * A_HEADS + h, 2:3, :] = amax
            st_s[c * A_HEADS + h, 3:4, :] = b_last

        for h in range(A_HEADS):
            half = h % 2
            hr = slice(h * A_DV, (h + 1) * A_DV)
            own_lanes = (c_lane < A_DQK) if half == 0 else (c_lane >= A_DQK)
            b_row = b_t[h:h + 1, r]
            mrow = st_s[c * A_HEADS + h, 0:1, :]
            rs = st_s[c * A_HEADS + h, 1:2, :]
            amax = st_s[c * A_HEADS + h, 2:3, :]
            b_last = st_s[c * A_HEADS + h, 3:4, :]
            m_prev = m_s[h]

            inter = b_row + m_prev
            m_j = jnp.maximum(inter, mrow)
            f_intra = jnp.exp(mrow - m_j)
            w_inter = jnp.exp(inter - m_j)
            qc = _dot(c_s[h].astype(BF16), q_masked[h])
            num = f_intra * svt_s[hr, r] + w_inter * qc[0:A_DV]
            den = f_intra * rs + w_inter * qc[A_DV:A_DV + 1]
            ht = num * (1.0 / jnp.maximum(jnp.abs(den), jnp.exp(-m_j)))
            hn = ht * lax.rsqrt(jnp.mean(ht * ht, axis=0, keepdims=True) + EPS)
            hst_s[hr, r] = (ogt_s[hr, r] * (hn * ghb_ref[hr, :])).astype(BF16)

            m_new = jnp.maximum(b_last + m_prev, amax)
            decay = jnp.exp(b_last + m_prev - m_new)
            grow = jnp.exp(amax - m_new)
            c_s[h] = jnp.where(own_lanes, decay * c_s[h] + grow * up_s[c, h], 0.0)
            m_s[h] = m_new

    o_ref[0] = x + _dot_tn(hst_s[...], wout_ref[...])


def _mlstm_layer(h, g, w_fm, w_k, w_g, b_g, g_head_b, w_out):
    bsz, seq, _ = h.shape
    row = pl.BlockSpec((1, ROW_TILE, D_MODEL), lambda b, t: (b, t, 0))
    return pl.pallas_call(
        _mlstm_kernel,
        grid=(bsz, seq // ROW_TILE),
        in_specs=[row, _const_spec((1, D_MODEL)), _const_spec((A_QK + 2 * A_V, D_MODEL)),
                  _const_spec((D_MODEL, A_QK)), _const_spec((D_MODEL, 2 * LANES)),
                  _const_spec((1, 2 * LANES)), _const_spec((A_V, LANES)), _const_spec((A_V, D_MODEL))],
        out_specs=row,
        out_shape=jax.ShapeDtypeStruct(h.shape, F32),
        scratch_shapes=[
            pltpu.VMEM((A_QK, ROW_TILE), BF16),
            pltpu.VMEM((ROW_TILE, A_QK), BF16),
            pltpu.VMEM((A_V, ROW_TILE), BF16),
            pltpu.VMEM((A_V, ROW_TILE), F32),
            pltpu.VMEM((A_V, ROW_TILE), BF16),
            pltpu.VMEM((A_V, ROW_TILE), F32),
            pltpu.VMEM((A_STEPS, A_HEADS, A_CEXT, LANES), F32),
            pltpu.VMEM((A_STEPS * A_HEADS, 8, LANES), F32),
            pltpu.VMEM((A_HEADS, A_CEXT, LANES), F32),
            pltpu.VMEM((A_HEADS, 1, LANES), F32),
        ],
        compiler_params=pltpu.CompilerParams(
            dimension_semantics=("arbitrary", "arbitrary"), vmem_limit_bytes=VMEM_LIMIT),
        name="mlstm",
    )(h, g, w_fm, w_k, w_g, b_g, g_head_b, w_out)


def _kv_kernel(x_ref, g_ref, w_ref, k_ref, v_ref):
    t = pl.program_id(1)

    @pl.when(t == 0)
    def _():
        k_ref[...] = jnp.zeros_like(k_ref)
        v_ref[...] = jnp.zeros_like(v_ref)

    @pl.when(t > 0)
    def _():
        xn = _rmsnorm(x_ref[0], g_ref[...]).astype(BF16)
        k_ref[0] = _dot(xn, w_ref[:, 0:D_MODEL]).astype(BF16)
        v_ref[0] = _dot(xn, w_ref[:, D_MODEL:2 * D_MODEL]).astype(BF16)


def _kv_proj(h, g, w_kv):
    bsz, seq, _ = h.shape
    assert LEFT == ROW_TILE
    out = pl.BlockSpec((1, ROW_TILE, D_MODEL), lambda b, t: (b, t, 0))
    shape = jax.ShapeDtypeStruct((bsz, seq + LEFT, D_MODEL), BF16)
    return pl.pallas_call(
        _kv_kernel,
        grid=(bsz, seq // ROW_TILE + 1),
        in_specs=[pl.BlockSpec((1, ROW_TILE, D_MODEL), lambda b, t: (b, jnp.maximum(t - 1, 0), 0)),
                  _const_spec((1, D_MODEL)), _const_spec((D_MODEL, 2 * D_MODEL))],
        out_specs=[out, out],
        out_shape=[shape, shape],
        compiler_params=pltpu.CompilerParams(
            dimension_semantics=("arbitrary", "arbitrary"), vmem_limit_bytes=VMEM_LIMIT),
        name="kv_proj",
    )(h, g, w_kv)


Q_BLOCK = 2 * CHUNK
K_BLOCK = LEFT + Q_BLOCK


def _attn_kernel(x_ref, g_ref, wq_ref, wo_ref, bias_ref, k_ref, v_ref, o_ref,
                 q_s, a_s, s_s, p_s, m_s, l_s):
    t = pl.program_id(1)
    x = x_ref[0]
    xn = _rmsnorm(x, g_ref[...]).astype(BF16)
    q_s[...] = (_dot(xn, wq_ref[...]) * (B_DH ** -0.5)).astype(BF16)

    lane = lax.broadcasted_iota(jnp.int32, (Q_BLOCK, LANES), 1)
    lo = lane < B_DH
    n_tiles = K_BLOCK // LANES
    n_blocks = ROW_TILE // Q_BLOCK
    n_pairs = B_HEADS // 2
    left_tiles = LEFT // LANES

    def band(i, first_tile):
        first = left_tiles - i if first_tile else 0
        rows = pl.ds(pl.multiple_of(t * ROW_TILE, ROW_TILE) + (i * Q_BLOCK + first * LANES),
                     (n_tiles - first) * LANES)
        return first, rows, slice(first * LANES, K_BLOCK)

    def scores(i, p, first_tile):
        pl_ = slice(p * LANES, (p + 1) * LANES)
        _, rows, cols = band(i, first_tile)
        qp = q_s[i * Q_BLOCK:(i + 1) * Q_BLOCK, pl_]
        kp = k_ref[0, rows, pl_]
        for half in range(2):
            h = 2 * p + half
            own = lo if half == 0 else jnp.logical_not(lo)
            s = _dot_nt(jnp.where(own, qp, jnp.zeros_like(qp)), kp) + bias_ref[h, :, cols]
            s_s[h, :, cols] = s
            m_s[h] = jnp.broadcast_to(jnp.max(s, axis=1, keepdims=True), (Q_BLOCK, LANES))

    def exps(i, p, first_tile):
        first, _, _ = band(i, first_tile)
        for h in (2 * p, 2 * p + 1):
            m = m_s[h]
            acc = jnp.zeros((Q_BLOCK, LANES), F32)
            for j in range(first, n_tiles):
                tl = slice(j * LANES, (j + 1) * LANES)
                e = jnp.exp(s_s[h, :, tl] - m)
                acc = acc + e
                p_s[h, :, tl] = e.astype(BF16)
            l_s[h] = jnp.broadcast_to(jnp.sum(acc, axis=1, keepdims=True), (Q_BLOCK, LANES))

    def values(i, p, first_tile):
        pl_ = slice(p * LANES, (p + 1) * LANES)
        _, rows, cols = band(i, first_tile)
        vp = v_ref[0, rows, pl_]
        o0 = _dot(p_s[2 * p, :, cols], vp) / l_s[2 * p]
        o1 = _dot(p_s[2 * p + 1, :, cols], vp) / l_s[2 * p + 1]
        a_s[i * Q_BLOCK:(i + 1) * Q_BLOCK, pl_] = jnp.where(lo, o0, o1).astype(BF16)

    def block(i, first_tile):
        for stage in (scores, exps, values):
            for p in range(n_pairs):
                stage(i, p, first_tile)

    for i in range(n_blocks):
        pl.when(t == 0)(functools.partial(block, i, True))
        pl.when(t > 0)(functools.partial(block, i, False))

    o_ref[0] = x + _dot(a_s[...], wo_ref[...])


def _attn_layer(h, g, w_q, w_o, bias, k_pad, v_pad):
    bsz, seq, _ = h.shape
    row = pl.BlockSpec((1, ROW_TILE, D_MODEL), lambda b, t: (b, t, 0))
    kv = pl.BlockSpec((1, seq + LEFT, D_MODEL), lambda b, t: (b, 0, 0))
    return pl.pallas_call(
        _attn_kernel,
        grid=(bsz, seq // ROW_TILE),
        in_specs=[row, _const_spec((1, D_MODEL)), _const_spec((D_MODEL, D_MODEL)),
                  _const_spec((D_MODEL, D_MODEL)), _const_spec((B_HEADS, Q_BLOCK, K_BLOCK)), kv, kv],
        out_specs=row,
        out_shape=jax.ShapeDtypeStruct(h.shape, F32),
        scratch_shapes=[pltpu.VMEM((ROW_TILE, D_MODEL), BF16),
                        pltpu.VMEM((ROW_TILE, D_MODEL), BF16),
                        pltpu.VMEM((B_HEADS, Q_BLOCK, K_BLOCK), F32),
                        pltpu.VMEM((B_HEADS, Q_BLOCK, K_BLOCK), BF16),
                        pltpu.VMEM((B_HEADS, Q_BLOCK, LANES), F32),
                        pltpu.VMEM((B_HEADS, Q_BLOCK, LANES), F32)],
        compiler_params=pltpu.CompilerParams(
            dimension_semantics=("arbitrary", "arbitrary"), vmem_limit_bytes=VMEM_LIMIT),
        name="attn",
    )(h, g, w_q, w_o, bias, k_pad, v_pad)


def _rel_bias(rel_table):
    n_heads = rel_table.shape[0]
    tab = rel_table.astype(F32)
    span = BAND + CHUNK - 1
    n_var = MAX_REL + CHUNK
    base = jnp.concatenate(
        [tab[:, MAX_REL - (CHUNK - 1):2 * MAX_REL + 1],
         jnp.broadcast_to(tab[:, 2 * MAX_REL:], (n_heads, span - n_var))], axis=1)
    rev = base[:, ::-1]
    padded = jnp.pad(rev, ((0, 0), (0, 1)))
    rows = jnp.tile(padded, (1, CHUNK))[:, :CHUNK * span].reshape(n_heads, CHUNK, span)
    band = rows[:, :, CHUNK - 1:CHUNK - 1 + BAND]
    ninf = jnp.full((n_heads, CHUNK, CHUNK), -jnp.inf, F32)
    return jnp.concatenate([jnp.concatenate([band, ninf], axis=2),
                            jnp.concatenate([ninf, band], axis=2)], axis=1)


def _mlstm_params(w_in, b_gate, g_head):
    w_q = w_in[:, 0:A_QK]
    w_k = w_in[:, A_QK:2 * A_QK]
    w_vo = w_in[:, 2 * A_QK:A_GATE_OFF]
    w_fm = jnp.concatenate([w_q, w_vo], axis=1).T.astype(BF16)
    pad = LANES - A_HEADS
    w_g = jnp.concatenate([jnp.pad(w_in[:, A_GATE_OFF:A_GATE_OFF + A_HEADS], ((0, 0), (0, pad))),
                           jnp.pad(w_in[:, A_GATE_OFF + A_HEADS:], ((0, 0), (0, pad)))], axis=1)
    b_g = jnp.concatenate([jnp.pad(b_gate[:A_HEADS], (0, pad)), jnp.pad(b_gate[A_HEADS:], (0, pad))])
    g_head_b = jnp.broadcast_to(g_head.astype(F32)[:, None], (A_V, LANES))
    return w_fm, w_k.astype(BF16), w_g.astype(BF16), b_g.astype(F32).reshape(1, 2 * LANES), g_head_b


def kernel(x, a_w_in, a_b_gate, a_g_head, a_w_out, b_w_q, b_rel_bias, b_w_o, kv_norm_g, w_kv,
           norm_mix_g, norm_ffn_g, ffn_w_gate, ffn_w_up, ffn_w_down, final_norm_g):
    bsz, seq, d = x.shape
    assert d == D_MODEL and seq % ROW_TILE == 0
    depth = norm_mix_g.shape[0]
    n_a = a_w_in.shape[0]

    def row(v):
        return v.reshape(1, -1).astype(F32)

    h = x
    k_pad = v_pad = None
    for l in range(depth):
        if l == n_a:
            k_pad, v_pad = _kv_proj(h, row(kv_norm_g), w_kv.astype(BF16))
        if l < n_a:
            h = _mlstm_layer(h, row(norm_mix_g[l]), *_mlstm_params(a_w_in[l], a_b_gate[l], a_g_head[l]),
                             a_w_out[l].astype(BF16))
        else:
            j = l - n_a
            h = _attn_layer(h, row(norm_mix_g[l]), b_w_q[j].astype(BF16), b_w_o[j].astype(BF16),
                            _rel_bias(b_rel_bias[j]), k_pad, v_pad)
        h = _ffn(h.reshape(bsz * seq, d), row(norm_ffn_g[l]), ffn_w_gate[l].astype(BF16),
                 ffn_w_up[l].astype(BF16), ffn_w_down[l].astype(BF16), row(final_norm_g),
                 final_norm=(l == depth - 1)).reshape(bsz, seq, d)
    return h
```

```python
import functools

import jax
import jax.numpy as jnp
from jax import lax
from jax.experimental import pallas as pl
from jax.experimental.pallas import tpu as pltpu

F32 = jnp.float32
BF16 = jnp.bfloat16

D_MODEL = 1024
CHUNK = 64
A_HEADS = 8
A_DQK = 64
A_DV = 128
A_QK = A_HEADS * A_DQK
A_V = A_HEADS * A_DV
B_HEADS = 16
B_DH = 64
LEFT_CHUNKS = 8
LEFT = LEFT_CHUNKS * CHUNK
BAND = LEFT + CHUNK
MAX_REL = 256
D_FF = 2816
EPS = 1e-6
NEG_INIT = -1e30

LANES = 128
MXU_COLS = 256

ROW_TILE = 512
A_SUPER = 128
FF_TILE = 256
VMEM_LIMIT = 56 * 1024 * 1024

A_GATE_OFF = 2 * A_QK + 2 * A_V


def _rmsnorm(x, g):
    return x * lax.rsqrt(jnp.mean(x * x, axis=-1, keepdims=True) + EPS) * g


def _dot(a, b):
    return jnp.dot(a, b, preferred_element_type=F32)


def _dot_nt(a, b):
    return lax.dot_general(a, b, (((1,), (1,)), ((), ())), preferred_element_type=F32)


def _dot_tn(a, b):
    return lax.dot_general(a, b, (((0,), (0,)), ((), ())), preferred_element_type=F32)


def _const_spec(shape):
    nd = len(shape)
    return pl.BlockSpec(shape, lambda *_: (0,) * nd, pipeline_mode=pl.Buffered(1))


def _ffn_kernel(x_ref, g_ref, wg_ref, wu_ref, wd_ref, gf_ref, o_ref, *, final_norm):
    x = x_ref[...]
    xn = _rmsnorm(x, g_ref[...]).astype(BF16)
    o_ref[...] = x
    for j in range(D_FF // FF_TILE):
        cols = slice(j * FF_TILE, (j + 1) * FF_TILE)
        hg = _dot(xn, wg_ref[:, cols])
        hu = _dot(xn, wu_ref[:, cols])
        act = (hg * jax.nn.sigmoid(hg)) * hu
        o_ref[...] += _dot(act.astype(BF16), wd_ref[cols, :])
    if final_norm:
        o_ref[...] = _rmsnorm(o_ref[...], gf_ref[...])


def _ffn(h2d, g, wg, wu, wd, gf, final_norm):
    n_tok = h2d.shape[0]
    row = pl.BlockSpec((ROW_TILE, D_MODEL), lambda i: (i, 0))
    return pl.pallas_call(
        functools.partial(_ffn_kernel, final_norm=final_norm),
        grid=(n_tok // ROW_TILE,),
        in_specs=[row, _const_spec((1, D_MODEL)), _const_spec((D_MODEL, D_FF)),
                  _const_spec((D_MODEL, D_FF)), _const_spec((D_FF, D_MODEL)),
                  _const_spec((1, D_MODEL))],
        out_specs=row,
        out_shape=jax.ShapeDtypeStruct(h2d.shape, F32),
        compiler_params=pltpu.CompilerParams(
            dimension_semantics=("arbitrary",), vmem_limit_bytes=VMEM_LIMIT),
        name="ffn",
    )(h2d, g, wg, wu, wd, gf)


A_CEXT = A_DV + 16
A_STEPS = ROW_TILE // A_SUPER


def _log_sigmoid(x):
    return jnp.minimum(x, 0.0) - jnp.log1p(jnp.exp(-jnp.abs(x)))


def _mlstm_kernel(x_ref, g_ref, wfm_ref, wk_ref, wg_ref, bg_ref, ghb_ref, wout_ref, o_ref,
                  qt_s, k_s, vt_s, ogt_s, hst_s, svt_s, up_s, st_s, c_s, m_s):
    t = pl.program_id(1)

    @pl.when(t == 0)
    def _():
        c_s[...] = jnp.zeros_like(c_s)
        m_s[...] = jnp.full_like(m_s, NEG_INIT)

    x = x_ref[0]
    xn = _rmsnorm(x, g_ref[...]).astype(BF16)
    qt_s[...] = _dot_nt(wfm_ref[0:A_QK, :], xn).astype(BF16)
    vt_s[...] = _dot_nt(wfm_ref[A_QK:A_QK + A_V, :], xn).astype(BF16)
    ogt_s[...] = jax.nn.sigmoid(_dot_nt(wfm_ref[A_QK + A_V:A_QK + 2 * A_V, :], xn))
    k_s[...] = (_dot(xn, wk_ref[...]) * (A_DQK ** -0.5)).astype(BF16)
    gates = _dot(xn, wg_ref[...]) + bg_ref[...]
    i_log = gates[:, 0:LANES]
    f_log = _log_sigmoid(gates[:, LANES:2 * LANES])

    rows = lax.broadcasted_iota(jnp.int32, (ROW_TILE, LANES), 0) & (A_SUPER - 1)
    b_cum = f_log
    shift = 1
    while shift < A_SUPER:
        rolled = pltpu.roll(b_cum, shift, axis=0)
        b_cum = b_cum + jnp.where(rows >= shift, rolled, 0.0)
        shift *= 2
    imb = i_log - b_cum
    b_t = b_cum.T
    imb_t = imb.T

    row_i = lax.broadcasted_iota(jnp.int32, (A_SUPER, A_SUPER), 0)
    col_i = lax.broadcasted_iota(jnp.int32, (A_SUPER, A_SUPER), 1)
    causal = col_i >= row_i
    ext_row = lax.broadcasted_iota(jnp.int32, (A_CEXT - A_DV, LANES), 0) == 0
    c_lane = lax.broadcasted_iota(jnp.int32, (A_CEXT, LANES), 1)

    for c in range(A_STEPS):
        r = slice(c * A_SUPER, (c + 1) * A_SUPER)
        q_masked = []
        for h in range(A_HEADS):
            p, half = divmod(h, 2)
            pr = slice(p * LANES, (p + 1) * LANES)
            hr = slice(h * A_DV, (h + 1) * A_DV)
            own_rows = (row_i < A_DQK) if half == 0 else (row_i >= A_DQK)
            qp = qt_s[pr, r]
            qtm = jnp.where(own_rows, qp, jnp.zeros_like(qp))
            q_masked.append(qtm)
            kp = k_s[r, pr]
            vt_h = vt_s[hr, r]
            b_row = b_t[h:h + 1, r]
            imb_row = imb_t[h:h + 1, r]
            imb_col = imb[r, h:h + 1]
            b_last = jnp.broadcast_to(b_row[:, A_SUPER - 1:A_SUPER], (1, LANES))

            log_dt = jnp.where(causal, b_row + imb_col, -jnp.inf)
            mrow = jnp.max(log_dt, axis=0, keepdims=True)
            st = _dot(kp, qtm) * jnp.exp(log_dt - mrow)
            svt_s[hr, r] = _dot(vt_h, st.astype(BF16))

            a_row = b_last + imb_row
            amax = jnp.broadcast_to(jnp.max(a_row, axis=1, keepdims=True), (1, LANES))
            wa = jnp.exp(a_row - amax)
            vw = (vt_h.astype(F32) * wa).astype(BF16)
            ext = jnp.where(ext_row, jnp.broadcast_to(wa, (A_CEXT - A_DV, LANES)), 0.0).astype(BF16)
            up_s[c, h] = _dot(jnp.concatenate([vw, ext], axis=0), kp)

            st_s[c * A_HEADS + h, 0:1, :] = mrow
            st_s[c * A_HEADS + h, 1:2, :] = jnp.sum(st, axis=0, keepdims=True)
            st_s[c * A_HEADS + h, 2:3, :] = amax
            st_s[c * A_HEADS + h, 3:4, :] = b_last

        for h in range(A_HEADS):
            half = h % 2
            hr = slice(h * A_DV, (h + 1) * A_DV)
            own_lanes = (c_lane < A_DQK) if half == 0 else (c_lane >= A_DQK)
            b_row = b_t[h:h + 1, r]
            mrow = st_s[c * A_HEADS + h, 0:1, :]
            rs = st_s[c * A_HEADS + h, 1:2, :]
            amax = st_s[c * A_HEADS + h, 2:3, :]
            b_last = st_s[c * A_HEADS + h, 3:4, :]
            m_prev = m_s[h]

            inter = b_row + m_prev
            m_j = jnp.maximum(inter, mrow)
            f_intra = jnp.exp(mrow - m_j)
            w_inter = jnp.exp(inter - m_j)
            qc = _dot(c_s[h].astype(BF16), q_masked[h])
            num = f_intra * svt_s[hr, r] + w_inter * qc[0:A_DV]
            den = f_intra * rs + w_inter * qc[A_DV:A_DV + 1]
            ht = num * (1.0 / jnp.maximum(jnp.abs(den), jnp.exp(-m_j)))
            hn = ht * lax.rsqrt(jnp.mean(ht * ht, axis=0, keepdims=True) + EPS)
            hst_s[hr, r] = (ogt_s[hr, r] * (hn * ghb_ref[hr, :])).astype(BF16)

            m_new = jnp.maximum(b_last + m_prev, amax)
            decay = jnp.exp(b_last + m_prev - m_new)
            grow = jnp.exp(amax - m_new)
            c_s[h] = jnp.where(own_lanes, decay * c_s[h] + grow * up_s[c, h], 0.0)
            m_s[h] = m_new

    o_ref[0] = x + _dot_tn(hst_s[...], wout_ref[...])


def _mlstm_layer(h, g, w_fm, w_k, w_g, b_g, g_head_b, w_out):
    bsz, seq, _ = h.shape
    row = pl.BlockSpec((1, ROW_TILE, D_MODEL), lambda b, t: (b, t, 0))
    return pl.pallas_call(
        _mlstm_kernel,
        grid=(bsz, seq // ROW_TILE),
        in_specs=[row, _const_spec((1, D_MODEL)), _const_spec((A_QK + 2 * A_V, D_MODEL)),
                  _const_spec((D_MODEL, A_QK)), _const_spec((D_MODEL, 2 * LANES)),
                  _const_spec((1, 2 * LANES)), _const_spec((A_V, LANES)), _const_spec((A_V, D_MODEL))],
        out_specs=row,
        out_shape=jax.ShapeDtypeStruct(h.shape, F32),
        scratch_shapes=[
            pltpu.VMEM((A_QK, ROW_TILE), BF16),
            pltpu.VMEM((ROW_TILE, A_QK), BF16),
            pltpu.VMEM((A_V, ROW_TILE), BF16),
            pltpu.VMEM((A_V, ROW_TILE), F32),
            pltpu.VMEM((A_V, ROW_TILE), BF16),
            pltpu.VMEM((A_V, ROW_TILE), F32),
            pltpu.VMEM((A_STEPS, A_HEADS, A_CEXT, LANES), F32),
            pltpu.VMEM((A_STEPS * A_HEADS, 8, LANES), F32),
            pltpu.VMEM((A_HEADS, A_CEXT, LANES), F32),
            pltpu.VMEM((A_HEADS, 1, LANES), F32),
        ],
        compiler_params=pltpu.CompilerParams(
            dimension_semantics=("arbitrary", "arbitrary"), vmem_limit_bytes=VMEM_LIMIT),
        name="mlstm",
    )(h, g, w_fm, w_k, w_g, b_g, g_head_b, w_out)


def _kv_kernel(x_ref, g_ref, wkt_ref, wv_ref, kt_ref, v_ref):
    xn = _rmsnorm(x_ref[0], g_ref[...]).astype(BF16)
    kt_ref[0] = _dot_nt(wkt_ref[...], xn).astype(BF16)
    v_ref[0] = _dot(xn, wv_ref[...]).astype(BF16)


def _kv_proj(h, g, w_kt, w_v):
    bsz, seq, _ = h.shape
    row = pl.BlockSpec((1, ROW_TILE, D_MODEL), lambda b, t: (b, t, 0))
    col = pl.BlockSpec((1, D_MODEL, ROW_TILE), lambda b, t: (b, 0, t))
    return pl.pallas_call(
        _kv_kernel,
        grid=(bsz, seq // ROW_TILE),
        in_specs=[row, _const_spec((1, D_MODEL)), _const_spec((D_MODEL, D_MODEL)),
                  _const_spec((D_MODEL, D_MODEL))],
        out_specs=[col, row],
        out_shape=[jax.ShapeDtypeStruct((bsz, D_MODEL, seq), BF16),
                   jax.ShapeDtypeStruct((bsz, seq, D_MODEL), BF16)],
        compiler_params=pltpu.CompilerParams(
            dimension_semantics=("arbitrary", "arbitrary"), vmem_limit_bytes=VMEM_LIMIT),
        name="kv_proj",
    )(h, g, w_kt, w_v)


Q_BLOCK = 2 * CHUNK
K_BLOCK = LEFT + Q_BLOCK


def _attn_kernel(x_ref, g_ref, wq_ref, wo_ref, bias_ref, kt_ref, v_ref, o_ref,
                 q_s, a_s, s_s, p_s, m_s, l_s):
    t = pl.program_id(1)
    x = x_ref[0]
    xn = _rmsnorm(x, g_ref[...]).astype(BF16)
    q_s[...] = (_dot(xn, wq_ref[...]) * (B_DH ** -0.5)).astype(BF16)

    lane = lax.broadcasted_iota(jnp.int32, (Q_BLOCK, LANES), 1)
    lo = lane < B_DH
    n_tiles = K_BLOCK // LANES
    n_blocks = ROW_TILE // Q_BLOCK
    n_pairs = B_HEADS // 2
    left_tiles = LEFT // LANES

    def band(i, first_tile):
        if first_tile:
            first = left_tiles - i
            keys = pl.ds(0, (n_tiles - first) * LANES)
        else:
            first = 0
            keys = pl.ds(pl.multiple_of((t - 1) * ROW_TILE, ROW_TILE) + i * Q_BLOCK, K_BLOCK)
        return first, keys, slice(first * LANES, K_BLOCK)

    def scores(i, p, first_tile):
        pl_ = slice(p * LANES, (p + 1) * LANES)
        _, keys, cols = band(i, first_tile)
        qp = q_s[i * Q_BLOCK:(i + 1) * Q_BLOCK, pl_]
        kp = kt_ref[0, pl_, keys]
        for half in range(2):
            h = 2 * p + half
            own = lo if half == 0 else jnp.logical_not(lo)
            s = _dot(jnp.where(own, qp, jnp.zeros_like(qp)), kp) + bias_ref[h, :, cols]
            s_s[h, :, cols] = s
            m_s[h] = jnp.broadcast_to(jnp.max(s, axis=1, keepdims=True), (Q_BLOCK, LANES))

    def exps(i, p, first_tile):
        first, _, _ = band(i, first_tile)
        for h in (2 * p, 2 * p + 1):
            m = m_s[h]
            acc = jnp.zeros((Q_BLOCK, LANES), F32)
            for j in range(first, n_tiles):
                tl = slice(j * LANES, (j + 1) * LANES)
                e = jnp.exp(s_s[h, :, tl] - m)
                acc = acc + e
                p_s[h, :, tl] = e.astype(BF16)
            l_s[h] = jnp.broadcast_to(jnp.sum(acc, axis=1, keepdims=True), (Q_BLOCK, LANES))

    def values(i, p, first_tile):
        pl_ = slice(p * LANES, (p + 1) * LANES)
        _, keys, cols = band(i, first_tile)
        vp = v_ref[0, keys, pl_]
        o0 = _dot(p_s[2 * p, :, cols], vp) / l_s[2 * p]
        o1 = _dot(p_s[2 * p + 1, :, cols], vp) / l_s[2 * p + 1]
        a_s[i * Q_BLOCK:(i + 1) * Q_BLOCK, pl_] = jnp.where(lo, o0, o1).astype(BF16)

    def block(i, first_tile):
        for stage in (scores, exps, values):
            for p in range(n_pairs):
                stage(i, p, first_tile)

    for i in range(n_blocks):
        pl.when(t == 0)(functools.partial(block, i, True))
        pl.when(t > 0)(functools.partial(block, i, False))

    o_ref[0] = x + _dot(a_s[...], wo_ref[...])


def _attn_layer(h, g, w_q, w_o, bias, k_t, v):
    bsz, seq, _ = h.shape
    assert LEFT == ROW_TILE
    row = pl.BlockSpec((1, ROW_TILE, D_MODEL), lambda b, t: (b, t, 0))
    kt_all = pl.BlockSpec((1, D_MODEL, seq), lambda b, t: (b, 0, 0))
    v_all = pl.BlockSpec((1, seq, D_MODEL), lambda b, t: (b, 0, 0))
    return pl.pallas_call(
        _attn_kernel,
        grid=(bsz, seq // ROW_TILE),
        in_specs=[row, _const_spec((1, D_MODEL)), _const_spec((D_MODEL, D_MODEL)),
                  _const_spec((D_MODEL, D_MODEL)), _const_spec((B_HEADS, Q_BLOCK, K_BLOCK)), kt_all, v_all],
        out_specs=row,
        out_shape=jax.ShapeDtypeStruct(h.shape, F32),
        scratch_shapes=[pltpu.VMEM((ROW_TILE, D_MODEL), BF16),
                        pltpu.VMEM((ROW_TILE, D_MODEL), BF16),
                        pltpu.VMEM((B_HEADS, Q_BLOCK, K_BLOCK), F32),
                        pltpu.VMEM((B_HEADS, Q_BLOCK, K_BLOCK), BF16),
                        pltpu.VMEM((B_HEADS, Q_BLOCK, LANES), F32),
                        pltpu.VMEM((B_HEADS, Q_BLOCK, LANES), F32)],
        compiler_params=pltpu.CompilerParams(
            dimension_semantics=("arbitrary", "arbitrary"), vmem_limit_bytes=VMEM_LIMIT),
        name="attn",
    )(h, g, w_q, w_o, bias, k_t, v)


def _rel_bias(rel_table):
    n_heads = rel_table.shape[0]
    tab = rel_table.astype(F32)
    span = BAND + CHUNK - 1
    n_var = MAX_REL + CHUNK
    base = jnp.concatenate(
        [tab[:, MAX_REL - (CHUNK - 1):2 * MAX_REL + 1],
         jnp.broadcast_to(tab[:, 2 * MAX_REL:], (n_heads, span - n_var))], axis=1)
    rev = base[:, ::-1]
    padded = jnp.pad(rev, ((0, 0), (0, 1)))
    rows = jnp.tile(padded, (1, CHUNK))[:, :CHUNK * span].reshape(n_heads, CHUNK, span)
    band = rows[:, :, CHUNK - 1:CHUNK - 1 + BAND]
    ninf = jnp.full((n_heads, CHUNK, CHUNK), -jnp.inf, F32)
    return jnp.concatenate([jnp.concatenate([band, ninf], axis=2),
                            jnp.concatenate([ninf, band], axis=2)], axis=1)


def _mlstm_params(w_in, b_gate, g_head):
    w_q = w_in[:, 0:A_QK]
    w_k = w_in[:, A_QK:2 * A_QK]
    w_vo = w_in[:, 2 * A_QK:A_GATE_OFF]
    w_fm = jnp.concatenate([w_q, w_vo], axis=1).T.astype(BF16)
    pad = LANES - A_HEADS
    w_g = jnp.concatenate([jnp.pad(w_in[:, A_GATE_OFF:A_GATE_OFF + A_HEADS], ((0, 0), (0, pad))),
                           jnp.pad(w_in[:, A_GATE_OFF + A_HEADS:], ((0, 0), (0, pad)))], axis=1)
    b_g = jnp.concatenate([jnp.pad(b_gate[:A_HEADS], (0, pad)), jnp.pad(b_gate[A_HEADS:], (0, pad))])
    g_head_b = jnp.broadcast_to(g_head.astype(F32)[:, None], (A_V, LANES))
    return w_fm, w_k.astype(BF16), w_g.astype(BF16), b_g.astype(F32).reshape(1, 2 * LANES), g_head_b


def kernel(x, a_w_in, a_b_gate, a_g_head, a_w_out, b_w_q, b_rel_bias, b_w_o, kv_norm_g, w_kv,
           norm_mix_g, norm_ffn_g, ffn_w_gate, ffn_w_up, ffn_w_down, final_norm_g):
    bsz, seq, d = x.shape
    assert d == D_MODEL and seq % ROW_TILE == 0
    depth = norm_mix_g.shape[0]
    n_a = a_w_in.shape[0]

    def row(v):
        return v.reshape(1, -1).astype(F32)

    h = x
    k_t = v_sh = None
    for l in range(depth):
        if l == n_a:
            k_t, v_sh = _kv_proj(h, row(kv_norm_g), w_kv[:, :D_MODEL].T.astype(BF16),
                                 w_kv[:, D_MODEL:].astype(BF16))
        if l < n_a:
            h = _mlstm_layer(h, row(norm_mix_g[l]), *_mlstm_params(a_w_in[l], a_b_gate[l], a_g_head[l]),
                             a_w_out[l].astype(BF16))
        else:
            j = l - n_a
            h = _attn_layer(h, row(norm_mix_g[l]), b_w_q[j].astype(BF16), b_w_o[j].astype(BF16),
                            _rel_bias(b_rel_bias[j]), k_t, v_sh)
        h = _ffn(h.reshape(bsz * seq, d), row(norm_ffn_g[l]), ffn_w_gate[l].astype(BF16),
                 ffn_w_up[l].astype(BF16), ffn_w_down[l].astype(BF16), row(final_norm_g),
                 final_norm=(l == depth - 1)).reshape(bsz, seq, d)
    return h
```

```python
import functools

import jax
import jax.numpy as jnp
from jax import lax
from jax.experimental import pallas as pl
from jax.experimental.pallas import tpu as pltpu

F32 = jnp.float32
BF16 = jnp.bfloat16

D_MODEL = 1024
CHUNK = 64
A_HEADS = 8
A_DQK = 64
A_DV = 128
A_QK = A_HEADS * A_DQK
A_V = A_HEADS * A_DV
B_HEADS = 16
B_DH = 64
LEFT_CHUNKS = 8
LEFT = LEFT_CHUNKS * CHUNK
BAND = LEFT + CHUNK
MAX_REL = 256
D_FF = 2816
EPS = 1e-6
NEG_INIT = -1e30

LANES = 128
MXU_COLS = 256

ROW_TILE = 512
A_SUPER = 128
FF_TILE = 256
VMEM_LIMIT = 56 * 1024 * 1024

A_GATE_OFF = 2 * A_QK + 2 * A_V


def _rmsnorm(x, g):
    return x * lax.rsqrt(jnp.mean(x * x, axis=-1, keepdims=True) + EPS) * g


def _dot(a, b):
    return jnp.dot(a, b, preferred_element_type=F32)


def _dot_nt(a, b):
    return lax.dot_general(a, b, (((1,), (1,)), ((), ())), preferred_element_type=F32)


def _dot_tn(a, b):
    return lax.dot_general(a, b, (((0,), (0,)), ((), ())), preferred_element_type=F32)


def _const_spec(shape):
    nd = len(shape)
    return pl.BlockSpec(shape, lambda *_: (0,) * nd, pipeline_mode=pl.Buffered(1))


def _ffn_kernel(x_ref, g_ref, wg_ref, wu_ref, wd_ref, gf_ref, o_ref, *, final_norm):
    x = x_ref[...]
    xn = _rmsnorm(x, g_ref[...]).astype(BF16)
    o_ref[...] = x
    for j in range(D_FF // FF_TILE):
        cols = slice(j * FF_TILE, (j + 1) * FF_TILE)
        hg = _dot(xn, wg_ref[:, cols])
        hu = _dot(xn, wu_ref[:, cols])
        act = (hg * jax.nn.sigmoid(hg)) * hu
        o_ref[...] += _dot(act.astype(BF16), wd_ref[cols, :])
    if final_norm:
        o_ref[...] = _rmsnorm(o_ref[...], gf_ref[...])


def _ffn(h2d, g, wg, wu, wd, gf, final_norm):
    n_tok = h2d.shape[0]
    row = pl.BlockSpec((ROW_TILE, D_MODEL), lambda i: (i, 0))
    return pl.pallas_call(
        functools.partial(_ffn_kernel, final_norm=final_norm),
        grid=(n_tok // ROW_TILE,),
        in_specs=[row, _const_spec((1, D_MODEL)), _const_spec((D_MODEL, D_FF)),
                  _const_spec((D_MODEL, D_FF)), _const_spec((D_FF, D_MODEL)),
                  _const_spec((1, D_MODEL))],
        out_specs=row,
        out_shape=jax.ShapeDtypeStruct(h2d.shape, F32),
        compiler_params=pltpu.CompilerParams(
            dimension_semantics=("arbitrary",), vmem_limit_bytes=VMEM_LIMIT),
        name="ffn",
    )(h2d, g, wg, wu, wd, gf)


A_CEXT = A_DV + 16
A_STEPS = ROW_TILE // A_SUPER


def _log_sigmoid(x):
    return jnp.minimum(x, 0.0) - jnp.log1p(jnp.exp(-jnp.abs(x)))


def _mlstm_kernel(x_ref, g_ref, wfm_ref, wk_ref, wg_ref, bg_ref, ghb_ref, wout_ref, o_ref,
                  qt_s, k_s, vt_s, ogt_s, hst_s, svt_s, up_s, st_s, imb_s, bt_s, imbt_s, c_s, m_s):
    t = pl.program_id(1)

    @pl.when(t == 0)
    def _():
        c_s[...] = jnp.zeros_like(c_s)
        m_s[...] = jnp.full_like(m_s, NEG_INIT)

    x = x_ref[0]
    xn = _rmsnorm(x, g_ref[...]).astype(BF16)
    qt_s[...] = _dot_nt(wfm_ref[0:A_QK, :], xn).astype(BF16)
    vt_s[...] = _dot_nt(wfm_ref[A_QK:A_QK + A_V, :], xn).astype(BF16)
    ogt_s[...] = jax.nn.sigmoid(_dot_nt(wfm_ref[A_QK + A_V:A_QK + 2 * A_V, :], xn))
    k_s[...] = (_dot(xn, wk_ref[...]) * (A_DQK ** -0.5)).astype(BF16)
    gates = _dot(xn, wg_ref[...]) + bg_ref[...]
    i_log = gates[:, 0:LANES]
    f_log = _log_sigmoid(gates[:, LANES:2 * LANES])

    rows = lax.broadcasted_iota(jnp.int32, (ROW_TILE, LANES), 0) & (A_SUPER - 1)
    b_cum = f_log
    shift = 1
    while shift < A_SUPER:
        rolled = pltpu.roll(b_cum, shift, axis=0)
        b_cum = b_cum + jnp.where(rows >= shift, rolled, 0.0)
        shift *= 2
    imb = i_log - b_cum
    imb_s[...] = imb
    bt_s[...] = b_cum.T[0:A_HEADS]
    imbt_s[...] = imb.T[0:A_HEADS]

    row_i = lax.broadcasted_iota(jnp.int32, (A_SUPER, A_SUPER), 0)
    col_i = lax.broadcasted_iota(jnp.int32, (A_SUPER, A_SUPER), 1)
    causal = col_i >= row_i
    first_rows = row_i < A_DQK
    first_lanes = col_i < A_DQK
    ext_row = lax.broadcasted_iota(jnp.int32, (A_CEXT - A_DV, LANES), 0) == 0
    lane_row = lax.broadcasted_iota(jnp.int32, (1, LANES), 1) < A_DQK

    def pair_queries(p, r):
        qp = qt_s[p * LANES:(p + 1) * LANES, r]
        zero = jnp.zeros_like(qp)
        return jnp.concatenate([jnp.where(first_rows, qp, zero), jnp.where(first_rows, zero, qp)], axis=1)

    def step(c):
        r = slice(c * A_SUPER, (c + 1) * A_SUPER)
        for p in range(A_HEADS // 2):
            kp = k_s[r, p * LANES:(p + 1) * LANES]
            st_pair = _dot(kp, pair_queries(p, r))
            lhs = []
            for half in range(2):
                h = 2 * p + half
                hr = slice(h * A_DV, (h + 1) * A_DV)
                vt_h = vt_s[hr, r]
                b_row = bt_s[h:h + 1, r]
                imb_row = imbt_s[h:h + 1, r]
                imb_col = imb_s[r, h:h + 1]
                b_last = jnp.broadcast_to(b_row[:, A_SUPER - 1:A_SUPER], (1, LANES))

                log_dt = jnp.where(causal, b_row + imb_col, -jnp.inf)
                mrow = jnp.max(log_dt, axis=0, keepdims=True)
                st = st_pair[:, half * A_SUPER:(half + 1) * A_SUPER] * jnp.exp(log_dt - mrow)
                svt_s[hr, r] = _dot(vt_h, st.astype(BF16))

                a_row = b_last + imb_row
                amax = jnp.broadcast_to(jnp.max(a_row, axis=1, keepdims=True), (1, LANES))
                wa = jnp.exp(a_row - amax)
                vw = (vt_h.astype(F32) * wa).astype(BF16)
                ext = jnp.where(ext_row, jnp.broadcast_to(wa, (A_CEXT - A_DV, LANES)), 0.0).astype(BF16)
                lhs.append(jnp.concatenate([vw, ext], axis=0))

                st_s[c * A_HEADS + h, 0:1, :] = mrow
                st_s[c * A_HEADS + h, 1:2, :] = jnp.sum(st, axis=0, keepdims=True)
                st_s[c * A_HEADS + h, 2:3, :] = amax
                st_s[c * A_HEADS + h, 3:4, :] = b_last
            k_zero = jnp.zeros_like(kp)
            k_split = jnp.concatenate([jnp.where(first_lanes, kp, k_zero),
                                       jnp.where(first_lanes, k_zero, kp)], axis=0)
            up_s[c, p] = _dot(jnp.concatenate(lhs, axis=1), k_split)

        for p in range(A_HEADS // 2):
            qc_pair = _dot(c_s[p].astype(BF16), pair_queries(p, r))
            scale = []
            for half in range(2):
                h = 2 * p + half
                hr = slice(h * A_DV, (h + 1) * A_DV)
                b_row = bt_s[h:h + 1, r]
                mrow = st_s[c * A_HEADS + h, 0:1, :]
                rs = st_s[c * A_HEADS + h, 1:2, :]
                amax = st_s[c * A_HEADS + h, 2:3, :]
                b_last = st_s[c * A_HEADS + h, 3:4, :]
                m_prev = m_s[h]

                inter = b_row + m_prev
                m_j = jnp.maximum(inter, mrow)
                f_intra = jnp.exp(mrow - m_j)
                w_inter = jnp.exp(inter - m_j)
                qc = qc_pair[:, half * A_SUPER:(half + 1) * A_SUPER]
                num = f_intra * svt_s[hr, r] + w_inter * qc[0:A_DV]
                den = f_intra * rs + w_inter * qc[A_DV:A_DV + 1]
                ht = num * (1.0 / jnp.maximum(jnp.abs(den), jnp.exp(-m_j)))
                hn = ht * lax.rsqrt(jnp.mean(ht * ht, axis=0, keepdims=True) + EPS)
                hst_s[hr, r] = (ogt_s[hr, r] * (hn * ghb_ref[hr, :])).astype(BF16)

                m_new = jnp.maximum(b_last + m_prev, amax)
                scale.append((jnp.exp(b_last + m_prev - m_new), jnp.exp(amax - m_new)))
                m_s[h] = m_new
            decay = jnp.where(lane_row, scale[0][0], scale[1][0])
            grow = jnp.where(lane_row, scale[0][1], scale[1][1])
            c_s[p] = decay * c_s[p] + grow * up_s[c, p]

    @pl.when(t >= 0)
    def _():
        for c in range(A_STEPS):
            step(c)

    o_ref[0] = x + _dot_tn(hst_s[...], wout_ref[...])


def _mlstm_layer(h, g, w_fm, w_k, w_g, b_g, g_head_b, w_out):
    bsz, seq, _ = h.shape
    row = pl.BlockSpec((1, ROW_TILE, D_MODEL), lambda b, t: (b, t, 0))
    return pl.pallas_call(
        _mlstm_kernel,
        grid=(bsz, seq // ROW_TILE),
        in_specs=[row, _const_spec((1, D_MODEL)), _const_spec((A_QK + 2 * A_V, D_MODEL)),
                  _const_spec((D_MODEL, A_QK)), _const_spec((D_MODEL, 2 * LANES)),
                  _const_spec((1, 2 * LANES)), _const_spec((A_V, LANES)), _const_spec((A_V, D_MODEL))],
        out_specs=row,
        out_shape=jax.ShapeDtypeStruct(h.shape, F32),
        scratch_shapes=[
            pltpu.VMEM((A_QK, ROW_TILE), BF16),
            pltpu.VMEM((ROW_TILE, A_QK), BF16),
            pltpu.VMEM((A_V, ROW_TILE), BF16),
            pltpu.VMEM((A_V, ROW_TILE), F32),
            pltpu.VMEM((A_V, ROW_TILE), BF16),
            pltpu.VMEM((A_V, ROW_TILE), F32),
            pltpu.VMEM((A_STEPS, A_HEADS // 2, A_CEXT, LANES), F32),
            pltpu.VMEM((A_STEPS * A_HEADS, 8, LANES), F32),
            pltpu.VMEM((ROW_TILE, LANES), F32),
            pltpu.VMEM((A_HEADS, ROW_TILE), F32),
            pltpu.VMEM((A_HEADS, ROW_TILE), F32),
            pltpu.VMEM((A_HEADS // 2, A_CEXT, LANES), F32),
            pltpu.VMEM((A_HEADS, 1, LANES), F32),
        ],
        compiler_params=pltpu.CompilerParams(
            dimension_semantics=("arbitrary", "arbitrary"), vmem_limit_bytes=VMEM_LIMIT),
        name="mlstm",
    )(h, g, w_fm, w_k, w_g, b_g, g_head_b, w_out)


def _kv_kernel(x_ref, g_ref, wkt_ref, wv_ref, kt_ref, v_ref):
    xn = _rmsnorm(x_ref[0], g_ref[...]).astype(BF16)
    kt_ref[0] = _dot_nt(wkt_ref[...], xn).astype(BF16)
    v_ref[0] = _dot(xn, wv_ref[...]).astype(BF16)


def _kv_proj(h, g, w_kt, w_v):
    bsz, seq, _ = h.shape
    row = pl.BlockSpec((1, ROW_TILE, D_MODEL), lambda b, t: (b, t, 0))
    col = pl.BlockSpec((1, D_MODEL, ROW_TILE), lambda b, t: (b, 0, t))
    return pl.pallas_call(
        _kv_kernel,
        grid=(bsz, seq // ROW_TILE),
        in_specs=[row, _const_spec((1, D_MODEL)), _const_spec((D_MODEL, D_MODEL)),
                  _const_spec((D_MODEL, D_MODEL))],
        out_specs=[col, row],
        out_shape=[jax.ShapeDtypeStruct((bsz, D_MODEL, seq), BF16),
                   jax.ShapeDtypeStruct((bsz, seq, D_MODEL), BF16)],
        compiler_params=pltpu.CompilerParams(
            dimension_semantics=("arbitrary", "arbitrary"), vmem_limit_bytes=VMEM_LIMIT),
        name="kv_proj",
    )(h, g, w_kt, w_v)


Q_BLOCK = 2 * CHUNK
K_BLOCK = LEFT + Q_BLOCK


def _attn_kernel(x_ref, g_ref, wq_ref, wo_ref, bias_ref, kt_ref, v_ref, o_ref,
                 q_s, a_s, s_s, p_s, m_s, l_s):
    t = pl.program_id(1)
    x = x_ref[0]
    xn = _rmsnorm(x, g_ref[...]).astype(BF16)
    q_s[...] = (_dot(xn, wq_ref[...]) * (B_DH ** -0.5)).astype(BF16)

    lane = lax.broadcasted_iota(jnp.int32, (Q_BLOCK, LANES), 1)
    lo = lane < B_DH
    n_tiles = K_BLOCK // LANES
    n_blocks = ROW_TILE // Q_BLOCK
    n_pairs = B_HEADS // 2
    left_tiles = LEFT // LANES

    def band(i, first_tile):
        if first_tile:
            first = left_tiles - i
            keys = pl.ds(0, (n_tiles - first) * LANES)
        else:
            first = 0
            keys = pl.ds(pl.multiple_of((t - 1) * ROW_TILE, ROW_TILE) + i * Q_BLOCK, K_BLOCK)
        return first, keys, slice(first * LANES, K_BLOCK)

    def scores(i, p, first_tile):
        pl_ = slice(p * LANES, (p + 1) * LANES)
        _, keys, cols = band(i, first_tile)
        qp = q_s[i * Q_BLOCK:(i + 1) * Q_BLOCK, pl_]
        kp = kt_ref[0, pl_, keys]
        for half in range(2):
            h = 2 * p + half
            own = lo if half == 0 else jnp.logical_not(lo)
            s = _dot(jnp.where(own, qp, jnp.zeros_like(qp)), kp) + bias_ref[h, :, cols]
            s_s[h, :, cols] = s
            m_s[h] = jnp.broadcast_to(jnp.max(s, axis=1, keepdims=True), (Q_BLOCK, LANES))

    def exps(i, p, first_tile):
        first, _, _ = band(i, first_tile)
        for h in (2 * p, 2 * p + 1):
            m = m_s[h]
            acc = jnp.zeros((Q_BLOCK, LANES), F32)
            for j in range(first, n_tiles):
                tl = slice(j * LANES, (j + 1) * LANES)
                e = jnp.exp(s_s[h, :, tl] - m)
                acc = acc + e
                p_s[h, :, tl] = e.astype(BF16)
            l_s[h] = jnp.broadcast_to(jnp.sum(acc, axis=1, keepdims=True), (Q_BLOCK, LANES))

    def values(i, p, first_tile):
        pl_ = slice(p * LANES, (p + 1) * LANES)
        _, keys, cols = band(i, first_tile)
        vp = v_ref[0, keys, pl_]
        o0 = _dot(p_s[2 * p, :, cols], vp) / l_s[2 * p]
        o1 = _dot(p_s[2 * p + 1, :, cols], vp) / l_s[2 * p + 1]
        a_s[i * Q_BLOCK:(i + 1) * Q_BLOCK, pl_] = jnp.where(lo, o0, o1).astype(BF16)

    def block(i, first_tile):
        for stage in (scores, exps, values):
            for p in range(n_pairs):
                stage(i, p, first_tile)

    for i in range(n_blocks):
        pl.when(t == 0)(functools.partial(block, i, True))
        pl.when(t > 0)(functools.partial(block, i, False))

    o_ref[0] = x + _dot(a_s[...], wo_ref[...])


def _attn_layer(h, g, w_q, w_o, bias, k_t, v):
    bsz, seq, _ = h.shape
    assert LEFT == ROW_TILE
    row = pl.BlockSpec((1, ROW_TILE, D_MODEL), lambda b, t: (b, t, 0))
    kt_all = pl.BlockSpec((1, D_MODEL, seq), lambda b, t: (b, 0, 0))
    v_all = pl.BlockSpec((1, seq, D_MODEL), lambda b, t: (b, 0, 0))
    return pl.pallas_call(
        _attn_kernel,
        grid=(bsz, seq // ROW_TILE),
        in_specs=[row, _const_spec((1, D_MODEL)), _const_spec((D_MODEL, D_MODEL)),
                  _const_spec((D_MODEL, D_MODEL)), _const_spec((B_HEADS, Q_BLOCK, K_BLOCK)), kt_all, v_all],
        out_specs=row,
        out_shape=jax.ShapeDtypeStruct(h.shape, F32),
        scratch_shapes=[pltpu.VMEM((ROW_TILE, D_MODEL), BF16),
                        pltpu.VMEM((ROW_TILE, D_MODEL), BF16),
                        pltpu.VMEM((B_HEADS, Q_BLOCK, K_BLOCK), F32),
                        pltpu.VMEM((B_HEADS, Q_BLOCK, K_BLOCK), BF16),
                        pltpu.VMEM((B_HEADS, Q_BLOCK, LANES), F32),
                        pltpu.VMEM((B_HEADS, Q_BLOCK, LANES), F32)],
        compiler_params=pltpu.CompilerParams(
            dimension_semantics=("arbitrary", "arbitrary"), vmem_limit_bytes=VMEM_LIMIT),
        name="attn",
    )(h, g, w_q, w_o, bias, k_t, v)


def _rel_bias(rel_table):
    n_heads = rel_table.shape[0]
    tab = rel_table.astype(F32)
    span = BAND + CHUNK - 1
    n_var = MAX_REL + CHUNK
    base = jnp.concatenate(
        [tab[:, MAX_REL - (CHUNK - 1):2 * MAX_REL + 1],
         jnp.broadcast_to(tab[:, 2 * MAX_REL:], (n_heads, span - n_var))], axis=1)
    rev = base[:, ::-1]
    padded = jnp.pad(rev, ((0, 0), (0, 1)))
    rows = jnp.tile(padded, (1, CHUNK))[:, :CHUNK * span].reshape(n_heads, CHUNK, span)
    band = rows[:, :, CHUNK - 1:CHUNK - 1 + BAND]
    ninf = jnp.full((n_heads, CHUNK, CHUNK), -jnp.inf, F32)
    return jnp.concatenate([jnp.concatenate([band, ninf], axis=2),
                            jnp.concatenate([ninf, band], axis=2)], axis=1)


def _mlstm_params(w_in, b_gate, g_head):
    w_q = w_in[:, 0:A_QK]
    w_k = w_in[:, A_QK:2 * A_QK]
    w_vo = w_in[:, 2 * A_QK:A_GATE_OFF]
    w_fm = jnp.concatenate([w_q, w_vo], axis=1).T.astype(BF16)
    pad = LANES - A_HEADS
    w_g = jnp.concatenate([jnp.pad(w_in[:, A_GATE_OFF:A_GATE_OFF + A_HEADS], ((0, 0), (0, pad))),
                           jnp.pad(w_in[:, A_GATE_OFF + A_HEADS:], ((0, 0), (0, pad)))], axis=1)
    b_g = jnp.concatenate([jnp.pad(b_gate[:A_HEADS], (0, pad)), jnp.pad(b_gate[A_HEADS:], (0, pad))])
    g_head_b = jnp.broadcast_to(g_head.astype(F32)[:, None], (A_V, LANES))
    return w_fm, w_k.astype(BF16), w_g.astype(BF16), b_g.astype(F32).reshape(1, 2 * LANES), g_head_b


def kernel(x, a_w_in, a_b_gate, a_g_head, a_w_out, b_w_q, b_rel_bias, b_w_o, kv_norm_g, w_kv,
           norm_mix_g, norm_ffn_g, ffn_w_gate, ffn_w_up, ffn_w_down, final_norm_g):
    bsz, seq, d = x.shape
    assert d == D_MODEL and seq % ROW_TILE == 0
    depth = norm_mix_g.shape[0]
    n_a = a_w_in.shape[0]

    def row(v):
        return v.reshape(1, -1).astype(F32)

    h = x
    k_t = v_sh = None
    for l in range(depth):
        if l == n_a:
            k_t, v_sh = _kv_proj(h, row(kv_norm_g), w_kv[:, :D_MODEL].T.astype(BF16),
                                 w_kv[:, D_MODEL:].astype(BF16))
        if l < n_a:
            h = _mlstm_layer(h, row(norm_mix_g[l]), *_mlstm_params(a_w_in[l], a_b_gate[l], a_g_head[l]),
                             a_w_out[l].astype(BF16))
        else:
            j = l - n_a
            h = _attn_layer(h, row(norm_mix_g[l]), b_w_q[j].astype(BF16), b_w_o[j].astype(BF16),
                            _rel_bias(b_rel_bias[j]), k_t, v_sh)
        h = _ffn(h.reshape(bsz * seq, d), row(norm_ffn_g[l]), ffn_w_gate[l].astype(BF16),
                 ffn_w_up[l].astype(BF16), ffn_w_down[l].astype(BF16), row(final_norm_g),
                 final_norm=(l == depth - 1)).reshape(bsz, seq, d)
    return h
```

```python
import functools

import jax
import jax.numpy as jnp
from jax import lax
from jax.experimental import pallas as pl
from jax.experimental.pallas import tpu as pltpu

F32 = jnp.float32
BF16 = jnp.bfloat16

D_MODEL = 1024
CHUNK = 64
A_HEADS = 8
A_DQK = 64
A_DV = 128
A_QK = A_HEADS * A_DQK
A_V = A_HEADS * A_DV
B_HEADS = 16
B_DH = 64
LEFT_CHUNKS = 8
LEFT = LEFT_CHUNKS * CHUNK
BAND = LEFT + CHUNK
MAX_REL = 256
D_FF = 2816
EPS = 1e-6
NEG_INIT = -1e30

LANES = 128
MXU_COLS = 256

ROW_TILE = 512
A_SUPER = 128
FF_TILE = 256
VMEM_LIMIT = 56 * 1024 * 1024

A_GATE_OFF = 2 * A_QK + 2 * A_V


def _rmsnorm(x, g):
    return x * lax.rsqrt(jnp.mean(x * x, axis=-1, keepdims=True) + EPS) * g


def _dot(a, b):
    return jnp.dot(a, b, preferred_element_type=F32)


def _dot_nt(a, b):
    return lax.dot_general(a, b, (((1,), (1,)), ((), ())), preferred_element_type=F32)


def _dot_tn(a, b):
    return lax.dot_general(a, b, (((0,), (0,)), ((), ())), preferred_element_type=F32)


def _const_spec(shape):
    nd = len(shape)
    return pl.BlockSpec(shape, lambda *_: (0,) * nd, pipeline_mode=pl.Buffered(1))


def _ffn_kernel(x_ref, g_ref, wg_ref, wu_ref, wd_ref, gf_ref, o_ref, *, final_norm):
    x = x_ref[...]
    xn = _rmsnorm(x, g_ref[...]).astype(BF16)
    o_ref[...] = x
    for j in range(D_FF // FF_TILE):
        cols = slice(j * FF_TILE, (j + 1) * FF_TILE)
        hg = _dot(xn, wg_ref[:, cols])
        hu = _dot(xn, wu_ref[:, cols])
        act = (hg * jax.nn.sigmoid(hg)) * hu
        o_ref[...] += _dot(act.astype(BF16), wd_ref[cols, :])
    if final_norm:
        o_ref[...] = _rmsnorm(o_ref[...], gf_ref[...])


def _ffn(h2d, g, wg, wu, wd, gf, final_norm):
    n_tok = h2d.shape[0]
    row = pl.BlockSpec((ROW_TILE, D_MODEL), lambda i: (i, 0))
    return pl.pallas_call(
        functools.partial(_ffn_kernel, final_norm=final_norm),
        grid=(n_tok // ROW_TILE,),
        in_specs=[row, _const_spec((1, D_MODEL)), _const_spec((D_MODEL, D_FF)),
                  _const_spec((D_MODEL, D_FF)), _const_spec((D_FF, D_MODEL)),
                  _const_spec((1, D_MODEL))],
        out_specs=row,
        out_shape=jax.ShapeDtypeStruct(h2d.shape, F32),
        compiler_params=pltpu.CompilerParams(
            dimension_semantics=("arbitrary",), vmem_limit_bytes=VMEM_LIMIT),
        name="ffn",
    )(h2d, g, wg, wu, wd, gf)


A_CEXT = A_DV + 16
A_STEPS = ROW_TILE // A_SUPER


def _log_sigmoid(x):
    return jnp.minimum(x, 0.0) - jnp.log1p(jnp.exp(-jnp.abs(x)))


def _mlstm_kernel(x_ref, g_ref, wfm_ref, wk_ref, wg_ref, bg_ref, ghb_ref, wout_ref, o_ref,
                  qt_s, k_s, vt_s, ogt_s, hst_s, svt_s, up_s, st_s, imb_s, bt_s, imbt_s, c_s, m_s):
    t = pl.program_id(1)

    @pl.when(t == 0)
    def _():
        c_s[...] = jnp.zeros_like(c_s)
        m_s[...] = jnp.full_like(m_s, NEG_INIT)

    x = x_ref[0]
    xn = _rmsnorm(x, g_ref[...]).astype(BF16)
    qt_s[...] = _dot_nt(wfm_ref[0:A_QK, :], xn).astype(BF16)
    vt_s[...] = _dot_nt(wfm_ref[A_QK:A_QK + A_V, :], xn).astype(BF16)
    ogt_s[...] = jax.nn.sigmoid(_dot_nt(wfm_ref[A_QK + A_V:A_QK + 2 * A_V, :], xn))
    k_s[...] = (_dot(xn, wk_ref[...]) * (A_DQK ** -0.5)).astype(BF16)
    gates = _dot(xn, wg_ref[...]) + bg_ref[...]
    i_log = gates[:, 0:LANES]
    f_log = _log_sigmoid(gates[:, LANES:2 * LANES])

    rows = lax.broadcasted_iota(jnp.int32, (ROW_TILE, LANES), 0) & (A_SUPER - 1)
    b_cum = f_log
    shift = 1
    while shift < A_SUPER:
        rolled = pltpu.roll(b_cum, shift, axis=0)
        b_cum = b_cum + jnp.where(rows >= shift, rolled, 0.0)
        shift *= 2
    imb = i_log - b_cum
    imb_s[...] = imb
    bt_s[...] = b_cum.T[0:A_HEADS]
    imbt_s[...] = imb.T[0:A_HEADS]

    row_i = lax.broadcasted_iota(jnp.int32, (A_SUPER, A_SUPER), 0)
    col_i = lax.broadcasted_iota(jnp.int32, (A_SUPER, A_SUPER), 1)
    causal = col_i >= row_i
    first_rows = row_i < A_DQK
    first_lanes = col_i < A_DQK
    ext_row = lax.broadcasted_iota(jnp.int32, (A_CEXT - A_DV, LANES), 0) == 0
    lane_row = lax.broadcasted_iota(jnp.int32, (1, LANES), 1) < A_DQK

    def pair_queries(p, r):
        qp = qt_s[p * LANES:(p + 1) * LANES, r]
        zero = jnp.zeros_like(qp)
        return jnp.concatenate([jnp.where(first_rows, qp, zero), jnp.where(first_rows, zero, qp)], axis=1)

    def step(c):
        r = slice(c * A_SUPER, (c + 1) * A_SUPER)
        def independent(p):
            kp = k_s[r, p * LANES:(p + 1) * LANES]
            st_pair = _dot(kp, pair_queries(p, r))
            lhs = []
            for half in range(2):
                h = 2 * p + half
                hr = slice(h * A_DV, (h + 1) * A_DV)
                vt_h = vt_s[hr, r]
                b_row = bt_s[h:h + 1, r]
                imb_row = imbt_s[h:h + 1, r]
                imb_col = imb_s[r, h:h + 1]
                b_last = jnp.broadcast_to(b_row[:, A_SUPER - 1:A_SUPER], (1, LANES))

                log_dt = jnp.where(causal, b_row + imb_col, -jnp.inf)
                mrow = jnp.max(log_dt, axis=0, keepdims=True)
                st = st_pair[:, half * A_SUPER:(half + 1) * A_SUPER] * jnp.exp(log_dt - mrow)
                svt_s[hr, r] = _dot(vt_h, st.astype(BF16))

                a_row = b_last + imb_row
                amax = jnp.broadcast_to(jnp.max(a_row, axis=1, keepdims=True), (1, LANES))
                wa = jnp.exp(a_row - amax)
                vw = (vt_h.astype(F32) * wa).astype(BF16)
                ext = jnp.where(ext_row, jnp.broadcast_to(wa, (A_CEXT - A_DV, LANES)), 0.0).astype(BF16)
                lhs.append(jnp.concatenate([vw, ext], axis=0))

                st_s[c * A_HEADS + h, 0:1, :] = mrow
                st_s[c * A_HEADS + h, 1:2, :] = jnp.sum(st, axis=0, keepdims=True)
                st_s[c * A_HEADS + h, 2:3, :] = amax
                st_s[c * A_HEADS + h, 3:4, :] = b_last
            k_zero = jnp.zeros_like(kp)
            k_split = jnp.concatenate([jnp.where(first_lanes, kp, k_zero),
                                       jnp.where(first_lanes, k_zero, kp)], axis=0)
            up_s[c, p] = _dot(jnp.concatenate(lhs, axis=1), k_split)

        def dependent(p):
            qc_pair = _dot(c_s[p].astype(BF16), pair_queries(p, r))
            scale = []
            for half in range(2):
                h = 2 * p + half
                hr = slice(h * A_DV, (h + 1) * A_DV)
                b_row = bt_s[h:h + 1, r]
                mrow = st_s[c * A_HEADS + h, 0:1, :]
                rs = st_s[c * A_HEADS + h, 1:2, :]
                amax = st_s[c * A_HEADS + h, 2:3, :]
                b_last = st_s[c * A_HEADS + h, 3:4, :]
                m_prev = m_s[h]

                inter = b_row + m_prev
                m_j = jnp.maximum(inter, mrow)
                f_intra = jnp.exp(mrow - m_j)
                w_inter = jnp.exp(inter - m_j)
                qc = qc_pair[:, half * A_SUPER:(half + 1) * A_SUPER]
                num = f_intra * svt_s[hr, r] + w_inter * qc[0:A_DV]
                den = f_intra * rs + w_inter * qc[A_DV:A_DV + 1]
                ht = num * (1.0 / jnp.maximum(jnp.abs(den), jnp.exp(-m_j)))
                hn = ht * lax.rsqrt(jnp.mean(ht * ht, axis=0, keepdims=True) + EPS)
                hst_s[hr, r] = (ogt_s[hr, r] * (hn * ghb_ref[hr, :])).astype(BF16)

                m_new = jnp.maximum(b_last + m_prev, amax)
                scale.append((jnp.exp(b_last + m_prev - m_new), jnp.exp(amax - m_new)))
                m_s[h] = m_new
            decay = jnp.where(lane_row, scale[0][0], scale[1][0])
            grow = jnp.where(lane_row, scale[0][1], scale[1][1])
            c_s[p] = decay * c_s[p] + grow * up_s[c, p]

        for p in range(A_HEADS // 2):
            independent(p)
            dependent(p)

    @pl.when(t >= 0)
    def _():
        for c in range(A_STEPS):
            step(c)

    o_ref[0] = x + _dot_tn(hst_s[...], wout_ref[...])


def _mlstm_layer(h, g, w_fm, w_k, w_g, b_g, g_head_b, w_out):
    bsz, seq, _ = h.shape
    row = pl.BlockSpec((1, ROW_TILE, D_MODEL), lambda b, t: (b, t, 0))
    return pl.pallas_call(
        _mlstm_kernel,
        grid=(bsz, seq // ROW_TILE),
        in_specs=[row, _const_spec((1, D_MODEL)), _const_spec((A_QK + 2 * A_V, D_MODEL)),
                  _const_spec((D_MODEL, A_QK)), _const_spec((D_MODEL, 2 * LANES)),
                  _const_spec((1, 2 * LANES)), _const_spec((A_V, LANES)), _const_spec((A_V, D_MODEL))],
        out_specs=row,
        out_shape=jax.ShapeDtypeStruct(h.shape, F32),
        scratch_shapes=[
            pltpu.VMEM((A_QK, ROW_TILE), BF16),
            pltpu.VMEM((ROW_TILE, A_QK), BF16),
            pltpu.VMEM((A_V, ROW_TILE), BF16),
            pltpu.VMEM((A_V, ROW_TILE), F32),
            pltpu.VMEM((A_V, ROW_TILE), BF16),
            pltpu.VMEM((A_V, ROW_TILE), F32),
            pltpu.VMEM((A_STEPS, A_HEADS // 2, A_CEXT, LANES), F32),
            pltpu.VMEM((A_STEPS * A_HEADS, 8, LANES), F32),
            pltpu.VMEM((ROW_TILE, LANES), F32),
            pltpu.VMEM((A_HEADS, ROW_TILE), F32),
            pltpu.VMEM((A_HEADS, ROW_TILE), F32),
            pltpu.VMEM((A_HEADS // 2, A_CEXT, LANES), F32),
            pltpu.VMEM((A_HEADS, 1, LANES), F32),
        ],
        compiler_params=pltpu.CompilerParams(
            dimension_semantics=("arbitrary", "arbitrary"), vmem_limit_bytes=VMEM_LIMIT),
        name="mlstm",
    )(h, g, w_fm, w_k, w_g, b_g, g_head_b, w_out)


def _kv_kernel(x_ref, g_ref, wkt_ref, wv_ref, kt_ref, v_ref):
    xn = _rmsnorm(x_ref[0], g_ref[...]).astype(BF16)
    kt_ref[0] = _dot_nt(wkt_ref[...], xn).astype(BF16)
    v_ref[0] = _dot(xn, wv_ref[...]).astype(BF16)


def _kv_proj(h, g, w_kt, w_v):
    bsz, seq, _ = h.shape
    row = pl.BlockSpec((1, ROW_TILE, D_MODEL), lambda b, t: (b, t, 0))
    col = pl.BlockSpec((1, D_MODEL, ROW_TILE), lambda b, t: (b, 0, t))
    return pl.pallas_call(
        _kv_kernel,
        grid=(bsz, seq // ROW_TILE),
        in_specs=[row, _const_spec((1, D_MODEL)), _const_spec((D_MODEL, D_MODEL)),
                  _const_spec((D_MODEL, D_MODEL))],
        out_specs=[col, row],
        out_shape=[jax.ShapeDtypeStruct((bsz, D_MODEL, seq), BF16),
                   jax.ShapeDtypeStruct((bsz, seq, D_MODEL), BF16)],
        compiler_params=pltpu.CompilerParams(
            dimension_semantics=("arbitrary", "arbitrary"), vmem_limit_bytes=VMEM_LIMIT),
        name="kv_proj",
    )(h, g, w_kt, w_v)


Q_BLOCK = 2 * CHUNK
K_BLOCK = LEFT + Q_BLOCK


def _attn_kernel(x_ref, g_ref, wq_ref, wo_ref, bias_ref, kt_ref, v_ref, o_ref,
                 q_s, a_s, s_s, p_s, m_s, l_s):
    t = pl.program_id(1)
    x = x_ref[0]
    xn = _rmsnorm(x, g_ref[...]).astype(BF16)
    q_s[...] = (_dot(xn, wq_ref[...]) * (B_DH ** -0.5)).astype(BF16)

    lane = lax.broadcasted_iota(jnp.int32, (Q_BLOCK, LANES), 1)
    lo = lane < B_DH
    n_tiles = K_BLOCK // LANES
    n_blocks = ROW_TILE // Q_BLOCK
    n_pairs = B_HEADS // 2
    left_tiles = LEFT // LANES

    def band(i, first_tile):
        if first_tile:
            first = left_tiles - i
            keys = pl.ds(0, (n_tiles - first) * LANES)
        else:
            first = 0
            keys = pl.ds(pl.multiple_of((t - 1) * ROW_TILE, ROW_TILE) + i * Q_BLOCK, K_BLOCK)
        return first, keys, slice(first * LANES, K_BLOCK)

    def scores(i, p, first_tile):
        pl_ = slice(p * LANES, (p + 1) * LANES)
        _, keys, cols = band(i, first_tile)
        qp = q_s[i * Q_BLOCK:(i + 1) * Q_BLOCK, pl_]
        kp = kt_ref[0, pl_, keys]
        for half in range(2):
            h = 2 * p + half
            own = lo if half == 0 else jnp.logical_not(lo)
            s = _dot(jnp.where(own, qp, jnp.zeros_like(qp)), kp) + bias_ref[h, :, cols]
            s_s[h, :, cols] = s
            m_s[h] = jnp.broadcast_to(jnp.max(s, axis=1, keepdims=True), (Q_BLOCK, LANES))

    def exps(i, p, first_tile):
        first, _, _ = band(i, first_tile)
        for h in (2 * p, 2 * p + 1):
            m = m_s[h]
            acc = jnp.zeros((Q_BLOCK, LANES), F32)
            for j in range(first, n_tiles):
                tl = slice(j * LANES, (j + 1) * LANES)
                e = jnp.exp(s_s[h, :, tl] - m)
                acc = acc + e
                p_s[h, :, tl] = e.astype(BF16)
            l_s[h] = jnp.broadcast_to(jnp.sum(acc, axis=1, keepdims=True), (Q_BLOCK, LANES))

    def values(i, p, first_tile):
        pl_ = slice(p * LANES, (p + 1) * LANES)
        _, keys, cols = band(i, first_tile)
        vp = v_ref[0, keys, pl_]
        o0 = _dot(p_s[2 * p, :, cols], vp) / l_s[2 * p]
        o1 = _dot(p_s[2 * p + 1, :, cols], vp) / l_s[2 * p + 1]
        a_s[i * Q_BLOCK:(i + 1) * Q_BLOCK, pl_] = jnp.where(lo, o0, o1).astype(BF16)

    def block(i, first_tile):
        for stage in (scores, exps, values):
            for p in range(n_pairs):
                stage(i, p, first_tile)

    for i in range(n_blocks):
        pl.when(t == 0)(functools.partial(block, i, True))
        pl.when(t > 0)(functools.partial(block, i, False))

    o_ref[0] = x + _dot(a_s[...], wo_ref[...])


def _attn_layer(h, g, w_q, w_o, bias, k_t, v):
    bsz, seq, _ = h.shape
    assert LEFT == ROW_TILE
    row = pl.BlockSpec((1, ROW_TILE, D_MODEL), lambda b, t: (b, t, 0))
    kt_all = pl.BlockSpec((1, D_MODEL, seq), lambda b, t: (b, 0, 0))
    v_all = pl.BlockSpec((1, seq, D_MODEL), lambda b, t: (b, 0, 0))
    return pl.pallas_call(
        _attn_kernel,
        grid=(bsz, seq // ROW_TILE),
        in_specs=[row, _const_spec((1, D_MODEL)), _const_spec((D_MODEL, D_MODEL)),
                  _const_spec((D_MODEL, D_MODEL)), _const_spec((B_HEADS, Q_BLOCK, K_BLOCK)), kt_all, v_all],
        out_specs=row,
        out_shape=jax.ShapeDtypeStruct(h.shape, F32),
        scratch_shapes=[pltpu.VMEM((ROW_TILE, D_MODEL), BF16),
                        pltpu.VMEM((ROW_TILE, D_MODEL), BF16),
                        pltpu.VMEM((B_HEADS, Q_BLOCK, K_BLOCK), F32),
                        pltpu.VMEM((B_HEADS, Q_BLOCK, K_BLOCK), BF16),
                        pltpu.VMEM((B_HEADS, Q_BLOCK, LANES), F32),
                        pltpu.VMEM((B_HEADS, Q_BLOCK, LANES), F32)],
        compiler_params=pltpu.CompilerParams(
            dimension_semantics=("arbitrary", "arbitrary"), vmem_limit_bytes=VMEM_LIMIT),
        name="attn",
    )(h, g, w_q, w_o, bias, k_t, v)


def _rel_bias(rel_table):
    n_heads = rel_table.shape[0]
    tab = rel_table.astype(F32)
    span = BAND + CHUNK - 1
    n_var = MAX_REL + CHUNK
    base = jnp.concatenate(
        [tab[:, MAX_REL - (CHUNK - 1):2 * MAX_REL + 1],
         jnp.broadcast_to(tab[:, 2 * MAX_REL:], (n_heads, span - n_var))], axis=1)
    rev = base[:, ::-1]
    padded = jnp.pad(rev, ((0, 0), (0, 1)))
    rows = jnp.tile(padded, (1, CHUNK))[:, :CHUNK * span].reshape(n_heads, CHUNK, span)
    band = rows[:, :, CHUNK - 1:CHUNK - 1 + BAND]
    ninf = jnp.full((n_heads, CHUNK, CHUNK), -jnp.inf, F32)
    return jnp.concatenate([jnp.concatenate([band, ninf], axis=2),
                            jnp.concatenate([ninf, band], axis=2)], axis=1)


def _mlstm_params(w_in, b_gate, g_head):
    w_q = w_in[:, 0:A_QK]
    w_k = w_in[:, A_QK:2 * A_QK]
    w_vo = w_in[:, 2 * A_QK:A_GATE_OFF]
    w_fm = jnp.concatenate([w_q, w_vo], axis=1).T.astype(BF16)
    pad = LANES - A_HEADS
    w_g = jnp.concatenate([jnp.pad(w_in[:, A_GATE_OFF:A_GATE_OFF + A_HEADS], ((0, 0), (0, pad))),
                           jnp.pad(w_in[:, A_GATE_OFF + A_HEADS:], ((0, 0), (0, pad)))], axis=1)
    b_g = jnp.concatenate([jnp.pad(b_gate[:A_HEADS], (0, pad)), jnp.pad(b_gate[A_HEADS:], (0, pad))])
    g_head_b = jnp.broadcast_to(g_head.astype(F32)[:, None], (A_V, LANES))
    return w_fm, w_k.astype(BF16), w_g.astype(BF16), b_g.astype(F32).reshape(1, 2 * LANES), g_head_b


def kernel(x, a_w_in, a_b_gate, a_g_head, a_w_out, b_w_q, b_rel_bias, b_w_o, kv_norm_g, w_kv,
           norm_mix_g, norm_ffn_g, ffn_w_gate, ffn_w_up, ffn_w_down, final_norm_g):
    bsz, seq, d = x.shape
    assert d == D_MODEL and seq % ROW_TILE == 0
    depth = norm_mix_g.shape[0]
    n_a = a_w_in.shape[0]

    def row(v):
        return v.reshape(1, -1).astype(F32)

    h = x
    k_t = v_sh = None
    for l in range(depth):
        if l == n_a:
            k_t, v_sh = _kv_proj(h, row(kv_norm_g), w_kv[:, :D_MODEL].T.astype(BF16),
                                 w_kv[:, D_MODEL:].astype(BF16))
        if l < n_a:
            h = _mlstm_layer(h, row(norm_mix_g[l]), *_mlstm_params(a_w_in[l], a_b_gate[l], a_g_head[l]),
                             a_w_out[l].astype(BF16))
        else:
            j = l - n_a
            h = _attn_layer(h, row(norm_mix_g[l]), b_w_q[j].astype(BF16), b_w_o[j].astype(BF16),
                            _rel_bias(b_rel_bias[j]), k_t, v_sh)
        h = _ffn(h.reshape(bsz * seq, d), row(norm_ffn_g[l]), ffn_w_gate[l].astype(BF16),
                 ffn_w_up[l].astype(BF16), ffn_w_down[l].astype(BF16), row(final_norm_g),
                 final_norm=(l == depth - 1)).reshape(bsz, seq, d)
    return h
```

```python
import functools

import jax
import jax.numpy as jnp
from jax import lax
from jax.experimental import pallas as pl
from jax.experimental.pallas import tpu as pltpu

F32 = jnp.float32
BF16 = jnp.bfloat16

D_MODEL = 1024
CHUNK = 64
A_HEADS = 8
A_DQK = 64
A_DV = 128
A_QK = A_HEADS * A_DQK
A_V = A_HEADS * A_DV
B_HEADS = 16
B_DH = 64
LEFT_CHUNKS = 8
LEFT = LEFT_CHUNKS * CHUNK
BAND = LEFT + CHUNK
MAX_REL = 256
D_FF = 2816
EPS = 1e-6
NEG_INIT = -1e30

LANES = 128
MXU_COLS = 256

ROW_TILE = 512
A_SUPER = 128
FF_TILE = 256
VMEM_LIMIT = 56 * 1024 * 1024

A_GATE_OFF = 2 * A_QK + 2 * A_V


def _rmsnorm(x, g):
    return x * lax.rsqrt(jnp.mean(x * x, axis=-1, keepdims=True) + EPS) * g


def _dot(a, b):
    return jnp.dot(a, b, preferred_element_type=F32)


def _dot_nt(a, b):
    return lax.dot_general(a, b, (((1,), (1,)), ((), ())), preferred_element_type=F32)


def _dot_tn(a, b):
    return lax.dot_general(a, b, (((0,), (0,)), ((), ())), preferred_element_type=F32)


def _const_spec(shape):
    nd = len(shape)
    return pl.BlockSpec(shape, lambda *_: (0,) * nd, pipeline_mode=pl.Buffered(1))


def _ffn_kernel(x_ref, g_ref, wg_ref, wu_ref, wd_ref, gf_ref, o_ref, *, final_norm):
    x = x_ref[...]
    xn = _rmsnorm(x, g_ref[...]).astype(BF16)
    o_ref[...] = x
    for j in range(D_FF // FF_TILE):
        cols = slice(j * FF_TILE, (j + 1) * FF_TILE)
        hg = _dot(xn, wg_ref[:, cols])
        hu = _dot(xn, wu_ref[:, cols])
        act = (hg * jax.nn.sigmoid(hg)) * hu
        o_ref[...] += _dot(act.astype(BF16), wd_ref[cols, :])
    if final_norm:
        o_ref[...] = _rmsnorm(o_ref[...], gf_ref[...])


def _ffn(h2d, g, wg, wu, wd, gf, final_norm):
    n_tok = h2d.shape[0]
    row = pl.BlockSpec((ROW_TILE, D_MODEL), lambda i: (i, 0))
    return pl.pallas_call(
        functools.partial(_ffn_kernel, final_norm=final_norm),
        grid=(n_tok // ROW_TILE,),
        in_specs=[row, _const_spec((1, D_MODEL)), _const_spec((D_MODEL, D_FF)),
                  _const_spec((D_MODEL, D_FF)), _const_spec((D_FF, D_MODEL)),
                  _const_spec((1, D_MODEL))],
        out_specs=row,
        out_shape=jax.ShapeDtypeStruct(h2d.shape, F32),
        compiler_params=pltpu.CompilerParams(
            dimension_semantics=("arbitrary",), vmem_limit_bytes=VMEM_LIMIT),
        name="ffn",
    )(h2d, g, wg, wu, wd, gf)


A_CEXT = A_DV + 16
A_STEPS = ROW_TILE // A_SUPER


def _log_sigmoid(x):
    return jnp.minimum(x, 0.0) - jnp.log1p(jnp.exp(-jnp.abs(x)))


def _mlstm_kernel(x_ref, g_ref, wfm_ref, wk_ref, wg_ref, bg_ref, ghb_ref, wout_ref, o_ref,
                  qt_s, k_s, vt_s, ogt_s, hst_s, imb_s, bt_s, imbt_s, c_s, m_s):
    t = pl.program_id(1)

    @pl.when(t == 0)
    def _():
        c_s[...] = jnp.zeros_like(c_s)
        m_s[...] = jnp.full_like(m_s, NEG_INIT)

    x = x_ref[0]
    xn = _rmsnorm(x, g_ref[...]).astype(BF16)
    qt_s[...] = _dot_nt(wfm_ref[0:A_QK, :], xn).astype(BF16)
    vt_s[...] = _dot_nt(wfm_ref[A_QK:A_QK + A_V, :], xn).astype(BF16)
    ogt_s[...] = jax.nn.sigmoid(_dot_nt(wfm_ref[A_QK + A_V:A_QK + 2 * A_V, :], xn))
    k_s[...] = (_dot(xn, wk_ref[...]) * (A_DQK ** -0.5)).astype(BF16)
    gates = _dot(xn, wg_ref[...]) + bg_ref[...]
    i_log = gates[:, 0:LANES]
    f_log = _log_sigmoid(gates[:, LANES:2 * LANES])

    rows = lax.broadcasted_iota(jnp.int32, (ROW_TILE, LANES), 0) & (A_SUPER - 1)
    b_cum = f_log
    shift = 1
    while shift < A_SUPER:
        rolled = pltpu.roll(b_cum, shift, axis=0)
        b_cum = b_cum + jnp.where(rows >= shift, rolled, 0.0)
        shift *= 2
    imb = i_log - b_cum
    imb_s[...] = imb
    bt_s[...] = b_cum.T[0:A_HEADS]
    imbt_s[...] = imb.T[0:A_HEADS]

    row_i = lax.broadcasted_iota(jnp.int32, (A_SUPER, A_SUPER), 0)
    col_i = lax.broadcasted_iota(jnp.int32, (A_SUPER, A_SUPER), 1)
    causal = col_i >= row_i
    first_rows = row_i < A_DQK
    first_lanes = col_i < A_DQK
    ext_row = lax.broadcasted_iota(jnp.int32, (A_CEXT - A_DV, LANES), 0) == 0
    lane_row = lax.broadcasted_iota(jnp.int32, (1, LANES), 1) < A_DQK

    def pair_queries(p, r):
        qp = qt_s[p * LANES:(p + 1) * LANES, r]
        zero = jnp.zeros_like(qp)
        return jnp.concatenate([jnp.where(first_rows, qp, zero), jnp.where(first_rows, zero, qp)], axis=1)

    def pair_step(c, p):
        r = slice(c * A_SUPER, (c + 1) * A_SUPER)
        q_pair = pair_queries(p, r)
        kp = k_s[r, p * LANES:(p + 1) * LANES]
        st_pair = _dot(kp, q_pair)
        qc_pair = _dot(c_s[p].astype(BF16), q_pair)
        lhs, scale = [], []
        for half in range(2):
            h = 2 * p + half
            hr = slice(h * A_DV, (h + 1) * A_DV)
            vt_h = vt_s[hr, r]
            b_row = bt_s[h:h + 1, r]
            imb_row = imbt_s[h:h + 1, r]
            imb_col = imb_s[r, h:h + 1]
            b_last = jnp.broadcast_to(b_row[:, A_SUPER - 1:A_SUPER], (1, LANES))
            m_prev = m_s[h]

            log_dt = jnp.where(causal, b_row + imb_col, -jnp.inf)
            mrow = jnp.max(log_dt, axis=0, keepdims=True)
            st = st_pair[:, half * A_SUPER:(half + 1) * A_SUPER] * jnp.exp(log_dt - mrow)
            svt = _dot(vt_h, st.astype(BF16))
            rs = jnp.sum(st, axis=0, keepdims=True)

            inter = b_row + m_prev
            m_j = jnp.maximum(inter, mrow)
            f_intra = jnp.exp(mrow - m_j)
            w_inter = jnp.exp(inter - m_j)
            qc = qc_pair[:, half * A_SUPER:(half + 1) * A_SUPER]
            num = f_intra * svt + w_inter * qc[0:A_DV]
            den = f_intra * rs + w_inter * qc[A_DV:A_DV + 1]
            ht = num * (1.0 / jnp.maximum(jnp.abs(den), jnp.exp(-m_j)))
            hn = ht * lax.rsqrt(jnp.mean(ht * ht, axis=0, keepdims=True) + EPS)
            hst_s[hr, r] = (ogt_s[hr, r] * (hn * ghb_ref[hr, :])).astype(BF16)

            a_row = b_last + imb_row
            amax = jnp.broadcast_to(jnp.max(a_row, axis=1, keepdims=True), (1, LANES))
            wa = jnp.exp(a_row - amax)
            vw = (vt_h.astype(F32) * wa).astype(BF16)
            ext = jnp.where(ext_row, jnp.broadcast_to(wa, (A_CEXT - A_DV, LANES)), 0.0).astype(BF16)
            lhs.append(jnp.concatenate([vw, ext], axis=0))

            m_new = jnp.maximum(b_last + m_prev, amax)
            scale.append((jnp.exp(b_last + m_prev - m_new), jnp.exp(amax - m_new)))
            m_s[h] = m_new
        k_zero = jnp.zeros_like(kp)
        k_split = jnp.concatenate([jnp.where(first_lanes, kp, k_zero),
                                   jnp.where(first_lanes, k_zero, kp)], axis=0)
        up = _dot(jnp.concatenate(lhs, axis=1), k_split)
        decay = jnp.where(lane_row, scale[0][0], scale[1][0])
        grow = jnp.where(lane_row, scale[0][1], scale[1][1])
        c_s[p] = decay * c_s[p] + grow * up

    @pl.when(t >= 0)
    def _():
        for c in range(A_STEPS):
            for p in range(A_HEADS // 2):
                pair_step(c, p)

    o_ref[0] = x + _dot_tn(hst_s[...], wout_ref[...])


def _mlstm_layer(h, g, w_fm, w_k, w_g, b_g, g_head_b, w_out):
    bsz, seq, _ = h.shape
    row = pl.BlockSpec((1, ROW_TILE, D_MODEL), lambda b, t: (b, t, 0))
    return pl.pallas_call(
        _mlstm_kernel,
        grid=(bsz, seq // ROW_TILE),
        in_specs=[row, _const_spec((1, D_MODEL)), _const_spec((A_QK + 2 * A_V, D_MODEL)),
                  _const_spec((D_MODEL, A_QK)), _const_spec((D_MODEL, 2 * LANES)),
                  _const_spec((1, 2 * LANES)), _const_spec((A_V, LANES)), _const_spec((A_V, D_MODEL))],
        out_specs=row,
        out_shape=jax.ShapeDtypeStruct(h.shape, F32),
        scratch_shapes=[
            pltpu.VMEM((A_QK, ROW_TILE), BF16),
            pltpu.VMEM((ROW_TILE, A_QK), BF16),
            pltpu.VMEM((A_V, ROW_TILE), BF16),
            pltpu.VMEM((A_V, ROW_TILE), F32),
            pltpu.VMEM((A_V, ROW_TILE), BF16),
            pltpu.VMEM((ROW_TILE, LANES), F32),
            pltpu.VMEM((A_HEADS, ROW_TILE), F32),
            pltpu.VMEM((A_HEADS, ROW_TILE), F32),
            pltpu.VMEM((A_HEADS // 2, A_CEXT, LANES), F32),
            pltpu.VMEM((A_HEADS, 1, LANES), F32),
        ],
        compiler_params=pltpu.CompilerParams(
            dimension_semantics=("arbitrary", "arbitrary"), vmem_limit_bytes=VMEM_LIMIT),
        name="mlstm",
    )(h, g, w_fm, w_k, w_g, b_g, g_head_b, w_out)


def _kv_kernel(x_ref, g_ref, wkt_ref, wv_ref, kt_ref, v_ref):
    xn = _rmsnorm(x_ref[0], g_ref[...]).astype(BF16)
    kt_ref[0] = _dot_nt(wkt_ref[...], xn).astype(BF16)
    v_ref[0] = _dot(xn, wv_ref[...]).astype(BF16)


def _kv_proj(h, g, w_kt, w_v):
    bsz, seq, _ = h.shape
    row = pl.BlockSpec((1, ROW_TILE, D_MODEL), lambda b, t: (b, t, 0))
    col = pl.BlockSpec((1, D_MODEL, ROW_TILE), lambda b, t: (b, 0, t))
    return pl.pallas_call(
        _kv_kernel,
        grid=(bsz, seq // ROW_TILE),
        in_specs=[row, _const_spec((1, D_MODEL)), _const_spec((D_MODEL, D_MODEL)),
                  _const_spec((D_MODEL, D_MODEL))],
        out_specs=[col, row],
        out_shape=[jax.ShapeDtypeStruct((bsz, D_MODEL, seq), BF16),
                   jax.ShapeDtypeStruct((bsz, seq, D_MODEL), BF16)],
        compiler_params=pltpu.CompilerParams(
            dimension_semantics=("arbitrary", "arbitrary"), vmem_limit_bytes=VMEM_LIMIT),
        name="kv_proj",
    )(h, g, w_kt, w_v)


Q_BLOCK = 2 * CHUNK
K_BLOCK = LEFT + Q_BLOCK


def _attn_kernel(x_ref, g_ref, wq_ref, wo_ref, bias_ref, kt_ref, v_ref, o_ref,
                 q_s, a_s, s_s, p_s, m_s, l_s):
    t = pl.program_id(1)
    x = x_ref[0]
    xn = _rmsnorm(x, g_ref[...]).astype(BF16)
    q_s[...] = (_dot(xn, wq_ref[...]) * (B_DH ** -0.5)).astype(BF16)

    lane = lax.broadcasted_iota(jnp.int32, (Q_BLOCK, LANES), 1)
    lo = lane < B_DH
    n_tiles = K_BLOCK // LANES
    n_blocks = ROW_TILE // Q_BLOCK
    n_pairs = B_HEADS // 2
    left_tiles = LEFT // LANES

    def band(i, first_tile):
        if first_tile:
            first = left_tiles - i
            keys = pl.ds(0, (n_tiles - first) * LANES)
        else:
            first = 0
            keys = pl.ds(pl.multiple_of((t - 1) * ROW_TILE, ROW_TILE) + i * Q_BLOCK, K_BLOCK)
        return first, keys, slice(first * LANES, K_BLOCK)

    def scores(i, p, first_tile):
        pl_ = slice(p * LANES, (p + 1) * LANES)
        _, keys, cols = band(i, first_tile)
        qp = q_s[i * Q_BLOCK:(i + 1) * Q_BLOCK, pl_]
        kp = kt_ref[0, pl_, keys]
        for half in range(2):
            h = 2 * p + half
            own = lo if half == 0 else jnp.logical_not(lo)
            s = _dot(jnp.where(own, qp, jnp.zeros_like(qp)), kp) + bias_ref[h, :, cols]
            s_s[h, :, cols] = s
            m_s[h] = jnp.broadcast_to(jnp.max(s, axis=1, keepdims=True), (Q_BLOCK, LANES))

    def exps(i, p, first_tile):
        first, _, _ = band(i, first_tile)
        for h in (2 * p, 2 * p + 1):
            m = m_s[h]
            acc = jnp.zeros((Q_BLOCK, LANES), F32)
            for j in range(first, n_tiles):
                tl = slice(j * LANES, (j + 1) * LANES)
                e = jnp.exp(s_s[h, :, tl] - m)
                acc = acc + e
                p_s[h, :, tl] = e.astype(BF16)
            l_s[h] = jnp.broadcast_to(jnp.sum(acc, axis=1, keepdims=True), (Q_BLOCK, LANES))

    def values(i, p, first_tile):
        pl_ = slice(p * LANES, (p + 1) * LANES)
        _, keys, cols = band(i, first_tile)
        vp = v_ref[0, keys, pl_]
        o0 = _dot(p_s[2 * p, :, cols], vp) / l_s[2 * p]
        o1 = _dot(p_s[2 * p + 1, :, cols], vp) / l_s[2 * p + 1]
        a_s[i * Q_BLOCK:(i + 1) * Q_BLOCK, pl_] = jnp.where(lo, o0, o1).astype(BF16)

    def block(i, first_tile):
        for stage in (scores, exps, values):
            for p in range(n_pairs):
                stage(i, p, first_tile)

    for i in range(n_blocks):
        pl.when(t == 0)(functools.partial(block, i, True))
        pl.when(t > 0)(functools.partial(block, i, False))

    o_ref[0] = x + _dot(a_s[...], wo_ref[...])


def _attn_layer(h, g, w_q, w_o, bias, k_t, v):
    bsz, seq, _ = h.shape
    assert LEFT == ROW_TILE
    row = pl.BlockSpec((1, ROW_TILE, D_MODEL), lambda b, t: (b, t, 0))
    kt_all = pl.BlockSpec((1, D_MODEL, seq), lambda b, t: (b, 0, 0))
    v_all = pl.BlockSpec((1, seq, D_MODEL), lambda b, t: (b, 0, 0))
    return pl.pallas_call(
        _attn_kernel,
        grid=(bsz, seq // ROW_TILE),
        in_specs=[row, _const_spec((1, D_MODEL)), _const_spec((D_MODEL, D_MODEL)),
                  _const_spec((D_MODEL, D_MODEL)), _const_spec((B_HEADS, Q_BLOCK, K_BLOCK)), kt_all, v_all],
        out_specs=row,
        out_shape=jax.ShapeDtypeStruct(h.shape, F32),
        scratch_shapes=[pltpu.VMEM((ROW_TILE, D_MODEL), BF16),
                        pltpu.VMEM((ROW_TILE, D_MODEL), BF16),
                        pltpu.VMEM((B_HEADS, Q_BLOCK, K_BLOCK), F32),
                        pltpu.VMEM((B_HEADS, Q_BLOCK, K_BLOCK), BF16),
                        pltpu.VMEM((B_HEADS, Q_BLOCK, LANES), F32),
                        pltpu.VMEM((B_HEADS, Q_BLOCK, LANES), F32)],
        compiler_params=pltpu.CompilerParams(
            dimension_semantics=("arbitrary", "arbitrary"), vmem_limit_bytes=VMEM_LIMIT),
        name="attn",
    )(h, g, w_q, w_o, bias, k_t, v)


def _rel_bias(rel_table):
    n_heads = rel_table.shape[0]
    tab = rel_table.astype(F32)
    span = BAND + CHUNK - 1
    n_var = MAX_REL + CHUNK
    base = jnp.concatenate(
        [tab[:, MAX_REL - (CHUNK - 1):2 * MAX_REL + 1],
         jnp.broadcast_to(tab[:, 2 * MAX_REL:], (n_heads, span - n_var))], axis=1)
    rev = base[:, ::-1]
    padded = jnp.pad(rev, ((0, 0), (0, 1)))
    rows = jnp.tile(padded, (1, CHUNK))[:, :CHUNK * span].reshape(n_heads, CHUNK, span)
    band = rows[:, :, CHUNK - 1:CHUNK - 1 + BAND]
    ninf = jnp.full((n_heads, CHUNK, CHUNK), -jnp.inf, F32)
    return jnp.concatenate([jnp.concatenate([band, ninf], axis=2),
                            jnp.concatenate([ninf, band], axis=2)], axis=1)


def _mlstm_params(w_in, b_gate, g_head):
    w_q = w_in[:, 0:A_QK]
    w_k = w_in[:, A_QK:2 * A_QK]
    w_vo = w_in[:, 2 * A_QK:A_GATE_OFF]
    w_fm = jnp.concatenate([w_q, w_vo], axis=1).T.astype(BF16)
    pad = LANES - A_HEADS
    w_g = jnp.concatenate([jnp.pad(w_in[:, A_GATE_OFF:A_GATE_OFF + A_HEADS], ((0, 0), (0, pad))),
                           jnp.pad(w_in[:, A_GATE_OFF + A_HEADS:], ((0, 0), (0, pad)))], axis=1)
    b_g = jnp.concatenate([jnp.pad(b_gate[:A_HEADS], (0, pad)), jnp.pad(b_gate[A_HEADS:], (0, pad))])
    g_head_b = jnp.broadcast_to(g_head.astype(F32)[:, None], (A_V, LANES))
    return w_fm, w_k.astype(BF16), w_g.astype(BF16), b_g.astype(F32).reshape(1, 2 * LANES), g_head_b


def kernel(x, a_w_in, a_b_gate, a_g_head, a_w_out, b_w_q, b_rel_bias, b_w_o, kv_norm_g, w_kv,
           norm_mix_g, norm_ffn_g, ffn_w_gate, ffn_w_up, ffn_w_down, final_norm_g):
    bsz, seq, d = x.shape
    assert d == D_MODEL and seq % ROW_TILE == 0
    depth = norm_mix_g.shape[0]
    n_a = a_w_in.shape[0]

    def row(v):
        return v.reshape(1, -1).astype(F32)

    h = x
    k_t = v_sh = None
    for l in range(depth):
        if l == n_a:
            k_t, v_sh = _kv_proj(h, row(kv_norm_g), w_kv[:, :D_MODEL].T.astype(BF16),
                                 w_kv[:, D_MODEL:].astype(BF16))
        if l < n_a:
            h = _mlstm_layer(h, row(norm_mix_g[l]), *_mlstm_params(a_w_in[l], a_b_gate[l], a_g_head[l]),
                             a_w_out[l].astype(BF16))
        else:
            j = l - n_a
            h = _attn_layer(h, row(norm_mix_g[l]), b_w_q[j].astype(BF16), b_w_o[j].astype(BF16),
                            _rel_bias(b_rel_bias[j]), k_t, v_sh)
        h = _ffn(h.reshape(bsz * seq, d), row(norm_ffn_g[l]), ffn_w_gate[l].astype(BF16),
                 ffn_w_up[l].astype(BF16), ffn_w_down[l].astype(BF16), row(final_norm_g),
                 final_norm=(l == depth - 1)).reshape(bsz, seq, d)
    return h
```

```python
import functools

import jax
import jax.numpy as jnp
from jax import lax
from jax.experimental import pallas as pl
from jax.experimental.pallas import tpu as pltpu

F32 = jnp.float32
BF16 = jnp.bfloat16

D_MODEL = 1024
CHUNK = 64
A_HEADS = 8
A_DQK = 64
A_DV = 128
A_QK = A_HEADS * A_DQK
A_V = A_HEADS * A_DV
B_HEADS = 16
B_DH = 64
LEFT_CHUNKS = 8
LEFT = LEFT_CHUNKS * CHUNK
BAND = LEFT + CHUNK
MAX_REL = 256
D_FF = 2816
EPS = 1e-6
LOG2E = 1.4426950408889634
NEG_INIT = -1e30

LANES = 128
MXU_COLS = 256

ROW_TILE = 512
FFN_ROWS = 1024
A_SUPER = 128
FF_TILE = 256
VMEM_LIMIT = 56 * 1024 * 1024

A_GATE_OFF = 2 * A_QK + 2 * A_V


def _rmsnorm(x, g):
    return x * lax.rsqrt(jnp.mean(x * x, axis=-1, keepdims=True) + EPS) * g


def _dot(a, b):
    return jnp.dot(a, b, preferred_element_type=F32)


def _dot_nt(a, b):
    return lax.dot_general(a, b, (((1,), (1,)), ((), ())), preferred_element_type=F32)


def _dot_tn(a, b):
    return lax.dot_general(a, b, (((0,), (0,)), ((), ())), preferred_element_type=F32)


def _const_spec(shape):
    nd = len(shape)
    return pl.BlockSpec(shape, lambda *_: (0,) * nd, pipeline_mode=pl.Buffered(1))


def _ffn_kernel(x_ref, g_ref, wg_ref, wu_ref, wd_ref, gf_ref, o_ref, *, final_norm):
    x = x_ref[...]
    xn = _rmsnorm(x, g_ref[...]).astype(BF16)
    o_ref[...] = x
    for j in range(D_FF // FF_TILE):
        cols = slice(j * FF_TILE, (j + 1) * FF_TILE)
        hg = _dot(xn, wg_ref[:, cols])
        hu = _dot(xn, wu_ref[:, cols])
        act = (hg * jax.nn.sigmoid(hg)) * hu
        o_ref[...] += _dot(act.astype(BF16), wd_ref[cols, :])
    if final_norm:
        o_ref[...] = _rmsnorm(o_ref[...], gf_ref[...])


def _ffn(h2d, g, wg, wu, wd, gf, final_norm):
    n_tok = h2d.shape[0]
    row = pl.BlockSpec((FFN_ROWS, D_MODEL), lambda i: (i, 0))
    return pl.pallas_call(
        functools.partial(_ffn_kernel, final_norm=final_norm),
        grid=(n_tok // FFN_ROWS,),
        in_specs=[row, _const_spec((1, D_MODEL)), _const_spec((D_MODEL, D_FF)),
                  _const_spec((D_MODEL, D_FF)), _const_spec((D_FF, D_MODEL)),
                  _const_spec((1, D_MODEL))],
        out_specs=row,
        out_shape=jax.ShapeDtypeStruct(h2d.shape, F32),
        compiler_params=pltpu.CompilerParams(
            dimension_semantics=("arbitrary",), vmem_limit_bytes=VMEM_LIMIT),
        name="ffn",
    )(h2d, g, wg, wu, wd, gf)


A_CEXT = A_DV + 16
A_STEPS = ROW_TILE // A_SUPER


def _log_sigmoid(x):
    return jnp.minimum(x, 0.0) - jnp.log1p(jnp.exp(-jnp.abs(x)))


def _mlstm_kernel(x_ref, g_ref, wfm_ref, wk_ref, wg_ref, bg_ref, ghb_ref, wout_ref, o_ref,
                  qt_s, k_s, vt_s, ogt_s, hst_s, imb_s, bt_s, imbt_s, c_s, m_s):
    t = pl.program_id(1)

    @pl.when(t == 0)
    def _():
        c_s[...] = jnp.zeros_like(c_s)
        m_s[...] = jnp.full_like(m_s, NEG_INIT)

    x = x_ref[0]
    xn = _rmsnorm(x, g_ref[...]).astype(BF16)
    qt_s[...] = _dot_nt(wfm_ref[0:A_QK, :], xn).astype(BF16)
    vt_s[...] = _dot_nt(wfm_ref[A_QK:A_QK + A_V, :], xn).astype(BF16)
    ogt_s[...] = jax.nn.sigmoid(_dot_nt(wfm_ref[A_QK + A_V:A_QK + 2 * A_V, :], xn))
    k_s[...] = (_dot(xn, wk_ref[...]) * (A_DQK ** -0.5)).astype(BF16)
    gates = _dot(xn, wg_ref[...]) + bg_ref[...]
    i_log = gates[:, 0:LANES]
    f_log = _log_sigmoid(gates[:, LANES:2 * LANES])

    rows = lax.broadcasted_iota(jnp.int32, (ROW_TILE, LANES), 0) & (A_SUPER - 1)
    b_cum = f_log
    shift = 1
    while shift < A_SUPER:
        rolled = pltpu.roll(b_cum, shift, axis=0)
        b_cum = b_cum + jnp.where(rows >= shift, rolled, 0.0)
        shift *= 2
    imb = i_log - b_cum
    imb_s[...] = imb
    bt_s[...] = b_cum.T[0:A_HEADS]
    imbt_s[...] = imb.T[0:A_HEADS]

    row_i = lax.broadcasted_iota(jnp.int32, (A_SUPER, A_SUPER), 0)
    col_i = lax.broadcasted_iota(jnp.int32, (A_SUPER, A_SUPER), 1)
    causal = col_i >= row_i
    first_rows = row_i < A_DQK
    first_lanes = col_i < A_DQK
    ext_row = lax.broadcasted_iota(jnp.int32, (A_CEXT - A_DV, LANES), 0) == 0
    lane_row = lax.broadcasted_iota(jnp.int32, (1, LANES), 1) < A_DQK

    def pair_queries(p, r):
        qp = qt_s[p * LANES:(p + 1) * LANES, r]
        zero = jnp.zeros_like(qp)
        return jnp.concatenate([jnp.where(first_rows, qp, zero), jnp.where(first_rows, zero, qp)], axis=1)

    def pair_step(c, p):
        r = slice(c * A_SUPER, (c + 1) * A_SUPER)
        q_pair = pair_queries(p, r)
        kp = k_s[r, p * LANES:(p + 1) * LANES]
        st_pair = _dot(kp, q_pair)
        qc_pair = _dot(c_s[p].astype(BF16), q_pair)
        lhs, scale = [], []
        for half in range(2):
            h = 2 * p + half
            hr = slice(h * A_DV, (h + 1) * A_DV)
            vt_h = vt_s[hr, r]
            b_row = bt_s[h:h + 1, r]
            imb_row = imbt_s[h:h + 1, r]
            imb_col = imb_s[r, h:h + 1]
            b_last = jnp.broadcast_to(b_row[:, A_SUPER - 1:A_SUPER], (1, LANES))
            m_prev = m_s[h]

            log_dt = jnp.where(causal, b_row + imb_col, -jnp.inf)
            mrow = jnp.max(log_dt, axis=0, keepdims=True)
            st = st_pair[:, half * A_SUPER:(half + 1) * A_SUPER] * jnp.exp(log_dt - mrow)
            svt = _dot(vt_h, st.astype(BF16))
            rs = jnp.sum(st, axis=0, keepdims=True)

            inter = b_row + m_prev
            m_j = jnp.maximum(inter, mrow)
            f_intra = jnp.exp(mrow - m_j)
            w_inter = jnp.exp(inter - m_j)
            qc = qc_pair[:, half * A_SUPER:(half + 1) * A_SUPER]
            num = f_intra * svt + w_inter * qc[0:A_DV]
            den = f_intra * rs + w_inter * qc[A_DV:A_DV + 1]
            ht = num * (1.0 / jnp.maximum(jnp.abs(den), jnp.exp(-m_j)))
            hn = ht * lax.rsqrt(jnp.mean(ht * ht, axis=0, keepdims=True) + EPS)
            hst_s[hr, r] = (ogt_s[hr, r] * (hn * ghb_ref[hr, :])).astype(BF16)

            a_row = b_last + imb_row
            amax = jnp.broadcast_to(jnp.max(a_row, axis=1, keepdims=True), (1, LANES))
            wa = jnp.exp(a_row - amax)
            vw = (vt_h.astype(F32) * wa).astype(BF16)
            ext = jnp.where(ext_row, jnp.broadcast_to(wa, (A_CEXT - A_DV, LANES)), 0.0).astype(BF16)
            lhs.append(jnp.concatenate([vw, ext], axis=0))

            m_new = jnp.maximum(b_last + m_prev, amax)
            scale.append((jnp.exp(b_last + m_prev - m_new), jnp.exp(amax - m_new)))
            m_s[h] = m_new
        k_zero = jnp.zeros_like(kp)
        k_split = jnp.concatenate([jnp.where(first_lanes, kp, k_zero),
                                   jnp.where(first_lanes, k_zero, kp)], axis=0)
        up = _dot(jnp.concatenate(lhs, axis=1), k_split)
        decay = jnp.where(lane_row, scale[0][0], scale[1][0])
        grow = jnp.where(lane_row, scale[0][1], scale[1][1])
        c_s[p] = decay * c_s[p] + grow * up

    @pl.when(t >= 0)
    def _():
        for c in range(A_STEPS):
            for p in range(A_HEADS // 2):
                pair_step(c, p)

    o_ref[0] = x + _dot_tn(hst_s[...], wout_ref[...])


def _mlstm_layer(h, g, w_fm, w_k, w_g, b_g, g_head_b, w_out):
    bsz, seq, _ = h.shape
    row = pl.BlockSpec((1, ROW_TILE, D_MODEL), lambda b, t: (b, t, 0))
    return pl.pallas_call(
        _mlstm_kernel,
        grid=(bsz, seq // ROW_TILE),
        in_specs=[row, _const_spec((1, D_MODEL)), _const_spec((A_QK + 2 * A_V, D_MODEL)),
                  _const_spec((D_MODEL, A_QK)), _const_spec((D_MODEL, 2 * LANES)),
                  _const_spec((1, 2 * LANES)), _const_spec((A_V, LANES)), _const_spec((A_V, D_MODEL))],
        out_specs=row,
        out_shape=jax.ShapeDtypeStruct(h.shape, F32),
        scratch_shapes=[
            pltpu.VMEM((A_QK, ROW_TILE), BF16),
            pltpu.VMEM((ROW_TILE, A_QK), BF16),
            pltpu.VMEM((A_V, ROW_TILE), BF16),
            pltpu.VMEM((A_V, ROW_TILE), F32),
            pltpu.VMEM((A_V, ROW_TILE), BF16),
            pltpu.VMEM((ROW_TILE, LANES), F32),
            pltpu.VMEM((A_HEADS, ROW_TILE), F32),
            pltpu.VMEM((A_HEADS, ROW_TILE), F32),
            pltpu.VMEM((A_HEADS // 2, A_CEXT, LANES), F32),
            pltpu.VMEM((A_HEADS, 1, LANES), F32),
        ],
        compiler_params=pltpu.CompilerParams(
            dimension_semantics=("arbitrary", "arbitrary"), vmem_limit_bytes=VMEM_LIMIT),
        name="mlstm",
    )(h, g, w_fm, w_k, w_g, b_g, g_head_b, w_out)


def _kv_kernel(x_ref, g_ref, wkt_ref, wv_ref, kt_ref, v_ref):
    xn = _rmsnorm(x_ref[0], g_ref[...]).astype(BF16)
    kt_ref[0] = _dot_nt(wkt_ref[...], xn).astype(BF16)
    v_ref[0] = _dot(xn, wv_ref[...]).astype(BF16)


def _kv_proj(h, g, w_kt, w_v):
    bsz, seq, _ = h.shape
    row = pl.BlockSpec((1, ROW_TILE, D_MODEL), lambda b, t: (b, t, 0))
    col = pl.BlockSpec((1, D_MODEL, ROW_TILE), lambda b, t: (b, 0, t))
    return pl.pallas_call(
        _kv_kernel,
        grid=(bsz, seq // ROW_TILE),
        in_specs=[row, _const_spec((1, D_MODEL)), _const_spec((D_MODEL, D_MODEL)),
                  _const_spec((D_MODEL, D_MODEL))],
        out_specs=[col, row],
        out_shape=[jax.ShapeDtypeStruct((bsz, D_MODEL, seq), BF16),
                   jax.ShapeDtypeStruct((bsz, seq, D_MODEL), BF16)],
        compiler_params=pltpu.CompilerParams(
            dimension_semantics=("arbitrary", "arbitrary"), vmem_limit_bytes=VMEM_LIMIT),
        name="kv_proj",
    )(h, g, w_kt, w_v)


Q_BLOCK = 2 * CHUNK
K_BLOCK = LEFT + Q_BLOCK


def _attn_kernel(x_ref, g_ref, wq_ref, wo_ref, bias_ref, kt_ref, v_ref, o_ref,
                 q_s, a_s, s_s, p_s, m_s, l_s):
    t = pl.program_id(1)
    x = x_ref[0]
    xn = _rmsnorm(x, g_ref[...]).astype(BF16)
    q_s[...] = (_dot(xn, wq_ref[...]) * (B_DH ** -0.5 * LOG2E)).astype(BF16)

    lane = lax.broadcasted_iota(jnp.int32, (Q_BLOCK, LANES), 1)
    lo = lane < B_DH
    n_tiles = K_BLOCK // LANES
    n_blocks = ROW_TILE // Q_BLOCK
    n_pairs = B_HEADS // 2
    left_tiles = LEFT // LANES

    def band(i, first_tile):
        if first_tile:
            first = left_tiles - i
            keys = pl.ds(0, (n_tiles - first) * LANES)
        else:
            first = 0
            keys = pl.ds(pl.multiple_of((t - 1) * ROW_TILE, ROW_TILE) + i * Q_BLOCK, K_BLOCK)
        return first, keys, slice(first * LANES, K_BLOCK)

    def scores(i, p, first_tile):
        pl_ = slice(p * LANES, (p + 1) * LANES)
        _, keys, cols = band(i, first_tile)
        qp = q_s[i * Q_BLOCK:(i + 1) * Q_BLOCK, pl_]
        kp = kt_ref[0, pl_, keys]
        for half in range(2):
            h = 2 * p + half
            own = lo if half == 0 else jnp.logical_not(lo)
            s = _dot(jnp.where(own, qp, jnp.zeros_like(qp)), kp) + bias_ref[h, :, cols]
            s_s[h, :, cols] = s
            m_s[h] = jnp.broadcast_to(jnp.max(s, axis=1, keepdims=True), (Q_BLOCK, LANES))

    def exps(i, p, first_tile):
        first, _, _ = band(i, first_tile)
        for h in (2 * p, 2 * p + 1):
            m = m_s[h]
            acc = jnp.zeros((Q_BLOCK, LANES), F32)
            for j in range(first, n_tiles):
                tl = slice(j * LANES, (j + 1) * LANES)
                e = jnp.exp2(s_s[h, :, tl] - m)
                acc = acc + e
                p_s[h, :, tl] = e.astype(BF16)
            l_s[h] = jnp.broadcast_to(jnp.sum(acc, axis=1, keepdims=True), (Q_BLOCK, LANES))

    def values(i, p, first_tile):
        pl_ = slice(p * LANES, (p + 1) * LANES)
        _, keys, cols = band(i, first_tile)
        vp = v_ref[0, keys, pl_]
        o0 = _dot(p_s[2 * p, :, cols], vp) / l_s[2 * p]
        o1 = _dot(p_s[2 * p + 1, :, cols], vp) / l_s[2 * p + 1]
        a_s[i * Q_BLOCK:(i + 1) * Q_BLOCK, pl_] = jnp.where(lo, o0, o1).astype(BF16)

    def block(i, first_tile):
        for stage in (scores, exps, values):
            for p in range(n_pairs):
                stage(i, p, first_tile)

    for i in range(n_blocks):
        pl.when(t == 0)(functools.partial(block, i, True))
        pl.when(t > 0)(functools.partial(block, i, False))

    o_ref[0] = x + _dot(a_s[...], wo_ref[...])


def _attn_layer(h, g, w_q, w_o, bias, k_t, v):
    bsz, seq, _ = h.shape
    assert LEFT == ROW_TILE
    row = pl.BlockSpec((1, ROW_TILE, D_MODEL), lambda b, t: (b, t, 0))
    kt_all = pl.BlockSpec((1, D_MODEL, seq), lambda b, t: (b, 0, 0))
    v_all = pl.BlockSpec((1, seq, D_MODEL), lambda b, t: (b, 0, 0))
    return pl.pallas_call(
        _attn_kernel,
        grid=(bsz, seq // ROW_TILE),
        in_specs=[row, _const_spec((1, D_MODEL)), _const_spec((D_MODEL, D_MODEL)),
                  _const_spec((D_MODEL, D_MODEL)), _const_spec((B_HEADS, Q_BLOCK, K_BLOCK)), kt_all, v_all],
        out_specs=row,
        out_shape=jax.ShapeDtypeStruct(h.shape, F32),
        scratch_shapes=[pltpu.VMEM((ROW_TILE, D_MODEL), BF16),
                        pltpu.VMEM((ROW_TILE, D_MODEL), BF16),
                        pltpu.VMEM((B_HEADS, Q_BLOCK, K_BLOCK), F32),
                        pltpu.VMEM((B_HEADS, Q_BLOCK, K_BLOCK), BF16),
                        pltpu.VMEM((B_HEADS, Q_BLOCK, LANES), F32),
                        pltpu.VMEM((B_HEADS, Q_BLOCK, LANES), F32)],
        compiler_params=pltpu.CompilerParams(
            dimension_semantics=("arbitrary", "arbitrary"), vmem_limit_bytes=VMEM_LIMIT),
        name="attn",
    )(h, g, w_q, w_o, bias, k_t, v)


def _rel_bias(rel_table):
    n_heads = rel_table.shape[0]
    tab = rel_table.astype(F32) * LOG2E
    span = BAND + CHUNK - 1
    n_var = MAX_REL + CHUNK
    base = jnp.concatenate(
        [tab[:, MAX_REL - (CHUNK - 1):2 * MAX_REL + 1],
         jnp.broadcast_to(tab[:, 2 * MAX_REL:], (n_heads, span - n_var))], axis=1)
    rev = base[:, ::-1]
    padded = jnp.pad(rev, ((0, 0), (0, 1)))
    rows = jnp.tile(padded, (1, CHUNK))[:, :CHUNK * span].reshape(n_heads, CHUNK, span)
    band = rows[:, :, CHUNK - 1:CHUNK - 1 + BAND]
    ninf = jnp.full((n_heads, CHUNK, CHUNK), -jnp.inf, F32)
    return jnp.concatenate([jnp.concatenate([band, ninf], axis=2),
                            jnp.concatenate([ninf, band], axis=2)], axis=1)


def _mlstm_params(w_in, b_gate, g_head):
    w_q = w_in[:, 0:A_QK]
    w_k = w_in[:, A_QK:2 * A_QK]
    w_vo = w_in[:, 2 * A_QK:A_GATE_OFF]
    w_fm = jnp.concatenate([w_q, w_vo], axis=1).T.astype(BF16)
    pad = LANES - A_HEADS
    w_g = jnp.concatenate([jnp.pad(w_in[:, A_GATE_OFF:A_GATE_OFF + A_HEADS], ((0, 0), (0, pad))),
                           jnp.pad(w_in[:, A_GATE_OFF + A_HEADS:], ((0, 0), (0, pad)))], axis=1)
    b_g = jnp.concatenate([jnp.pad(b_gate[:A_HEADS], (0, pad)), jnp.pad(b_gate[A_HEADS:], (0, pad))])
    g_head_b = jnp.broadcast_to(g_head.astype(F32)[:, None], (A_V, LANES))
    return w_fm, w_k.astype(BF16), w_g.astype(BF16), b_g.astype(F32).reshape(1, 2 * LANES), g_head_b


def kernel(x, a_w_in, a_b_gate, a_g_head, a_w_out, b_w_q, b_rel_bias, b_w_o, kv_norm_g, w_kv,
           norm_mix_g, norm_ffn_g, ffn_w_gate, ffn_w_up, ffn_w_down, final_norm_g):
    bsz, seq, d = x.shape
    assert d == D_MODEL and seq % ROW_TILE == 0
    depth = norm_mix_g.shape[0]
    n_a = a_w_in.shape[0]

    def row(v):
        return v.reshape(1, -1).astype(F32)

    h = x
    k_t = v_sh = None
    for l in range(depth):
        if l == n_a:
            k_t, v_sh = _kv_proj(h, row(kv_norm_g), w_kv[:, :D_MODEL].T.astype(BF16),
                                 w_kv[:, D_MODEL:].astype(BF16))
        if l < n_a:
            h = _mlstm_layer(h, row(norm_mix_g[l]), *_mlstm_params(a_w_in[l], a_b_gate[l], a_g_head[l]),
                             a_w_out[l].astype(BF16))
        else:
            j = l - n_a
            h = _attn_layer(h, row(norm_mix_g[l]), b_w_q[j].astype(BF16), b_w_o[j].astype(BF16),
                            _rel_bias(b_rel_bias[j]), k_t, v_sh)
        h = _ffn(h.reshape(bsz * seq, d), row(norm_ffn_g[l]), ffn_w_gate[l].astype(BF16),
                 ffn_w_up[l].astype(BF16), ffn_w_down[l].astype(BF16), row(final_norm_g),
                 final_norm=(l == depth - 1)).reshape(bsz, seq, d)
    return h
```

```python
import functools

import jax
import jax.numpy as jnp
from jax import lax
from jax.experimental import pallas as pl
from jax.experimental.pallas import tpu as pltpu

F32 = jnp.float32
BF16 = jnp.bfloat16

D_MODEL = 1024
CHUNK = 64
A_HEADS = 8
A_DQK = 64
A_DV = 128
A_QK = A_HEADS * A_DQK
A_V = A_HEADS * A_DV
B_HEADS = 16
B_DH = 64
LEFT_CHUNKS = 8
LEFT = LEFT_CHUNKS * CHUNK
BAND = LEFT + CHUNK
MAX_REL = 256
D_FF = 2816
EPS = 1e-6
LOG2E = 1.4426950408889634
SCORE_LIMIT = 96.0
NEG_INIT = -1e30

LANES = 128
MXU_COLS = 256

ROW_TILE = 512
FFN_ROWS = 1024
A_SUPER = 128
FF_TILE = 256
VMEM_LIMIT = 56 * 1024 * 1024

A_GATE_OFF = 2 * A_QK + 2 * A_V


def _rmsnorm(x, g):
    return x * lax.rsqrt(jnp.mean(x * x, axis=-1, keepdims=True) + EPS) * g


def _dot(a, b):
    return jnp.dot(a, b, preferred_element_type=F32)


def _dot_nt(a, b):
    return lax.dot_general(a, b, (((1,), (1,)), ((), ())), preferred_element_type=F32)


def _dot_tn(a, b):
    return lax.dot_general(a, b, (((0,), (0,)), ((), ())), preferred_element_type=F32)


def _const_spec(shape):
    nd = len(shape)
    return pl.BlockSpec(shape, lambda *_: (0,) * nd, pipeline_mode=pl.Buffered(1))


def _ffn_kernel(x_ref, g_ref, wg_ref, wu_ref, wd_ref, gf_ref, o_ref, *, final_norm):
    x = x_ref[...]
    xn = _rmsnorm(x, g_ref[...]).astype(BF16)
    o_ref[...] = x
    for j in range(D_FF // FF_TILE):
        cols = slice(j * FF_TILE, (j + 1) * FF_TILE)
        hg = _dot(xn, wg_ref[:, cols])
        hu = _dot(xn, wu_ref[:, cols])
        act = (hg * jax.nn.sigmoid(hg)) * hu
        o_ref[...] += _dot(act.astype(BF16), wd_ref[cols, :])
    if final_norm:
        o_ref[...] = _rmsnorm(o_ref[...], gf_ref[...])


def _ffn(h2d, g, wg, wu, wd, gf, final_norm):
    n_tok = h2d.shape[0]
    row = pl.BlockSpec((FFN_ROWS, D_MODEL), lambda i: (i, 0))
    return pl.pallas_call(
        functools.partial(_ffn_kernel, final_norm=final_norm),
        grid=(n_tok // FFN_ROWS,),
        in_specs=[row, _const_spec((1, D_MODEL)), _const_spec((D_MODEL, D_FF)),
                  _const_spec((D_MODEL, D_FF)), _const_spec((D_FF, D_MODEL)),
                  _const_spec((1, D_MODEL))],
        out_specs=row,
        out_shape=jax.ShapeDtypeStruct(h2d.shape, F32),
        compiler_params=pltpu.CompilerParams(
            dimension_semantics=("arbitrary",), vmem_limit_bytes=VMEM_LIMIT),
        name="ffn",
    )(h2d, g, wg, wu, wd, gf)


A_CEXT = A_DV + 16
A_STEPS = ROW_TILE // A_SUPER


def _log_sigmoid(x):
    return jnp.minimum(x, 0.0) - jnp.log1p(jnp.exp(-jnp.abs(x)))


def _mlstm_kernel(x_ref, g_ref, wfm_ref, wk_ref, wg_ref, bg_ref, ghb_ref, wout_ref, o_ref,
                  qt_s, k_s, vt_s, ogt_s, hst_s, imb_s, bt_s, imbt_s, c_s, m_s):
    t = pl.program_id(1)

    @pl.when(t == 0)
    def _():
        c_s[...] = jnp.zeros_like(c_s)
        m_s[...] = jnp.full_like(m_s, NEG_INIT)

    x = x_ref[0]
    xn = _rmsnorm(x, g_ref[...]).astype(BF16)
    qt_s[...] = _dot_nt(wfm_ref[0:A_QK, :], xn).astype(BF16)
    vt_s[...] = _dot_nt(wfm_ref[A_QK:A_QK + A_V, :], xn).astype(BF16)
    ogt_s[...] = jax.nn.sigmoid(_dot_nt(wfm_ref[A_QK + A_V:A_QK + 2 * A_V, :], xn))
    k_s[...] = (_dot(xn, wk_ref[...]) * (A_DQK ** -0.5)).astype(BF16)
    gates = _dot(xn, wg_ref[...]) + bg_ref[...]
    i_log = gates[:, 0:LANES]
    f_log = _log_sigmoid(gates[:, LANES:2 * LANES])

    rows = lax.broadcasted_iota(jnp.int32, (ROW_TILE, LANES), 0) & (A_SUPER - 1)
    b_cum = f_log
    shift = 1
    while shift < A_SUPER:
        rolled = pltpu.roll(b_cum, shift, axis=0)
        b_cum = b_cum + jnp.where(rows >= shift, rolled, 0.0)
        shift *= 2
    imb = i_log - b_cum
    imb_s[...] = imb
    bt_s[...] = b_cum.T[0:A_HEADS]
    imbt_s[...] = imb.T[0:A_HEADS]

    row_i = lax.broadcasted_iota(jnp.int32, (A_SUPER, A_SUPER), 0)
    col_i = lax.broadcasted_iota(jnp.int32, (A_SUPER, A_SUPER), 1)
    causal = col_i >= row_i
    first_rows = row_i < A_DQK
    first_lanes = col_i < A_DQK
    ext_row = lax.broadcasted_iota(jnp.int32, (A_CEXT - A_DV, LANES), 0) == 0
    lane_row = lax.broadcasted_iota(jnp.int32, (1, LANES), 1) < A_DQK

    def pair_queries(p, r):
        qp = qt_s[p * LANES:(p + 1) * LANES, r]
        zero = jnp.zeros_like(qp)
        return jnp.concatenate([jnp.where(first_rows, qp, zero), jnp.where(first_rows, zero, qp)], axis=1)

    def pair_step(c, p):
        r = slice(c * A_SUPER, (c + 1) * A_SUPER)
        q_pair = pair_queries(p, r)
        kp = k_s[r, p * LANES:(p + 1) * LANES]
        st_pair = _dot(kp, q_pair)
        qc_pair = _dot(c_s[p].astype(BF16), q_pair)
        lhs, scale = [], []
        for half in range(2):
            h = 2 * p + half
            hr = slice(h * A_DV, (h + 1) * A_DV)
            vt_h = vt_s[hr, r]
            b_row = bt_s[h:h + 1, r]
            imb_row = imbt_s[h:h + 1, r]
            imb_col = imb_s[r, h:h + 1]
            b_last = jnp.broadcast_to(b_row[:, A_SUPER - 1:A_SUPER], (1, LANES))
            m_prev = m_s[h]

            log_dt = jnp.where(causal, b_row + imb_col, -jnp.inf)
            mrow = jnp.max(log_dt, axis=0, keepdims=True)
            st = st_pair[:, half * A_SUPER:(half + 1) * A_SUPER] * jnp.exp(log_dt - mrow)
            svt = _dot(vt_h, st.astype(BF16))
            rs = jnp.sum(st, axis=0, keepdims=True)

            inter = b_row + m_prev
            m_j = jnp.maximum(inter, mrow)
            f_intra = jnp.exp(mrow - m_j)
            w_inter = jnp.exp(inter - m_j)
            qc = qc_pair[:, half * A_SUPER:(half + 1) * A_SUPER]
            num = f_intra * svt + w_inter * qc[0:A_DV]
            den = f_intra * rs + w_inter * qc[A_DV:A_DV + 1]
            ht = num * (1.0 / jnp.maximum(jnp.abs(den), jnp.exp(-m_j)))
            hn = ht * lax.rsqrt(jnp.mean(ht * ht, axis=0, keepdims=True) + EPS)
            hst_s[hr, r] = (ogt_s[hr, r] * (hn * ghb_ref[hr, :])).astype(BF16)

            a_row = b_last + imb_row
            amax = jnp.broadcast_to(jnp.max(a_row, axis=1, keepdims=True), (1, LANES))
            wa = jnp.exp(a_row - amax)
            vw = (vt_h.astype(F32) * wa).astype(BF16)
            ext = jnp.where(ext_row, jnp.broadcast_to(wa, (A_CEXT - A_DV, LANES)), 0.0).astype(BF16)
            lhs.append(jnp.concatenate([vw, ext], axis=0))

            m_new = jnp.maximum(b_last + m_prev, amax)
            scale.append((jnp.exp(b_last + m_prev - m_new), jnp.exp(amax - m_new)))
            m_s[h] = m_new
        k_zero = jnp.zeros_like(kp)
        k_split = jnp.concatenate([jnp.where(first_lanes, kp, k_zero),
                                   jnp.where(first_lanes, k_zero, kp)], axis=0)
        up = _dot(jnp.concatenate(lhs, axis=1), k_split)
        decay = jnp.where(lane_row, scale[0][0], scale[1][0])
        grow = jnp.where(lane_row, scale[0][1], scale[1][1])
        c_s[p] = decay * c_s[p] + grow * up

    @pl.when(t >= 0)
    def _():
        for c in range(A_STEPS):
            for p in range(A_HEADS // 2):
                pair_step(c, p)

    o_ref[0] = x + _dot_tn(hst_s[...], wout_ref[...])


def _mlstm_layer(h, g, w_fm, w_k, w_g, b_g, g_head_b, w_out):
    bsz, seq, _ = h.shape
    row = pl.BlockSpec((1, ROW_TILE, D_MODEL), lambda b, t: (b, t, 0))
    return pl.pallas_call(
        _mlstm_kernel,
        grid=(bsz, seq // ROW_TILE),
        in_specs=[row, _const_spec((1, D_MODEL)), _const_spec((A_QK + 2 * A_V, D_MODEL)),
                  _const_spec((D_MODEL, A_QK)), _const_spec((D_MODEL, 2 * LANES)),
                  _const_spec((1, 2 * LANES)), _const_spec((A_V, LANES)), _const_spec((A_V, D_MODEL))],
        out_specs=row,
        out_shape=jax.ShapeDtypeStruct(h.shape, F32),
        scratch_shapes=[
            pltpu.VMEM((A_QK, ROW_TILE), BF16),
            pltpu.VMEM((ROW_TILE, A_QK), BF16),
            pltpu.VMEM((A_V, ROW_TILE), BF16),
            pltpu.VMEM((A_V, ROW_TILE), F32),
            pltpu.VMEM((A_V, ROW_TILE), BF16),
            pltpu.VMEM((ROW_TILE, LANES), F32),
            pltpu.VMEM((A_HEADS, ROW_TILE), F32),
            pltpu.VMEM((A_HEADS, ROW_TILE), F32),
            pltpu.VMEM((A_HEADS // 2, A_CEXT, LANES), F32),
            pltpu.VMEM((A_HEADS, 1, LANES), F32),
        ],
        compiler_params=pltpu.CompilerParams(
            dimension_semantics=("arbitrary", "arbitrary"), vmem_limit_bytes=VMEM_LIMIT),
        name="mlstm",
    )(h, g, w_fm, w_k, w_g, b_g, g_head_b, w_out)


def _kv_kernel(x_ref, g_ref, wkt_ref, wv_ref, kt_ref, v_ref, kn_ref):
    xn = _rmsnorm(x_ref[0], g_ref[...]).astype(BF16)
    kt = _dot_nt(wkt_ref[...], xn).astype(BF16)
    kt_ref[0] = kt
    v_ref[0] = _dot(xn, wv_ref[...]).astype(BF16)
    kf = kt.astype(F32)
    sq = kf * kf
    for p in range(B_HEADS // 2):
        norm2 = jnp.sum(sq[p * LANES:(p + 1) * LANES], axis=0, keepdims=True)
        kn_ref[0, 0, p:p + 1, :] = jnp.broadcast_to(jnp.max(norm2, axis=1, keepdims=True), (1, LANES))


def _kv_proj(h, g, w_kt, w_v):
    bsz, seq, _ = h.shape
    row = pl.BlockSpec((1, ROW_TILE, D_MODEL), lambda b, t: (b, t, 0))
    col = pl.BlockSpec((1, D_MODEL, ROW_TILE), lambda b, t: (b, 0, t))
    return pl.pallas_call(
        _kv_kernel,
        grid=(bsz, seq // ROW_TILE),
        in_specs=[row, _const_spec((1, D_MODEL)), _const_spec((D_MODEL, D_MODEL)),
                  _const_spec((D_MODEL, D_MODEL))],
        out_specs=[col, row, pl.BlockSpec((1, 1, B_HEADS // 2, LANES), lambda b, t: (b, t, 0, 0))],
        out_shape=[jax.ShapeDtypeStruct((bsz, D_MODEL, seq), BF16),
                   jax.ShapeDtypeStruct((bsz, seq, D_MODEL), BF16),
                   jax.ShapeDtypeStruct((bsz, seq // ROW_TILE, B_HEADS // 2, LANES), F32)],
        compiler_params=pltpu.CompilerParams(
            dimension_semantics=("arbitrary", "arbitrary"), vmem_limit_bytes=VMEM_LIMIT),
        name="kv_proj",
    )(h, g, w_kt, w_v)


Q_BLOCK = 2 * CHUNK
K_BLOCK = LEFT + Q_BLOCK


def _attn_kernel(x_ref, g_ref, wq_ref, wo_ref, bias_ref, kn_ref, lim_ref, kt_ref, v_ref, o_ref,
                 q_s, a_s, s_s, p_s, m_s, l_s):
    t = pl.program_id(1)
    x = x_ref[0]
    xn = _rmsnorm(x, g_ref[...]).astype(BF16)
    q = _dot(xn, wq_ref[...]) * (B_DH ** -0.5 * LOG2E)
    q_s[...] = q.astype(BF16)

    excess = None
    for p in range(B_HEADS // 2):
        qp = q[:, p * LANES:(p + 1) * LANES]
        q_norm2 = jnp.max(jnp.sum(qp * qp, axis=1, keepdims=True))
        e = q_norm2 * kn_ref[0, p:p + 1, :] - lim_ref[p:p + 1, :]
        excess = e if excess is None else jnp.maximum(excess, e)
    bounded = jnp.max(excess) <= 0.0

    lane = lax.broadcasted_iota(jnp.int32, (Q_BLOCK, LANES), 1)
    lo = lane < B_DH
    n_tiles = K_BLOCK // LANES
    n_blocks = ROW_TILE // Q_BLOCK
    n_pairs = B_HEADS // 2
    left_tiles = LEFT // LANES

    def band(i, first_tile):
        if first_tile:
            first = left_tiles - i
            keys = pl.ds(0, (n_tiles - first) * LANES)
        else:
            first = 0
            keys = pl.ds(pl.multiple_of((t - 1) * ROW_TILE, ROW_TILE) + i * Q_BLOCK, K_BLOCK)
        return first, keys, slice(first * LANES, K_BLOCK)

    def scores(i, p, first_tile):
        pl_ = slice(p * LANES, (p + 1) * LANES)
        _, keys, cols = band(i, first_tile)
        qp = q_s[i * Q_BLOCK:(i + 1) * Q_BLOCK, pl_]
        kp = kt_ref[0, pl_, keys]
        for half in range(2):
            h = 2 * p + half
            own = lo if half == 0 else jnp.logical_not(lo)
            s = _dot(jnp.where(own, qp, jnp.zeros_like(qp)), kp) + bias_ref[h, :, cols]
            s_s[h, :, cols] = s
            m_s[h] = jnp.broadcast_to(jnp.max(s, axis=1, keepdims=True), (Q_BLOCK, LANES))

    def scores_bounded(i, p, first_tile):
        pl_ = slice(p * LANES, (p + 1) * LANES)
        first, keys, cols = band(i, first_tile)
        qp = q_s[i * Q_BLOCK:(i + 1) * Q_BLOCK, pl_]
        kp = kt_ref[0, pl_, keys]
        for half in range(2):
            h = 2 * p + half
            own = lo if half == 0 else jnp.logical_not(lo)
            s = _dot(jnp.where(own, qp, jnp.zeros_like(qp)), kp) + bias_ref[h, :, cols]
            acc = jnp.zeros((Q_BLOCK, LANES), F32)
            for j in range(n_tiles - first):
                e = jnp.exp2(s[:, j * LANES:(j + 1) * LANES])
                acc = acc + e
                p_s[h, :, (first + j) * LANES:(first + j + 1) * LANES] = e.astype(BF16)
            l_s[h] = jnp.broadcast_to(jnp.sum(acc, axis=1, keepdims=True), (Q_BLOCK, LANES))

    def exps(i, p, first_tile):
        first, _, _ = band(i, first_tile)
        for h in (2 * p, 2 * p + 1):
            m = m_s[h]
            acc = jnp.zeros((Q_BLOCK, LANES), F32)
            for j in range(first, n_tiles):
                tl = slice(j * LANES, (j + 1) * LANES)
                e = jnp.exp2(s_s[h, :, tl] - m)
                acc = acc + e
                p_s[h, :, tl] = e.astype(BF16)
            l_s[h] = jnp.broadcast_to(jnp.sum(acc, axis=1, keepdims=True), (Q_BLOCK, LANES))

    def values(i, p, first_tile):
        pl_ = slice(p * LANES, (p + 1) * LANES)
        _, keys, cols = band(i, first_tile)
        vp = v_ref[0, keys, pl_]
        o0 = _dot(p_s[2 * p, :, cols], vp) / l_s[2 * p]
        o1 = _dot(p_s[2 * p + 1, :, cols], vp) / l_s[2 * p + 1]
        a_s[i * Q_BLOCK:(i + 1) * Q_BLOCK, pl_] = jnp.where(lo, o0, o1).astype(BF16)

    def block(i, first_tile, stages):
        for stage in stages:
            for p in range(n_pairs):
                stage(i, p, first_tile)

    first = t == 0
    for i in range(n_blocks):
        for stages, use in (((scores_bounded, values), bounded),
                            ((scores, exps, values), jnp.logical_not(bounded))):
            pl.when(jnp.logical_and(use, first))(functools.partial(block, i, True, stages))
            pl.when(jnp.logical_and(use, jnp.logical_not(first)))(functools.partial(block, i, False, stages))

    o_ref[0] = x + _dot(a_s[...], wo_ref[...])


def _attn_layer(h, g, w_q, w_o, bias, k_norm2, limit2, k_t, v):
    bsz, seq, _ = h.shape
    assert LEFT == ROW_TILE
    row = pl.BlockSpec((1, ROW_TILE, D_MODEL), lambda b, t: (b, t, 0))
    kt_all = pl.BlockSpec((1, D_MODEL, seq), lambda b, t: (b, 0, 0))
    v_all = pl.BlockSpec((1, seq, D_MODEL), lambda b, t: (b, 0, 0))
    return pl.pallas_call(
        _attn_kernel,
        grid=(bsz, seq // ROW_TILE),
        in_specs=[row, _const_spec((1, D_MODEL)), _const_spec((D_MODEL, D_MODEL)),
                  _const_spec((D_MODEL, D_MODEL)), _const_spec((B_HEADS, Q_BLOCK, K_BLOCK)),
                  pl.BlockSpec((1, B_HEADS // 2, LANES), lambda b, t: (b, 0, 0)),
                  _const_spec((B_HEADS // 2, LANES)), kt_all, v_all],
        out_specs=row,
        out_shape=jax.ShapeDtypeStruct(h.shape, F32),
        scratch_shapes=[pltpu.VMEM((ROW_TILE, D_MODEL), BF16),
                        pltpu.VMEM((ROW_TILE, D_MODEL), BF16),
                        pltpu.VMEM((B_HEADS, Q_BLOCK, K_BLOCK), F32),
                        pltpu.VMEM((B_HEADS, Q_BLOCK, K_BLOCK), BF16),
                        pltpu.VMEM((B_HEADS, Q_BLOCK, LANES), F32),
                        pltpu.VMEM((B_HEADS, Q_BLOCK, LANES), F32)],
        compiler_params=pltpu.CompilerParams(
            dimension_semantics=("arbitrary", "arbitrary"), vmem_limit_bytes=VMEM_LIMIT),
        name="attn",
    )(h, g, w_q, w_o, bias, k_norm2, limit2, k_t, v)


def _rel_bias(rel_table):
    n_heads = rel_table.shape[0]
    tab = rel_table.astype(F32) * LOG2E
    span = BAND + CHUNK - 1
    n_var = MAX_REL + CHUNK
    base = jnp.concatenate(
        [tab[:, MAX_REL - (CHUNK - 1):2 * MAX_REL + 1],
         jnp.broadcast_to(tab[:, 2 * MAX_REL:], (n_heads, span - n_var))], axis=1)
    b_max = jnp.max(base, axis=1, keepdims=True)
    b_min = jnp.min(base, axis=1, keepdims=True)
    base = base - 0.5 * (b_max + b_min)
    half_range = jnp.max((0.5 * (b_max - b_min)).reshape(n_heads // 2, 2), axis=1, keepdims=True)
    room = SCORE_LIMIT - half_range
    limit2 = jnp.broadcast_to(jnp.where(room > 0, room * room, -1.0), (n_heads // 2, LANES))
    rev = base[:, ::-1]
    padded = jnp.pad(rev, ((0, 0), (0, 1)))
    rows = jnp.tile(padded, (1, CHUNK))[:, :CHUNK * span].reshape(n_heads, CHUNK, span)
    band = rows[:, :, CHUNK - 1:CHUNK - 1 + BAND]
    ninf = jnp.full((n_heads, CHUNK, CHUNK), -jnp.inf, F32)
    bias = jnp.concatenate([jnp.concatenate([band, ninf], axis=2),
                            jnp.concatenate([ninf, band], axis=2)], axis=1)
    return bias, limit2


def _mlstm_params(w_in, b_gate, g_head):
    w_q = w_in[:, 0:A_QK]
    w_k = w_in[:, A_QK:2 * A_QK]
    w_vo = w_in[:, 2 * A_QK:A_GATE_OFF]
    w_fm = jnp.concatenate([w_q, w_vo], axis=1).T.astype(BF16)
    pad = LANES - A_HEADS
    w_g = jnp.concatenate([jnp.pad(w_in[:, A_GATE_OFF:A_GATE_OFF + A_HEADS], ((0, 0), (0, pad))),
                           jnp.pad(w_in[:, A_GATE_OFF + A_HEADS:], ((0, 0), (0, pad)))], axis=1)
    b_g = jnp.concatenate([jnp.pad(b_gate[:A_HEADS], (0, pad)), jnp.pad(b_gate[A_HEADS:], (0, pad))])
    g_head_b = jnp.broadcast_to(g_head.astype(F32)[:, None], (A_V, LANES))
    return w_fm, w_k.astype(BF16), w_g.astype(BF16), b_g.astype(F32).reshape(1, 2 * LANES), g_head_b


def kernel(x, a_w_in, a_b_gate, a_g_head, a_w_out, b_w_q, b_rel_bias, b_w_o, kv_norm_g, w_kv,
           norm_mix_g, norm_ffn_g, ffn_w_gate, ffn_w_up, ffn_w_down, final_norm_g):
    bsz, seq, d = x.shape
    assert d == D_MODEL and seq % ROW_TILE == 0
    depth = norm_mix_g.shape[0]
    n_a = a_w_in.shape[0]

    def row(v):
        return v.reshape(1, -1).astype(F32)

    h = x
    k_t = v_sh = k_norm2 = None
    for l in range(depth):
        if l == n_a:
            k_t, v_sh, k_norm2 = _kv_proj(h, row(kv_norm_g), w_kv[:, :D_MODEL].T.astype(BF16),
                                          w_kv[:, D_MODEL:].astype(BF16))
            k_norm2 = jnp.max(k_norm2, axis=1)
        if l < n_a:
            h = _mlstm_layer(h, row(norm_mix_g[l]), *_mlstm_params(a_w_in[l], a_b_gate[l], a_g_head[l]),
                             a_w_out[l].astype(BF16))
        else:
            j = l - n_a
            bias, limit2 = _rel_bias(b_rel_bias[j])
            h = _attn_layer(h, row(norm_mix_g[l]), b_w_q[j].astype(BF16), b_w_o[j].astype(BF16),
                            bias, k_norm2, limit2, k_t, v_sh)
        h = _ffn(h.reshape(bsz * seq, d), row(norm_ffn_g[l]), ffn_w_gate[l].astype(BF16),
                 ffn_w_up[l].astype(BF16), ffn_w_down[l].astype(BF16), row(final_norm_g),
                 final_norm=(l == depth - 1)).reshape(bsz, seq, d)
    return h
```

```python
import functools

import jax
import jax.numpy as jnp
from jax import lax
from jax.experimental import pallas as pl
from jax.experimental.pallas import tpu as pltpu

F32 = jnp.float32
BF16 = jnp.bfloat16

D_MODEL = 1024
CHUNK = 64
A_HEADS = 8
A_DQK = 64
A_DV = 128
A_QK = A_HEADS * A_DQK
A_V = A_HEADS * A_DV
B_HEADS = 16
B_DH = 64
LEFT_CHUNKS = 8
LEFT = LEFT_CHUNKS * CHUNK
BAND = LEFT + CHUNK
MAX_REL = 256
D_FF = 2816
EPS = 1e-6
LOG2E = 1.4426950408889634
SCORE_LIMIT = 1.0
NEG_INIT = -1e30

LANES = 128
MXU_COLS = 256

ROW_TILE = 512
FFN_ROWS = 1024
A_SUPER = 128
FF_TILE = 256
VMEM_LIMIT = 56 * 1024 * 1024

A_GATE_OFF = 2 * A_QK + 2 * A_V


def _rmsnorm(x, g):
    return x * lax.rsqrt(jnp.mean(x * x, axis=-1, keepdims=True) + EPS) * g


def _dot(a, b):
    return jnp.dot(a, b, preferred_element_type=F32)


def _dot_nt(a, b):
    return lax.dot_general(a, b, (((1,), (1,)), ((), ())), preferred_element_type=F32)


def _dot_tn(a, b):
    return lax.dot_general(a, b, (((0,), (0,)), ((), ())), preferred_element_type=F32)


def _const_spec(shape):
    nd = len(shape)
    return pl.BlockSpec(shape, lambda *_: (0,) * nd, pipeline_mode=pl.Buffered(1))


def _ffn_kernel(x_ref, g_ref, wg_ref, wu_ref, wd_ref, gf_ref, o_ref, *, final_norm):
    x = x_ref[...]
    xn = _rmsnorm(x, g_ref[...]).astype(BF16)
    o_ref[...] = x
    for j in range(D_FF // FF_TILE):
        cols = slice(j * FF_TILE, (j + 1) * FF_TILE)
        hg = _dot(xn, wg_ref[:, cols])
        hu = _dot(xn, wu_ref[:, cols])
        act = (hg * jax.nn.sigmoid(hg)) * hu
        o_ref[...] += _dot(act.astype(BF16), wd_ref[cols, :])
    if final_norm:
        o_ref[...] = _rmsnorm(o_ref[...], gf_ref[...])


def _ffn(h2d, g, wg, wu, wd, gf, final_norm):
    n_tok = h2d.shape[0]
    row = pl.BlockSpec((FFN_ROWS, D_MODEL), lambda i: (i, 0))
    return pl.pallas_call(
        functools.partial(_ffn_kernel, final_norm=final_norm),
        grid=(n_tok // FFN_ROWS,),
        in_specs=[row, _const_spec((1, D_MODEL)), _const_spec((D_MODEL, D_FF)),
                  _const_spec((D_MODEL, D_FF)), _const_spec((D_FF, D_MODEL)),
                  _const_spec((1, D_MODEL))],
        out_specs=row,
        out_shape=jax.ShapeDtypeStruct(h2d.shape, F32),
        compiler_params=pltpu.CompilerParams(
            dimension_semantics=("arbitrary",), vmem_limit_bytes=VMEM_LIMIT),
        name="ffn",
    )(h2d, g, wg, wu, wd, gf)


A_CEXT = A_DV + 16
A_STEPS = ROW_TILE // A_SUPER


def _log_sigmoid(x):
    return jnp.minimum(x, 0.0) - jnp.log1p(jnp.exp(-jnp.abs(x)))


def _mlstm_kernel(x_ref, g_ref, wfm_ref, wk_ref, wg_ref, bg_ref, ghb_ref, wout_ref, o_ref,
                  qt_s, k_s, vt_s, ogt_s, hst_s, imb_s, bt_s, imbt_s, c_s, m_s):
    t = pl.program_id(1)

    @pl.when(t == 0)
    def _():
        c_s[...] = jnp.zeros_like(c_s)
        m_s[...] = jnp.full_like(m_s, NEG_INIT)

    x = x_ref[0]
    xn = _rmsnorm(x, g_ref[...]).astype(BF16)
    qt_s[...] = _dot_nt(wfm_ref[0:A_QK, :], xn).astype(BF16)
    vt_s[...] = _dot_nt(wfm_ref[A_QK:A_QK + A_V, :], xn).astype(BF16)
    ogt_s[...] = jax.nn.sigmoid(_dot_nt(wfm_ref[A_QK + A_V:A_QK + 2 * A_V, :], xn))
    k_s[...] = (_dot(xn, wk_ref[...]) * (A_DQK ** -0.5)).astype(BF16)
    gates = _dot(xn, wg_ref[...]) + bg_ref[...]
    i_log = gates[:, 0:LANES]
    f_log = _log_sigmoid(gates[:, LANES:2 * LANES])

    rows = lax.broadcasted_iota(jnp.int32, (ROW_TILE, LANES), 0) & (A_SUPER - 1)
    b_cum = f_log
    shift = 1
    while shift < A_SUPER:
        rolled = pltpu.roll(b_cum, shift, axis=0)
        b_cum = b_cum + jnp.where(rows >= shift, rolled, 0.0)
        shift *= 2
    imb = i_log - b_cum
    imb_s[...] = imb
    bt_s[...] = b_cum.T[0:A_HEADS]
    imbt_s[...] = imb.T[0:A_HEADS]

    row_i = lax.broadcasted_iota(jnp.int32, (A_SUPER, A_SUPER), 0)
    col_i = lax.broadcasted_iota(jnp.int32, (A_SUPER, A_SUPER), 1)
    causal = col_i >= row_i
    first_rows = row_i < A_DQK
    first_lanes = col_i < A_DQK
    ext_row = lax.broadcasted_iota(jnp.int32, (A_CEXT - A_DV, LANES), 0) == 0
    lane_row = lax.broadcasted_iota(jnp.int32, (1, LANES), 1) < A_DQK

    def pair_queries(p, r):
        qp = qt_s[p * LANES:(p + 1) * LANES, r]
        zero = jnp.zeros_like(qp)
        return jnp.concatenate([jnp.where(first_rows, qp, zero), jnp.where(first_rows, zero, qp)], axis=1)

    def pair_step(c, p):
        r = slice(c * A_SUPER, (c + 1) * A_SUPER)
        q_pair = pair_queries(p, r)
        kp = k_s[r, p * LANES:(p + 1) * LANES]
        st_pair = _dot(kp, q_pair)
        qc_pair = _dot(c_s[p].astype(BF16), q_pair)
        lhs, scale = [], []
        for half in range(2):
            h = 2 * p + half
            hr = slice(h * A_DV, (h + 1) * A_DV)
            vt_h = vt_s[hr, r]
            b_row = bt_s[h:h + 1, r]
            imb_row = imbt_s[h:h + 1, r]
            imb_col = imb_s[r, h:h + 1]
            b_last = jnp.broadcast_to(b_row[:, A_SUPER - 1:A_SUPER], (1, LANES))
            m_prev = m_s[h]

            log_dt = jnp.where(causal, b_row + imb_col, -jnp.inf)
            mrow = jnp.max(log_dt, axis=0, keepdims=True)
            st = st_pair[:, half * A_SUPER:(half + 1) * A_SUPER] * jnp.exp(log_dt - mrow)
            svt = _dot(vt_h, st.astype(BF16))
            rs = jnp.sum(st, axis=0, keepdims=True)

            inter = b_row + m_prev
            m_j = jnp.maximum(inter, mrow)
            f_intra = jnp.exp(mrow - m_j)
            w_inter = jnp.exp(inter - m_j)
            qc = qc_pair[:, half * A_SUPER:(half + 1) * A_SUPER]
            num = f_intra * svt + w_inter * qc[0:A_DV]
            den = f_intra * rs + w_inter * qc[A_DV:A_DV + 1]
            ht = num * (1.0 / jnp.maximum(jnp.abs(den), jnp.exp(-m_j)))
            hn = ht * lax.rsqrt(jnp.mean(ht * ht, axis=0, keepdims=True) + EPS)
            hst_s[hr, r] = (ogt_s[hr, r] * (hn * ghb_ref[hr, :])).astype(BF16)

            a_row = b_last + imb_row
            amax = jnp.broadcast_to(jnp.max(a_row, axis=1, keepdims=True), (1, LANES))
            wa = jnp.exp(a_row - amax)
            vw = (vt_h.astype(F32) * wa).astype(BF16)
            ext = jnp.where(ext_row, jnp.broadcast_to(wa, (A_CEXT - A_DV, LANES)), 0.0).astype(BF16)
            lhs.append(jnp.concatenate([vw, ext], axis=0))

            m_new = jnp.maximum(b_last + m_prev, amax)
            scale.append((jnp.exp(b_last + m_prev - m_new), jnp.exp(amax - m_new)))
            m_s[h] = m_new
        k_zero = jnp.zeros_like(kp)
        k_split = jnp.concatenate([jnp.where(first_lanes, kp, k_zero),
                                   jnp.where(first_lanes, k_zero, kp)], axis=0)
        up = _dot(jnp.concatenate(lhs, axis=1), k_split)
        decay = jnp.where(lane_row, scale[0][0], scale[1][0])
        grow = jnp.where(lane_row, scale[0][1], scale[1][1])
        c_s[p] = decay * c_s[p] + grow * up

    @pl.when(t >= 0)
    def _():
        for c in range(A_STEPS):
            for p in range(A_HEADS // 2):
                pair_step(c, p)

    o_ref[0] = x + _dot_tn(hst_s[...], wout_ref[...])


def _mlstm_layer(h, g, w_fm, w_k, w_g, b_g, g_head_b, w_out):
    bsz, seq, _ = h.shape
    row = pl.BlockSpec((1, ROW_TILE, D_MODEL), lambda b, t: (b, t, 0))
    return pl.pallas_call(
        _mlstm_kernel,
        grid=(bsz, seq // ROW_TILE),
        in_specs=[row, _const_spec((1, D_MODEL)), _const_spec((A_QK + 2 * A_V, D_MODEL)),
                  _const_spec((D_MODEL, A_QK)), _const_spec((D_MODEL, 2 * LANES)),
                  _const_spec((1, 2 * LANES)), _const_spec((A_V, LANES)), _const_spec((A_V, D_MODEL))],
        out_specs=row,
        out_shape=jax.ShapeDtypeStruct(h.shape, F32),
        scratch_shapes=[
            pltpu.VMEM((A_QK, ROW_TILE), BF16),
            pltpu.VMEM((ROW_TILE, A_QK), BF16),
            pltpu.VMEM((A_V, ROW_TILE), BF16),
            pltpu.VMEM((A_V, ROW_TILE), F32),
            pltpu.VMEM((A_V, ROW_TILE), BF16),
            pltpu.VMEM((ROW_TILE, LANES), F32),
            pltpu.VMEM((A_HEADS, ROW_TILE), F32),
            pltpu.VMEM((A_HEADS, ROW_TILE), F32),
            pltpu.VMEM((A_HEADS // 2, A_CEXT, LANES), F32),
            pltpu.VMEM((A_HEADS, 1, LANES), F32),
        ],
        compiler_params=pltpu.CompilerParams(
            dimension_semantics=("arbitrary", "arbitrary"), vmem_limit_bytes=VMEM_LIMIT),
        name="mlstm",
    )(h, g, w_fm, w_k, w_g, b_g, g_head_b, w_out)


def _kv_kernel(x_ref, g_ref, wkt_ref, wv_ref, kt_ref, v_ref, kn_ref):
    xn = _rmsnorm(x_ref[0], g_ref[...]).astype(BF16)
    kt = _dot_nt(wkt_ref[...], xn).astype(BF16)
    kt_ref[0] = kt
    v_ref[0] = _dot(xn, wv_ref[...]).astype(BF16)
    kf = kt.astype(F32)
    sq = kf * kf
    for p in range(B_HEADS // 2):
        norm2 = jnp.sum(sq[p * LANES:(p + 1) * LANES], axis=0, keepdims=True)
        kn_ref[0, 0, p:p + 1, :] = jnp.broadcast_to(jnp.max(norm2, axis=1, keepdims=True), (1, LANES))


def _kv_proj(h, g, w_kt, w_v):
    bsz, seq, _ = h.shape
    row = pl.BlockSpec((1, ROW_TILE, D_MODEL), lambda b, t: (b, t, 0))
    col = pl.BlockSpec((1, D_MODEL, ROW_TILE), lambda b, t: (b, 0, t))
    return pl.pallas_call(
        _kv_kernel,
        grid=(bsz, seq // ROW_TILE),
        in_specs=[row, _const_spec((1, D_MODEL)), _const_spec((D_MODEL, D_MODEL)),
                  _const_spec((D_MODEL, D_MODEL))],
        out_specs=[col, row, pl.BlockSpec((1, 1, B_HEADS // 2, LANES), lambda b, t: (b, t, 0, 0))],
        out_shape=[jax.ShapeDtypeStruct((bsz, D_MODEL, seq), BF16),
                   jax.ShapeDtypeStruct((bsz, seq, D_MODEL), BF16),
                   jax.ShapeDtypeStruct((bsz, seq // ROW_TILE, B_HEADS // 2, LANES), F32)],
        compiler_params=pltpu.CompilerParams(
            dimension_semantics=("arbitrary", "arbitrary"), vmem_limit_bytes=VMEM_LIMIT),
        name="kv_proj",
    )(h, g, w_kt, w_v)


Q_BLOCK = 2 * CHUNK
K_BLOCK = LEFT + Q_BLOCK


def _attn_kernel(x_ref, g_ref, wq_ref, wo_ref, bias_ref, kn_ref, lim_ref, kt_ref, v_ref, o_ref,
                 q_s, a_s, s_s, p_s, m_s, l_s):
    t = pl.program_id(1)
    x = x_ref[0]
    xn = _rmsnorm(x, g_ref[...]).astype(BF16)
    q = _dot(xn, wq_ref[...]) * (B_DH ** -0.5 * LOG2E)
    q_s[...] = q.astype(BF16)

    excess = None
    for p in range(B_HEADS // 2):
        qp = q[:, p * LANES:(p + 1) * LANES]
        q_norm2 = jnp.max(jnp.sum(qp * qp, axis=1, keepdims=True))
        e = q_norm2 * kn_ref[0, p:p + 1, :] - lim_ref[p:p + 1, :]
        excess = e if excess is None else jnp.maximum(excess, e)
    bounded = jnp.max(excess) <= 0.0

    lane = lax.broadcasted_iota(jnp.int32, (Q_BLOCK, LANES), 1)
    lo = lane < B_DH
    n_tiles = K_BLOCK // LANES
    n_blocks = ROW_TILE // Q_BLOCK
    n_pairs = B_HEADS // 2
    left_tiles = LEFT // LANES

    def band(i, first_tile):
        if first_tile:
            first = left_tiles - i
            keys = pl.ds(0, (n_tiles - first) * LANES)
        else:
            first = 0
            keys = pl.ds(pl.multiple_of((t - 1) * ROW_TILE, ROW_TILE) + i * Q_BLOCK, K_BLOCK)
        return first, keys, slice(first * LANES, K_BLOCK)

    def scores(i, p, first_tile):
        pl_ = slice(p * LANES, (p + 1) * LANES)
        _, keys, cols = band(i, first_tile)
        qp = q_s[i * Q_BLOCK:(i + 1) * Q_BLOCK, pl_]
        kp = kt_ref[0, pl_, keys]
        for half in range(2):
            h = 2 * p + half
            own = lo if half == 0 else jnp.logical_not(lo)
            s = _dot(jnp.where(own, qp, jnp.zeros_like(qp)), kp) + bias_ref[h, :, cols]
            s_s[h, :, cols] = s
            m_s[h] = jnp.broadcast_to(jnp.max(s, axis=1, keepdims=True), (Q_BLOCK, LANES))

    def scores_bounded(i, p, first_tile):
        pl_ = slice(p * LANES, (p + 1) * LANES)
        first, keys, cols = band(i, first_tile)
        qp = q_s[i * Q_BLOCK:(i + 1) * Q_BLOCK, pl_]
        kp = kt_ref[0, pl_, keys]
        for half in range(2):
            h = 2 * p + half
            own = lo if half == 0 else jnp.logical_not(lo)
            s = _dot(jnp.where(own, qp, jnp.zeros_like(qp)), kp) + bias_ref[h, :, cols]
            acc = jnp.zeros((Q_BLOCK, LANES), F32)
            for j in range(n_tiles - first):
                e = jnp.exp2(s[:, j * LANES:(j + 1) * LANES])
                acc = acc + e
                p_s[h, :, (first + j) * LANES:(first + j + 1) * LANES] = e.astype(BF16)
            l_s[h] = jnp.broadcast_to(jnp.sum(acc, axis=1, keepdims=True), (Q_BLOCK, LANES))

    def exps(i, p, first_tile):
        first, _, _ = band(i, first_tile)
        for h in (2 * p, 2 * p + 1):
            m = m_s[h]
            acc = jnp.zeros((Q_BLOCK, LANES), F32)
            for j in range(first, n_tiles):
                tl = slice(j * LANES, (j + 1) * LANES)
                e = jnp.exp2(s_s[h, :, tl] - m)
                acc = acc + e
                p_s[h, :, tl] = e.astype(BF16)
            l_s[h] = jnp.broadcast_to(jnp.sum(acc, axis=1, keepdims=True), (Q_BLOCK, LANES))

    def values(i, p, first_tile):
        pl_ = slice(p * LANES, (p + 1) * LANES)
        _, keys, cols = band(i, first_tile)
        vp = v_ref[0, keys, pl_]
        o0 = _dot(p_s[2 * p, :, cols], vp) / l_s[2 * p]
        o1 = _dot(p_s[2 * p + 1, :, cols], vp) / l_s[2 * p + 1]
        a_s[i * Q_BLOCK:(i + 1) * Q_BLOCK, pl_] = jnp.where(lo, o0, o1).astype(BF16)

    def block(i, first_tile, stages):
        for stage in stages:
            for p in range(n_pairs):
                stage(i, p, first_tile)

    first = t == 0
    for i in range(n_blocks):
        for stages, use in (((scores_bounded, values), bounded),
                            ((scores, exps, values), jnp.logical_not(bounded))):
            pl.when(jnp.logical_and(use, first))(functools.partial(block, i, True, stages))
            pl.when(jnp.logical_and(use, jnp.logical_not(first)))(functools.partial(block, i, False, stages))

    o_ref[0] = x + _dot(a_s[...], wo_ref[...])


def _attn_layer(h, g, w_q, w_o, bias, k_norm2, limit2, k_t, v):
    bsz, seq, _ = h.shape
    assert LEFT == ROW_TILE
    row = pl.BlockSpec((1, ROW_TILE, D_MODEL), lambda b, t: (b, t, 0))
    kt_all = pl.BlockSpec((1, D_MODEL, seq), lambda b, t: (b, 0, 0))
    v_all = pl.BlockSpec((1, seq, D_MODEL), lambda b, t: (b, 0, 0))
    return pl.pallas_call(
        _attn_kernel,
        grid=(bsz, seq // ROW_TILE),
        in_specs=[row, _const_spec((1, D_MODEL)), _const_spec((D_MODEL, D_MODEL)),
                  _const_spec((D_MODEL, D_MODEL)), _const_spec((B_HEADS, Q_BLOCK, K_BLOCK)),
                  pl.BlockSpec((1, B_HEADS // 2, LANES), lambda b, t: (b, 0, 0)),
                  _const_spec((B_HEADS // 2, LANES)), kt_all, v_all],
        out_specs=row,
        out_shape=jax.ShapeDtypeStruct(h.shape, F32),
        scratch_shapes=[pltpu.VMEM((ROW_TILE, D_MODEL), BF16),
                        pltpu.VMEM((ROW_TILE, D_MODEL), BF16),
                        pltpu.VMEM((B_HEADS, Q_BLOCK, K_BLOCK), F32),
                        pltpu.VMEM((B_HEADS, Q_BLOCK, K_BLOCK), BF16),
                        pltpu.VMEM((B_HEADS, Q_BLOCK, LANES), F32),
                        pltpu.VMEM((B_HEADS, Q_BLOCK, LANES), F32)],
        compiler_params=pltpu.CompilerParams(
            dimension_semantics=("arbitrary", "arbitrary"), vmem_limit_bytes=VMEM_LIMIT),
        name="attn",
    )(h, g, w_q, w_o, bias, k_norm2, limit2, k_t, v)


def _rel_bias(rel_table):
    n_heads = rel_table.shape[0]
    tab = rel_table.astype(F32) * LOG2E
    span = BAND + CHUNK - 1
    n_var = MAX_REL + CHUNK
    base = jnp.concatenate(
        [tab[:, MAX_REL - (CHUNK - 1):2 * MAX_REL + 1],
         jnp.broadcast_to(tab[:, 2 * MAX_REL:], (n_heads, span - n_var))], axis=1)
    b_max = jnp.max(base, axis=1, keepdims=True)
    b_min = jnp.min(base, axis=1, keepdims=True)
    base = base - 0.5 * (b_max + b_min)
    half_range = jnp.max((0.5 * (b_max - b_min)).reshape(n_heads // 2, 2), axis=1, keepdims=True)
    room = SCORE_LIMIT - half_range
    limit2 = jnp.broadcast_to(jnp.where(room > 0, room * room, -1.0), (n_heads // 2, LANES))
    rev = base[:, ::-1]
    padded = jnp.pad(rev, ((0, 0), (0, 1)))
    rows = jnp.tile(padded, (1, CHUNK))[:, :CHUNK * span].reshape(n_heads, CHUNK, span)
    band = rows[:, :, CHUNK - 1:CHUNK - 1 + BAND]
    ninf = jnp.full((n_heads, CHUNK, CHUNK), -jnp.inf, F32)
    bias = jnp.concatenate([jnp.concatenate([band, ninf], axis=2),
                            jnp.concatenate([ninf, band], axis=2)], axis=1)
    return bias, limit2


def _mlstm_params(w_in, b_gate, g_head):
    w_q = w_in[:, 0:A_QK]
    w_k = w_in[:, A_QK:2 * A_QK]
    w_vo = w_in[:, 2 * A_QK:A_GATE_OFF]
    w_fm = jnp.concatenate([w_q, w_vo], axis=1).T.astype(BF16)
    pad = LANES - A_HEADS
    w_g = jnp.concatenate([jnp.pad(w_in[:, A_GATE_OFF:A_GATE_OFF + A_HEADS], ((0, 0), (0, pad))),
                           jnp.pad(w_in[:, A_GATE_OFF + A_HEADS:], ((0, 0), (0, pad)))], axis=1)
    b_g = jnp.concatenate([jnp.pad(b_gate[:A_HEADS], (0, pad)), jnp.pad(b_gate[A_HEADS:], (0, pad))])
    g_head_b = jnp.broadcast_to(g_head.astype(F32)[:, None], (A_V, LANES))
    return w_fm, w_k.astype(BF16), w_g.astype(BF16), b_g.astype(F32).reshape(1, 2 * LANES), g_head_b


def kernel(x, a_w_in, a_b_gate, a_g_head, a_w_out, b_w_q, b_rel_bias, b_w_o, kv_norm_g, w_kv,
           norm_mix_g, norm_ffn_g, ffn_w_gate, ffn_w_up, ffn_w_down, final_norm_g):
    bsz, seq, d = x.shape
    assert d == D_MODEL and seq % ROW_TILE == 0
    depth = norm_mix_g.shape[0]
    n_a = a_w_in.shape[0]

    def row(v):
        return v.reshape(1, -1).astype(F32)

    h = x
    k_t = v_sh = k_norm2 = None
    for l in range(depth):
        if l == n_a:
            k_t, v_sh, k_norm2 = _kv_proj(h, row(kv_norm_g), w_kv[:, :D_MODEL].T.astype(BF16),
                                          w_kv[:, D_MODEL:].astype(BF16))
            k_norm2 = jnp.max(k_norm2, axis=1)
        if l < n_a:
            h = _mlstm_layer(h, row(norm_mix_g[l]), *_mlstm_params(a_w_in[l], a_b_gate[l], a_g_head[l]),
                             a_w_out[l].astype(BF16))
        else:
            j = l - n_a
            bias, limit2 = _rel_bias(b_rel_bias[j])
            h = _attn_layer(h, row(norm_mix_g[l]), b_w_q[j].astype(BF16), b_w_o[j].astype(BF16),
                            bias, k_norm2, limit2, k_t, v_sh)
        h = _ffn(h.reshape(bsz * seq, d), row(norm_ffn_g[l]), ffn_w_gate[l].astype(BF16),
                 ffn_w_up[l].astype(BF16), ffn_w_down[l].astype(BF16), row(final_norm_g),
                 final_norm=(l == depth - 1)).reshape(bsz, seq, d)
    return h
```

```python
import functools

import jax
import jax.numpy as jnp
from jax import lax
from jax.experimental import pallas as pl
from jax.experimental.pallas import tpu as pltpu

F32 = jnp.float32
BF16 = jnp.bfloat16

D_MODEL = 1024
CHUNK = 64
A_HEADS = 8
A_DQK = 64
A_DV = 128
A_QK = A_HEADS * A_DQK
A_V = A_HEADS * A_DV
B_HEADS = 16
B_DH = 64
LEFT_CHUNKS = 8
LEFT = LEFT_CHUNKS * CHUNK
BAND = LEFT + CHUNK
MAX_REL = 256
D_FF = 2816
EPS = 1e-6
LOG2E = 1.4426950408889634
SCORE_LIMIT = 96.0
NEG_INIT = -1e30

LANES = 128
MXU_COLS = 256

ROW_TILE = 512
FFN_ROWS = 1024
A_SUPER = 128
FF_TILE = 256
VMEM_LIMIT = 56 * 1024 * 1024

A_GATE_OFF = 2 * A_QK + 2 * A_V


def _rmsnorm(x, g):
    return x * lax.rsqrt(jnp.mean(x * x, axis=-1, keepdims=True) + EPS) * g


def _dot(a, b):
    return jnp.dot(a, b, preferred_element_type=F32)


def _dot_nt(a, b):
    return lax.dot_general(a, b, (((1,), (1,)), ((), ())), preferred_element_type=F32)


def _dot_tn(a, b):
    return lax.dot_general(a, b, (((0,), (0,)), ((), ())), preferred_element_type=F32)


def _const_spec(shape):
    nd = len(shape)
    return pl.BlockSpec(shape, lambda *_: (0,) * nd, pipeline_mode=pl.Buffered(1))


def _ffn_kernel(x_ref, g_ref, wg_ref, wu_ref, wd_ref, gf_ref, o_ref, *, final_norm):
    x = x_ref[...]
    xn = _rmsnorm(x, g_ref[...]).astype(BF16)
    o_ref[...] = x
    for j in range(D_FF // FF_TILE):
        cols = slice(j * FF_TILE, (j + 1) * FF_TILE)
        hg = _dot(xn, wg_ref[:, cols])
        hu = _dot(xn, wu_ref[:, cols])
        act = (hg * jax.nn.sigmoid(hg)) * hu
        o_ref[...] += _dot(act.astype(BF16), wd_ref[cols, :])
    if final_norm:
        o_ref[...] = _rmsnorm(o_ref[...], gf_ref[...])


def _ffn(h2d, g, wg, wu, wd, gf, final_norm):
    n_tok = h2d.shape[0]
    row = pl.BlockSpec((FFN_ROWS, D_MODEL), lambda i: (i, 0))
    return pl.pallas_call(
        functools.partial(_ffn_kernel, final_norm=final_norm),
        grid=(n_tok // FFN_ROWS,),
        in_specs=[row, _const_spec((1, D_MODEL)), _const_spec((D_MODEL, D_FF)),
                  _const_spec((D_MODEL, D_FF)), _const_spec((D_FF, D_MODEL)),
                  _const_spec((1, D_MODEL))],
        out_specs=row,
        out_shape=jax.ShapeDtypeStruct(h2d.shape, F32),
        compiler_params=pltpu.CompilerParams(
            dimension_semantics=("arbitrary",), vmem_limit_bytes=VMEM_LIMIT),
        name="ffn",
    )(h2d, g, wg, wu, wd, gf)


A_CEXT = A_DV + 16
A_STEPS = ROW_TILE // A_SUPER


def _log_sigmoid(x):
    return jnp.minimum(x, 0.0) - jnp.log1p(jnp.exp(-jnp.abs(x)))


def _mlstm_kernel(x_ref, g_ref, wfm_ref, wk_ref, bg_ref, ghb_ref, wout_ref, o_ref,
                  qt_s, k_s, vt_s, ogt_s, hst_s, imb_s, bt_s, imbt_s, cmxt_s, c_s, m_s):
    t = pl.program_id(1)

    @pl.when(t == 0)
    def _():
        c_s[...] = jnp.zeros_like(c_s)
        m_s[...] = jnp.full_like(m_s, NEG_INIT)

    x = x_ref[0]
    xn = _rmsnorm(x, g_ref[...]).astype(BF16)
    gates = _dot_nt(wfm_ref[A_QK + 2 * A_V:A_QK + 2 * A_V + 2 * A_HEADS, :], xn) + bg_ref[...]
    i_log = gates[0:A_HEADS]
    f_log = _log_sigmoid(gates[A_HEADS:2 * A_HEADS])

    pos = lax.broadcasted_iota(jnp.int32, (A_HEADS, ROW_TILE), 1) & (A_SUPER - 1)
    b_cum = f_log
    shift = 1
    while shift < A_SUPER:
        b_cum = b_cum + jnp.where(pos >= shift, pltpu.roll(b_cum, shift, axis=1), 0.0)
        shift *= 2
    imb = i_log - b_cum
    cmx = imb
    shift = 1
    while shift < A_SUPER:
        cmx = jnp.where(pos >= shift, jnp.maximum(cmx, pltpu.roll(cmx, shift, axis=1)), cmx)
        shift *= 2
    bt_s[...] = b_cum
    imbt_s[...] = imb
    cmxt_s[...] = cmx
    pad = jnp.zeros((LANES - A_HEADS, ROW_TILE), F32)
    imb_s[...] = jnp.concatenate([imb * LOG2E, pad], axis=0).T

    qt_s[...] = _dot_nt(wfm_ref[0:A_QK, :], xn).astype(BF16)
    vt_s[...] = _dot_nt(wfm_ref[A_QK:A_QK + A_V, :], xn).astype(BF16)
    ogt_s[...] = jax.nn.sigmoid(_dot_nt(wfm_ref[A_QK + A_V:A_QK + 2 * A_V, :], xn))
    k_s[...] = (_dot(xn, wk_ref[...]) * (A_DQK ** -0.5)).astype(BF16)

    row_i = lax.broadcasted_iota(jnp.int32, (A_SUPER, A_SUPER), 0)
    col_i = lax.broadcasted_iota(jnp.int32, (A_SUPER, A_SUPER), 1)
    causal = col_i >= row_i
    first_rows = row_i < A_DQK
    first_lanes = col_i < A_DQK
    ext_row = lax.broadcasted_iota(jnp.int32, (A_CEXT - A_DV, LANES), 0) == 0
    lane_row = lax.broadcasted_iota(jnp.int32, (1, LANES), 1) < A_DQK

    def pair_queries(p, r):
        qp = qt_s[p * LANES:(p + 1) * LANES, r]
        zero = jnp.zeros_like(qp)
        return jnp.concatenate([jnp.where(first_rows, qp, zero), jnp.where(first_rows, zero, qp)], axis=1)

    def pair_step(c, p):
        r = slice(c * A_SUPER, (c + 1) * A_SUPER)
        q_pair = pair_queries(p, r)
        kp = k_s[r, p * LANES:(p + 1) * LANES]
        st_pair = _dot(kp, q_pair)
        qc_pair = _dot(c_s[p].astype(BF16), q_pair)
        lhs, scale = [], []
        for half in range(2):
            h = 2 * p + half
            hr = slice(h * A_DV, (h + 1) * A_DV)
            vt_h = vt_s[hr, r]
            b_row = bt_s[h:h + 1, r]
            imb_row = imbt_s[h:h + 1, r]
            imb2_col = imb_s[r, h:h + 1]
            cmx_row = cmxt_s[h:h + 1, r]
            b_last = jnp.broadcast_to(b_row[:, A_SUPER - 1:A_SUPER], (1, LANES))
            m_prev = m_s[h]

            mrow = b_row + cmx_row
            d_rel = jnp.exp2(jnp.where(causal, imb2_col - cmx_row * LOG2E, -jnp.inf))
            st = st_pair[:, half * A_SUPER:(half + 1) * A_SUPER] * d_rel
            svt = _dot(vt_h, st.astype(BF16))
            rs = jnp.sum(st, axis=0, keepdims=True)

            inter = b_row + m_prev
            m_j = jnp.maximum(inter, mrow)
            f_intra = jnp.exp(mrow - m_j)
            w_inter = jnp.exp(inter - m_j)
            qc = qc_pair[:, half * A_SUPER:(half + 1) * A_SUPER]
            num = f_intra * svt + w_inter * qc[0:A_DV]
            den = f_intra * rs + w_inter * qc[A_DV:A_DV + 1]
            ht = num * (1.0 / jnp.maximum(jnp.abs(den), jnp.exp(-m_j)))
            hn = ht * lax.rsqrt(jnp.mean(ht * ht, axis=0, keepdims=True) + EPS)
            hst_s[hr, r] = (ogt_s[hr, r] * (hn * ghb_ref[hr, :])).astype(BF16)

            a_row = b_last + imb_row
            amax = jnp.broadcast_to(jnp.max(a_row, axis=1, keepdims=True), (1, LANES))
            wa = jnp.exp(a_row - amax)
            vw = (vt_h.astype(F32) * wa).astype(BF16)
            ext = jnp.where(ext_row, jnp.broadcast_to(wa, (A_CEXT - A_DV, LANES)), 0.0).astype(BF16)
            lhs.append(jnp.concatenate([vw, ext], axis=0))

            m_new = jnp.maximum(b_last + m_prev, amax)
            scale.append((jnp.exp(b_last + m_prev - m_new), jnp.exp(amax - m_new)))
            m_s[h] = m_new
        k_zero = jnp.zeros_like(kp)
        k_split = jnp.concatenate([jnp.where(first_lanes, kp, k_zero),
                                   jnp.where(first_lanes, k_zero, kp)], axis=0)
        up = _dot(jnp.concatenate(lhs, axis=1), k_split)
        decay = jnp.where(lane_row, scale[0][0], scale[1][0])
        grow = jnp.where(lane_row, scale[0][1], scale[1][1])
        c_s[p] = decay * c_s[p] + grow * up

    @pl.when(t >= 0)
    def _():
        for c in range(A_STEPS):
            for p in range(A_HEADS // 2):
                pair_step(c, p)

    o_ref[0] = x + _dot_tn(hst_s[...], wout_ref[...])


def _mlstm_layer(h, g, w_fm, w_k, b_g, g_head_b, w_out):
    bsz, seq, _ = h.shape
    row = pl.BlockSpec((1, ROW_TILE, D_MODEL), lambda b, t: (b, t, 0))
    return pl.pallas_call(
        _mlstm_kernel,
        grid=(bsz, seq // ROW_TILE),
        in_specs=[row, _const_spec((1, D_MODEL)), _const_spec((A_QK + 2 * A_V + 2 * A_HEADS, D_MODEL)),
                  _const_spec((D_MODEL, A_QK)), _const_spec((2 * A_HEADS, ROW_TILE)),
                  _const_spec((A_V, LANES)), _const_spec((A_V, D_MODEL))],
        out_specs=row,
        out_shape=jax.ShapeDtypeStruct(h.shape, F32),
        scratch_shapes=[
            pltpu.VMEM((A_QK, ROW_TILE), BF16),
            pltpu.VMEM((ROW_TILE, A_QK), BF16),
            pltpu.VMEM((A_V, ROW_TILE), BF16),
            pltpu.VMEM((A_V, ROW_TILE), F32),
            pltpu.VMEM((A_V, ROW_TILE), BF16),
            pltpu.VMEM((ROW_TILE, LANES), F32),
            pltpu.VMEM((A_HEADS, ROW_TILE), F32),
            pltpu.VMEM((A_HEADS, ROW_TILE), F32),
            pltpu.VMEM((A_HEADS, ROW_TILE), F32),
            pltpu.VMEM((A_HEADS // 2, A_CEXT, LANES), F32),
            pltpu.VMEM((A_HEADS, 1, LANES), F32),
        ],
        compiler_params=pltpu.CompilerParams(
            dimension_semantics=("arbitrary", "arbitrary"), vmem_limit_bytes=VMEM_LIMIT),
        name="mlstm",
    )(h, g, w_fm, w_k, b_g, g_head_b, w_out)


def _kv_kernel(x_ref, g_ref, wkt_ref, wv_ref, kt_ref, v_ref, kn_ref):
    xn = _rmsnorm(x_ref[0], g_ref[...]).astype(BF16)
    kt = _dot_nt(wkt_ref[...], xn).astype(BF16)
    kt_ref[0] = kt
    v_ref[0] = _dot(xn, wv_ref[...]).astype(BF16)
    kf = kt.astype(F32)
    sq = kf * kf
    for p in range(B_HEADS // 2):
        norm2 = jnp.sum(sq[p * LANES:(p + 1) * LANES], axis=0, keepdims=True)
        kn_ref[0, 0, p:p + 1, :] = jnp.broadcast_to(jnp.max(norm2, axis=1, keepdims=True), (1, LANES))


def _kv_proj(h, g, w_kt, w_v):
    bsz, seq, _ = h.shape
    row = pl.BlockSpec((1, ROW_TILE, D_MODEL), lambda b, t: (b, t, 0))
    col = pl.BlockSpec((1, D_MODEL, ROW_TILE), lambda b, t: (b, 0, t))
    return pl.pallas_call(
        _kv_kernel,
        grid=(bsz, seq // ROW_TILE),
        in_specs=[row, _const_spec((1, D_MODEL)), _const_spec((D_MODEL, D_MODEL)),
                  _const_spec((D_MODEL, D_MODEL))],
        out_specs=[col, row, pl.BlockSpec((1, 1, B_HEADS // 2, LANES), lambda b, t: (b, t, 0, 0))],
        out_shape=[jax.ShapeDtypeStruct((bsz, D_MODEL, seq), BF16),
                   jax.ShapeDtypeStruct((bsz, seq, D_MODEL), BF16),
                   jax.ShapeDtypeStruct((bsz, seq // ROW_TILE, B_HEADS // 2, LANES), F32)],
        compiler_params=pltpu.CompilerParams(
            dimension_semantics=("arbitrary", "arbitrary"), vmem_limit_bytes=VMEM_LIMIT),
        name="kv_proj",
    )(h, g, w_kt, w_v)


Q_BLOCK = 2 * CHUNK
K_BLOCK = LEFT + Q_BLOCK


def _attn_kernel(x_ref, g_ref, wq_ref, wo_ref, bias_ref, kn_ref, lim_ref, kt_ref, v_ref, o_ref,
                 q_s, a_s, s_s, p_s, m_s, l_s):
    t = pl.program_id(1)
    x = x_ref[0]
    xn = _rmsnorm(x, g_ref[...]).astype(BF16)
    q = _dot(xn, wq_ref[...]) * (B_DH ** -0.5 * LOG2E)
    q_s[...] = q.astype(BF16)

    excess = None
    for p in range(B_HEADS // 2):
        qp = q[:, p * LANES:(p + 1) * LANES]
        q_norm2 = jnp.max(jnp.sum(qp * qp, axis=1, keepdims=True))
        e = q_norm2 * kn_ref[0, p:p + 1, :] - lim_ref[p:p + 1, :]
        excess = e if excess is None else jnp.maximum(excess, e)
    bounded = jnp.max(excess) <= 0.0

    lane = lax.broadcasted_iota(jnp.int32, (Q_BLOCK, LANES), 1)
    lo = lane < B_DH
    n_tiles = K_BLOCK // LANES
    n_blocks = ROW_TILE // Q_BLOCK
    n_pairs = B_HEADS // 2
    left_tiles = LEFT // LANES

    def band(i, first_tile):
        if first_tile:
            first = left_tiles - i
            keys = pl.ds(0, (n_tiles - first) * LANES)
        else:
            first = 0
            keys = pl.ds(pl.multiple_of((t - 1) * ROW_TILE, ROW_TILE) + i * Q_BLOCK, K_BLOCK)
        return first, keys, slice(first * LANES, K_BLOCK)

    def scores(i, p, first_tile):
        pl_ = slice(p * LANES, (p + 1) * LANES)
        _, keys, cols = band(i, first_tile)
        qp = q_s[i * Q_BLOCK:(i + 1) * Q_BLOCK, pl_]
        kp = kt_ref[0, pl_, keys]
        for half in range(2):
            h = 2 * p + half
            own = lo if half == 0 else jnp.logical_not(lo)
            s = _dot(jnp.where(own, qp, jnp.zeros_like(qp)), kp) + bias_ref[h, :, cols]
            s_s[h, :, cols] = s
            m_s[h] = jnp.broadcast_to(jnp.max(s, axis=1, keepdims=True), (Q_BLOCK, LANES))

    def scores_bounded(i, p, first_tile):
        pl_ = slice(p * LANES, (p + 1) * LANES)
        first, keys, cols = band(i, first_tile)
        qp = q_s[i * Q_BLOCK:(i + 1) * Q_BLOCK, pl_]
        kp = kt_ref[0, pl_, keys]
        for half in range(2):
            h = 2 * p + half
            own = lo if half == 0 else jnp.logical_not(lo)
            s = _dot(jnp.where(own, qp, jnp.zeros_like(qp)), kp) + bias_ref[h, :, cols]
            acc = jnp.zeros((Q_BLOCK, LANES), F32)
            for j in range(n_tiles - first):
                e = jnp.exp2(s[:, j * LANES:(j + 1) * LANES])
                acc = acc + e
                p_s[h, :, (first + j) * LANES:(first + j + 1) * LANES] = e.astype(BF16)
            l_s[h] = jnp.broadcast_to(jnp.sum(acc, axis=1, keepdims=True), (Q_BLOCK, LANES))

    def exps(i, p, first_tile):
        first, _, _ = band(i, first_tile)
        for h in (2 * p, 2 * p + 1):
            m = m_s[h]
            acc = jnp.zeros((Q_BLOCK, LANES), F32)
            for j in range(first, n_tiles):
                tl = slice(j * LANES, (j + 1) * LANES)
                e = jnp.exp2(s_s[h, :, tl] - m)
                acc = acc + e
                p_s[h, :, tl] = e.astype(BF16)
            l_s[h] = jnp.broadcast_to(jnp.sum(acc, axis=1, keepdims=True), (Q_BLOCK, LANES))

    def values(i, p, first_tile):
        pl_ = slice(p * LANES, (p + 1) * LANES)
        _, keys, cols = band(i, first_tile)
        vp = v_ref[0, keys, pl_]
        o0 = _dot(p_s[2 * p, :, cols], vp) / l_s[2 * p]
        o1 = _dot(p_s[2 * p + 1, :, cols], vp) / l_s[2 * p + 1]
        a_s[i * Q_BLOCK:(i + 1) * Q_BLOCK, pl_] = jnp.where(lo, o0, o1).astype(BF16)

    def block(i, first_tile, stages):
        for stage in stages:
            for p in range(n_pairs):
                stage(i, p, first_tile)

    first = t == 0
    for i in range(n_blocks):
        for stages, use in (((scores_bounded, values), bounded),
                            ((scores, exps, values), jnp.logical_not(bounded))):
            pl.when(jnp.logical_and(use, first))(functools.partial(block, i, True, stages))
            pl.when(jnp.logical_and(use, jnp.logical_not(first)))(functools.partial(block, i, False, stages))

    o_ref[0] = x + _dot(a_s[...], wo_ref[...])


def _attn_layer(h, g, w_q, w_o, bias, k_norm2, limit2, k_t, v):
    bsz, seq, _ = h.shape
    assert LEFT == ROW_TILE
    row = pl.BlockSpec((1, ROW_TILE, D_MODEL), lambda b, t: (b, t, 0))
    kt_all = pl.BlockSpec((1, D_MODEL, seq), lambda b, t: (b, 0, 0))
    v_all = pl.BlockSpec((1, seq, D_MODEL), lambda b, t: (b, 0, 0))
    return pl.pallas_call(
        _attn_kernel,
        grid=(bsz, seq // ROW_TILE),
        in_specs=[row, _const_spec((1, D_MODEL)), _const_spec((D_MODEL, D_MODEL)),
                  _const_spec((D_MODEL, D_MODEL)), _const_spec((B_HEADS, Q_BLOCK, K_BLOCK)),
                  pl.BlockSpec((1, B_HEADS // 2, LANES), lambda b, t: (b, 0, 0)),
                  _const_spec((B_HEADS // 2, LANES)), kt_all, v_all],
        out_specs=row,
        out_shape=jax.ShapeDtypeStruct(h.shape, F32),
        scratch_shapes=[pltpu.VMEM((ROW_TILE, D_MODEL), BF16),
                        pltpu.VMEM((ROW_TILE, D_MODEL), BF16),
                        pltpu.VMEM((B_HEADS, Q_BLOCK, K_BLOCK), F32),
                        pltpu.VMEM((B_HEADS, Q_BLOCK, K_BLOCK), BF16),
                        pltpu.VMEM((B_HEADS, Q_BLOCK, LANES), F32),
                        pltpu.VMEM((B_HEADS, Q_BLOCK, LANES), F32)],
        compiler_params=pltpu.CompilerParams(
            dimension_semantics=("arbitrary", "arbitrary"), vmem_limit_bytes=VMEM_LIMIT),
        name="attn",
    )(h, g, w_q, w_o, bias, k_norm2, limit2, k_t, v)


def _rel_bias(rel_table):
    n_heads = rel_table.shape[0]
    tab = rel_table.astype(F32) * LOG2E
    span = BAND + CHUNK - 1
    n_var = MAX_REL + CHUNK
    base = jnp.concatenate(
        [tab[:, MAX_REL - (CHUNK - 1):2 * MAX_REL + 1],
         jnp.broadcast_to(tab[:, 2 * MAX_REL:], (n_heads, span - n_var))], axis=1)
    b_max = jnp.max(base, axis=1, keepdims=True)
    b_min = jnp.min(base, axis=1, keepdims=True)
    base = base - 0.5 * (b_max + b_min)
    half_range = jnp.max((0.5 * (b_max - b_min)).reshape(n_heads // 2, 2), axis=1, keepdims=True)
    room = SCORE_LIMIT - half_range
    limit2 = jnp.broadcast_to(jnp.where(room > 0, room * room, -1.0), (n_heads // 2, LANES))
    rev = base[:, ::-1]
    padded = jnp.pad(rev, ((0, 0), (0, 1)))
    rows = jnp.tile(padded, (1, CHUNK))[:, :CHUNK * span].reshape(n_heads, CHUNK, span)
    band = rows[:, :, CHUNK - 1:CHUNK - 1 + BAND]
    ninf = jnp.full((n_heads, CHUNK, CHUNK), -jnp.inf, F32)
    bias = jnp.concatenate([jnp.concatenate([band, ninf], axis=2),
                            jnp.concatenate([ninf, band], axis=2)], axis=1)
    return bias, limit2


def _mlstm_params(w_in, b_gate, g_head):
    w_q = w_in[:, 0:A_QK]
    w_k = w_in[:, A_QK:2 * A_QK]
    w_vo = w_in[:, 2 * A_QK:A_GATE_OFF]
    w_g = w_in[:, A_GATE_OFF:]
    w_fm = jnp.concatenate([w_q, w_vo, w_g], axis=1).T.astype(BF16)
    b_g = jnp.broadcast_to(b_gate.astype(F32)[:, None], (2 * A_HEADS, ROW_TILE))
    g_head_b = jnp.broadcast_to(g_head.astype(F32)[:, None], (A_V, LANES))
    return w_fm, w_k.astype(BF16), b_g, g_head_b


def kernel(x, a_w_in, a_b_gate, a_g_head, a_w_out, b_w_q, b_rel_bias, b_w_o, kv_norm_g, w_kv,
           norm_mix_g, norm_ffn_g, ffn_w_gate, ffn_w_up, ffn_w_down, final_norm_g):
    bsz, seq, d = x.shape
    assert d == D_MODEL and seq % ROW_TILE == 0
    depth = norm_mix_g.shape[0]
    n_a = a_w_in.shape[0]

    def row(v):
        return v.reshape(1, -1).astype(F32)

    h = x
    k_t = v_sh = k_norm2 = None
    for l in range(depth):
        if l == n_a:
            k_t, v_sh, k_norm2 = _kv_proj(h, row(kv_norm_g), w_kv[:, :D_MODEL].T.astype(BF16),
                                          w_kv[:, D_MODEL:].astype(BF16))
            k_norm2 = jnp.max(k_norm2, axis=1)
        if l < n_a:
            h = _mlstm_layer(h, row(norm_mix_g[l]), *_mlstm_params(a_w_in[l], a_b_gate[l], a_g_head[l]),
                             a_w_out[l].astype(BF16))
        else:
            j = l - n_a
            bias, limit2 = _rel_bias(b_rel_bias[j])
            h = _attn_layer(h, row(norm_mix_g[l]), b_w_q[j].astype(BF16), b_w_o[j].astype(BF16),
                            bias, k_norm2, limit2, k_t, v_sh)
        h = _ffn(h.reshape(bsz * seq, d), row(norm_ffn_g[l]), ffn_w_gate[l].astype(BF16),
                 ffn_w_up[l].astype(BF16), ffn_w_down[l].astype(BF16), row(final_norm_g),
                 final_norm=(l == depth - 1)).reshape(bsz, seq, d)
    return h
```

```python
import functools

import jax
import jax.numpy as jnp
from jax import lax
from jax.experimental import pallas as pl
from jax.experimental.pallas import tpu as pltpu

F32 = jnp.float32
BF16 = jnp.bfloat16

D_MODEL = 1024
CHUNK = 64
A_HEADS = 8
A_DQK = 64
A_DV = 128
A_QK = A_HEADS * A_DQK
A_V = A_HEADS * A_DV
B_HEADS = 16
B_DH = 64
LEFT_CHUNKS = 8
LEFT = LEFT_CHUNKS * CHUNK
BAND = LEFT + CHUNK
MAX_REL = 256
D_FF = 2816
EPS = 1e-6
LOG2E = 1.4426950408889634
SCORE_LIMIT = 96.0
NEG_INIT = -1e30

LANES = 128
MXU_COLS = 256

ROW_TILE = 512
FFN_ROWS = 1024
A_SUPER = 128
FF_TILE = 256
VMEM_LIMIT = 56 * 1024 * 1024

A_GATE_OFF = 2 * A_QK + 2 * A_V


def _rmsnorm(x, g):
    return x * lax.rsqrt(jnp.mean(x * x, axis=-1, keepdims=True) + EPS) * g


def _dot(a, b):
    return jnp.dot(a, b, preferred_element_type=F32)


def _dot_nt(a, b):
    return lax.dot_general(a, b, (((1,), (1,)), ((), ())), preferred_element_type=F32)


def _dot_tn(a, b):
    return lax.dot_general(a, b, (((0,), (0,)), ((), ())), preferred_element_type=F32)


def _const_spec(shape):
    nd = len(shape)
    return pl.BlockSpec(shape, lambda *_: (0,) * nd, pipeline_mode=pl.Buffered(1))


def _ffn_kernel(x_ref, g_ref, wg_ref, wu_ref, wd_ref, gf_ref, o_ref, *, final_norm):
    x = x_ref[...]
    xn = _rmsnorm(x, g_ref[...]).astype(BF16)
    o_ref[...] = x
    for j in range(D_FF // FF_TILE):
        cols = slice(j * FF_TILE, (j + 1) * FF_TILE)
        hg = _dot(xn, wg_ref[:, cols])
        hu = _dot(xn, wu_ref[:, cols])
        act = (hg * jax.nn.sigmoid(hg)) * hu
        o_ref[...] += _dot(act.astype(BF16), wd_ref[cols, :])
    if final_norm:
        o_ref[...] = _rmsnorm(o_ref[...], gf_ref[...])


def _ffn(h2d, g, wg, wu, wd, gf, final_norm):
    n_tok = h2d.shape[0]
    row = pl.BlockSpec((FFN_ROWS, D_MODEL), lambda i: (i, 0))
    return pl.pallas_call(
        functools.partial(_ffn_kernel, final_norm=final_norm),
        grid=(n_tok // FFN_ROWS,),
        in_specs=[row, _const_spec((1, D_MODEL)), _const_spec((D_MODEL, D_FF)),
                  _const_spec((D_MODEL, D_FF)), _const_spec((D_FF, D_MODEL)),
                  _const_spec((1, D_MODEL))],
        out_specs=row,
        out_shape=jax.ShapeDtypeStruct(h2d.shape, F32),
        compiler_params=pltpu.CompilerParams(
            dimension_semantics=("arbitrary",), vmem_limit_bytes=VMEM_LIMIT),
        name="ffn",
    )(h2d, g, wg, wu, wd, gf)


A_CEXT = A_DV + 16
A_STEPS = ROW_TILE // A_SUPER


def _log_sigmoid(x):
    return jnp.minimum(x, 0.0) - jnp.log1p(jnp.exp(-jnp.abs(x)))


def _mlstm_kernel(x_ref, g_ref, wfm_ref, wk_ref, bg_ref, ghb_ref, wout_ref, o_ref,
                  qt_s, k_s, vt_s, ogt_s, hst_s, imb_s, bt_s, imbt_s, cmxt_s, c_s, m_s):
    t = pl.program_id(1)

    @pl.when(t == 0)
    def _():
        c_s[...] = jnp.zeros_like(c_s)
        m_s[...] = jnp.full_like(m_s, NEG_INIT)

    x = x_ref[0]
    xn = _rmsnorm(x, g_ref[...]).astype(BF16)
    gates = _dot_nt(wfm_ref[A_QK + 2 * A_V:A_QK + 2 * A_V + 2 * A_HEADS, :], xn) + bg_ref[...]
    i_log = gates[0:A_HEADS]
    f_log = _log_sigmoid(gates[A_HEADS:2 * A_HEADS])

    pos = lax.broadcasted_iota(jnp.int32, (A_HEADS, ROW_TILE), 1) & (A_SUPER - 1)
    b_cum = f_log
    shift = 1
    while shift < A_SUPER:
        b_cum = b_cum + jnp.where(pos >= shift, pltpu.roll(b_cum, shift, axis=1), 0.0)
        shift *= 2
    imb = i_log - b_cum
    cmx = imb
    shift = 1
    while shift < A_SUPER:
        cmx = jnp.where(pos >= shift, jnp.maximum(cmx, pltpu.roll(cmx, shift, axis=1)), cmx)
        shift *= 2
    bt_s[...] = b_cum
    imbt_s[...] = imb
    cmxt_s[...] = cmx
    pad = jnp.zeros((LANES - A_HEADS, ROW_TILE), F32)
    imb_s[...] = jnp.concatenate([imb * LOG2E, pad], axis=0).T

    qt_s[...] = _dot_nt(wfm_ref[0:A_QK, :], xn).astype(BF16)
    vt_s[...] = _dot_nt(wfm_ref[A_QK:A_QK + A_V, :], xn).astype(BF16)
    head_gain = jnp.concatenate([ghb_ref[...]] * (ROW_TILE // LANES), axis=1)
    ogt_s[...] = jax.nn.sigmoid(_dot_nt(wfm_ref[A_QK + A_V:A_QK + 2 * A_V, :], xn)) * head_gain
    k_s[...] = (_dot(xn, wk_ref[...]) * (A_DQK ** -0.5)).astype(BF16)

    row_i = lax.broadcasted_iota(jnp.int32, (A_SUPER, A_SUPER), 0)
    col_i = lax.broadcasted_iota(jnp.int32, (A_SUPER, A_SUPER), 1)
    causal = col_i >= row_i
    first_rows = row_i < A_DQK
    first_lanes = col_i < A_DQK
    ext_row = lax.broadcasted_iota(jnp.int32, (A_CEXT - A_DV, LANES), 0) == 0
    lane_row = lax.broadcasted_iota(jnp.int32, (1, LANES), 1) < A_DQK

    def pair_queries(p, r):
        qp = qt_s[p * LANES:(p + 1) * LANES, r]
        zero = jnp.zeros_like(qp)
        return jnp.concatenate([jnp.where(first_rows, qp, zero), jnp.where(first_rows, zero, qp)], axis=1)

    def pair_step(c, p):
        r = slice(c * A_SUPER, (c + 1) * A_SUPER)
        q_pair = pair_queries(p, r)
        kp = k_s[r, p * LANES:(p + 1) * LANES]
        st_pair = _dot(kp, q_pair)
        qc_pair = _dot(c_s[p].astype(BF16), q_pair)
        lhs, scale = [], []
        for half in range(2):
            h = 2 * p + half
            hr = slice(h * A_DV, (h + 1) * A_DV)
            vt_h = vt_s[hr, r]
            b_row = bt_s[h:h + 1, r]
            imb_row = imbt_s[h:h + 1, r]
            imb2_col = imb_s[r, h:h + 1]
            cmx_row = cmxt_s[h:h + 1, r]
            b_last = jnp.broadcast_to(b_row[:, A_SUPER - 1:A_SUPER], (1, LANES))
            m_prev = m_s[h]

            mrow = b_row + cmx_row
            d_rel = jnp.exp2(jnp.where(causal, imb2_col - cmx_row * LOG2E, -jnp.inf))
            st = st_pair[:, half * A_SUPER:(half + 1) * A_SUPER] * d_rel
            svt = _dot(vt_h, st.astype(BF16))
            rs = jnp.sum(st, axis=0, keepdims=True)

            inter = b_row + m_prev
            m_j = jnp.maximum(inter, mrow)
            f_intra = jnp.exp(mrow - m_j)
            w_inter = jnp.exp(inter - m_j)
            qc = qc_pair[:, half * A_SUPER:(half + 1) * A_SUPER]
            num = f_intra * svt + w_inter * qc[0:A_DV]
            den = f_intra * rs + w_inter * qc[A_DV:A_DV + 1]
            inv = 1.0 / jnp.maximum(jnp.abs(den), jnp.exp(-m_j))
            inv_rms = lax.rsqrt(inv * inv * jnp.mean(num * num, axis=0, keepdims=True) + EPS)
            hst_s[hr, r] = (ogt_s[hr, r] * (num * (inv * inv_rms))).astype(BF16)

            a_row = b_last + imb_row
            amax = jnp.broadcast_to(jnp.max(a_row, axis=1, keepdims=True), (1, LANES))
            wa = jnp.exp(a_row - amax)
            vw = (vt_h.astype(F32) * wa).astype(BF16)
            ext = jnp.where(ext_row, jnp.broadcast_to(wa, (A_CEXT - A_DV, LANES)), 0.0).astype(BF16)
            lhs.append(jnp.concatenate([vw, ext], axis=0))

            m_new = jnp.maximum(b_last + m_prev, amax)
            scale.append((jnp.exp(b_last + m_prev - m_new), jnp.exp(amax - m_new)))
            m_s[h] = m_new
        k_zero = jnp.zeros_like(kp)
        k_split = jnp.concatenate([jnp.where(first_lanes, kp, k_zero),
                                   jnp.where(first_lanes, k_zero, kp)], axis=0)
        up = _dot(jnp.concatenate(lhs, axis=1), k_split)
        decay = jnp.where(lane_row, scale[0][0], scale[1][0])
        grow = jnp.where(lane_row, scale[0][1], scale[1][1])
        c_s[p] = decay * c_s[p] + grow * up

    @pl.when(t >= 0)
    def _():
        for c in range(A_STEPS):
            for p in range(A_HEADS // 2):
                pair_step(c, p)

    o_ref[0] = x + _dot_tn(hst_s[...], wout_ref[...])


def _mlstm_layer(h, g, w_fm, w_k, b_g, g_head_b, w_out):
    bsz, seq, _ = h.shape
    row = pl.BlockSpec((1, ROW_TILE, D_MODEL), lambda b, t: (b, t, 0))
    return pl.pallas_call(
        _mlstm_kernel,
        grid=(bsz, seq // ROW_TILE),
        in_specs=[row, _const_spec((1, D_MODEL)), _const_spec((A_QK + 2 * A_V + 2 * A_HEADS, D_MODEL)),
                  _const_spec((D_MODEL, A_QK)), _const_spec((2 * A_HEADS, ROW_TILE)),
                  _const_spec((A_V, LANES)), _const_spec((A_V, D_MODEL))],
        out_specs=row,
        out_shape=jax.ShapeDtypeStruct(h.shape, F32),
        scratch_shapes=[
            pltpu.VMEM((A_QK, ROW_TILE), BF16),
            pltpu.VMEM((ROW_TILE, A_QK), BF16),
            pltpu.VMEM((A_V, ROW_TILE), BF16),
            pltpu.VMEM((A_V, ROW_TILE), F32),
            pltpu.VMEM((A_V, ROW_TILE), BF16),
            pltpu.VMEM((ROW_TILE, LANES), F32),
            pltpu.VMEM((A_HEADS, ROW_TILE), F32),
            pltpu.VMEM((A_HEADS, ROW_TILE), F32),
            pltpu.VMEM((A_HEADS, ROW_TILE), F32),
            pltpu.VMEM((A_HEADS // 2, A_CEXT, LANES), F32),
            pltpu.VMEM((A_HEADS, 1, LANES), F32),
        ],
        compiler_params=pltpu.CompilerParams(
            dimension_semantics=("arbitrary", "arbitrary"), vmem_limit_bytes=VMEM_LIMIT),
        name="mlstm",
    )(h, g, w_fm, w_k, b_g, g_head_b, w_out)


def _kv_kernel(x_ref, g_ref, wkt_ref, wv_ref, kt_ref, v_ref, kn_ref):
    xn = _rmsnorm(x_ref[0], g_ref[...]).astype(BF16)
    kt = _dot_nt(wkt_ref[...], xn).astype(BF16)
    kt_ref[0] = kt
    v_ref[0] = _dot(xn, wv_ref[...]).astype(BF16)
    kf = kt.astype(F32)
    sq = kf * kf
    for p in range(B_HEADS // 2):
        norm2 = jnp.sum(sq[p * LANES:(p + 1) * LANES], axis=0, keepdims=True)
        kn_ref[0, 0, p:p + 1, :] = jnp.broadcast_to(jnp.max(norm2, axis=1, keepdims=True), (1, LANES))


def _kv_proj(h, g, w_kt, w_v):
    bsz, seq, _ = h.shape
    row = pl.BlockSpec((1, ROW_TILE, D_MODEL), lambda b, t: (b, t, 0))
    col = pl.BlockSpec((1, D_MODEL, ROW_TILE), lambda b, t: (b, 0, t))
    return pl.pallas_call(
        _kv_kernel,
        grid=(bsz, seq // ROW_TILE),
        in_specs=[row, _const_spec((1, D_MODEL)), _const_spec((D_MODEL, D_MODEL)),
                  _const_spec((D_MODEL, D_MODEL))],
        out_specs=[col, row, pl.BlockSpec((1, 1, B_HEADS // 2, LANES), lambda b, t: (b, t, 0, 0))],
        out_shape=[jax.ShapeDtypeStruct((bsz, D_MODEL, seq), BF16),
                   jax.ShapeDtypeStruct((bsz, seq, D_MODEL), BF16),
                   jax.ShapeDtypeStruct((bsz, seq // ROW_TILE, B_HEADS // 2, LANES), F32)],
        compiler_params=pltpu.CompilerParams(
            dimension_semantics=("arbitrary", "arbitrary"), vmem_limit_bytes=VMEM_LIMIT),
        name="kv_proj",
    )(h, g, w_kt, w_v)


Q_BLOCK = 2 * CHUNK
K_BLOCK = LEFT + Q_BLOCK


def _attn_kernel(x_ref, g_ref, wq_ref, wo_ref, bias_ref, kn_ref, lim_ref,
                 ktp_ref, ktc_ref, vp_ref, vc_ref, o_ref,
                 q_s, a_s, s_s, p_s, m_s, l_s):
    t = pl.program_id(1)
    x = x_ref[0]
    xn = _rmsnorm(x, g_ref[...]).astype(BF16)
    q = _dot(xn, wq_ref[...]) * (B_DH ** -0.5 * LOG2E)
    q_s[...] = q.astype(BF16)

    excess = None
    for p in range(B_HEADS // 2):
        qp = q[:, p * LANES:(p + 1) * LANES]
        q_norm2 = jnp.max(jnp.sum(qp * qp, axis=1, keepdims=True))
        e = q_norm2 * kn_ref[0, p:p + 1, :] - lim_ref[p:p + 1, :]
        excess = e if excess is None else jnp.maximum(excess, e)
    bounded = jnp.max(excess) <= 0.0

    lane = lax.broadcasted_iota(jnp.int32, (Q_BLOCK, LANES), 1)
    lo = lane < B_DH
    n_tiles = K_BLOCK // LANES
    n_blocks = ROW_TILE // Q_BLOCK
    n_pairs = B_HEADS // 2
    left_tiles = LEFT // LANES

    def band(i, first_tile):
        first = left_tiles - i if first_tile else 0
        prev = None if first_tile else slice(i * Q_BLOCK, ROW_TILE)
        return first, (prev, slice(0, (i + 1) * Q_BLOCK)), slice(first * LANES, K_BLOCK)

    def band_keys(p, keys):
        pl_ = slice(p * LANES, (p + 1) * LANES)
        prev, cur = keys
        if prev is None:
            return ktc_ref[0, pl_, cur]
        return jnp.concatenate([ktp_ref[0, pl_, prev], ktc_ref[0, pl_, cur]], axis=1)

    def band_values(p, keys):
        pl_ = slice(p * LANES, (p + 1) * LANES)
        prev, cur = keys
        if prev is None:
            return vc_ref[0, cur, pl_]
        return jnp.concatenate([vp_ref[0, prev, pl_], vc_ref[0, cur, pl_]], axis=0)

    def scores(i, p, first_tile):
        pl_ = slice(p * LANES, (p + 1) * LANES)
        _, keys, cols = band(i, first_tile)
        qp = q_s[i * Q_BLOCK:(i + 1) * Q_BLOCK, pl_]
        kp = band_keys(p, keys)
        for half in range(2):
            h = 2 * p + half
            own = lo if half == 0 else jnp.logical_not(lo)
            s = _dot(jnp.where(own, qp, jnp.zeros_like(qp)), kp) + bias_ref[h, :, cols]
            s_s[h, :, cols] = s
            m_s[h] = jnp.broadcast_to(jnp.max(s, axis=1, keepdims=True), (Q_BLOCK, LANES))

    def scores_bounded(i, p, first_tile):
        pl_ = slice(p * LANES, (p + 1) * LANES)
        first, keys, cols = band(i, first_tile)
        qp = q_s[i * Q_BLOCK:(i + 1) * Q_BLOCK, pl_]
        kp = band_keys(p, keys)
        for half in range(2):
            h = 2 * p + half
            own = lo if half == 0 else jnp.logical_not(lo)
            s = _dot(jnp.where(own, qp, jnp.zeros_like(qp)), kp) + bias_ref[h, :, cols]
            acc = jnp.zeros((Q_BLOCK, LANES), F32)
            for j in range(n_tiles - first):
                e = jnp.exp2(s[:, j * LANES:(j + 1) * LANES])
                acc = acc + e
                p_s[h, :, (first + j) * LANES:(first + j + 1) * LANES] = e.astype(BF16)
            l_s[h] = jnp.broadcast_to(jnp.sum(acc, axis=1, keepdims=True), (Q_BLOCK, LANES))

    def exps(i, p, first_tile):
        first, _, _ = band(i, first_tile)
        for h in (2 * p, 2 * p + 1):
            m = m_s[h]
            acc = jnp.zeros((Q_BLOCK, LANES), F32)
            for j in range(first, n_tiles):
                tl = slice(j * LANES, (j + 1) * LANES)
                e = jnp.exp2(s_s[h, :, tl] - m)
                acc = acc + e
                p_s[h, :, tl] = e.astype(BF16)
            l_s[h] = jnp.broadcast_to(jnp.sum(acc, axis=1, keepdims=True), (Q_BLOCK, LANES))

    def values(i, p, first_tile):
        pl_ = slice(p * LANES, (p + 1) * LANES)
        _, keys, cols = band(i, first_tile)
        vp = band_values(p, keys)
        o0 = _dot(p_s[2 * p, :, cols], vp) / l_s[2 * p]
        o1 = _dot(p_s[2 * p + 1, :, cols], vp) / l_s[2 * p + 1]
        a_s[i * Q_BLOCK:(i + 1) * Q_BLOCK, pl_] = jnp.where(lo, o0, o1).astype(BF16)

    def block(i, first_tile, stages):
        for stage in stages:
            for p in range(n_pairs):
                stage(i, p, first_tile)

    first = t == 0
    for i in range(n_blocks):
        for stages, use in (((scores_bounded, values), bounded),
                            ((scores, exps, values), jnp.logical_not(bounded))):
            pl.when(jnp.logical_and(use, first))(functools.partial(block, i, True, stages))
            pl.when(jnp.logical_and(use, jnp.logical_not(first)))(functools.partial(block, i, False, stages))

    o_ref[0] = x + _dot(a_s[...], wo_ref[...])


def _attn_layer(h, g, w_q, w_o, bias, k_norm2, limit2, k_t, v):
    bsz, seq, _ = h.shape
    assert LEFT == ROW_TILE
    row = pl.BlockSpec((1, ROW_TILE, D_MODEL), lambda b, t: (b, t, 0))
    kt_prev = pl.BlockSpec((1, D_MODEL, ROW_TILE), lambda b, t: (b, 0, jnp.maximum(t - 1, 0)))
    kt_cur = pl.BlockSpec((1, D_MODEL, ROW_TILE), lambda b, t: (b, 0, t))
    v_prev = pl.BlockSpec((1, ROW_TILE, D_MODEL), lambda b, t: (b, jnp.maximum(t - 1, 0), 0))
    return pl.pallas_call(
        _attn_kernel,
        grid=(bsz, seq // ROW_TILE),
        in_specs=[row, _const_spec((1, D_MODEL)), _const_spec((D_MODEL, D_MODEL)),
                  _const_spec((D_MODEL, D_MODEL)), _const_spec((B_HEADS, Q_BLOCK, K_BLOCK)),
                  pl.BlockSpec((1, B_HEADS // 2, LANES), lambda b, t: (b, 0, 0)),
                  _const_spec((B_HEADS // 2, LANES)), kt_prev, kt_cur, v_prev, row],
        out_specs=row,
        out_shape=jax.ShapeDtypeStruct(h.shape, F32),
        scratch_shapes=[pltpu.VMEM((ROW_TILE, D_MODEL), BF16),
                        pltpu.VMEM((ROW_TILE, D_MODEL), BF16),
                        pltpu.VMEM((B_HEADS, Q_BLOCK, K_BLOCK), F32),
                        pltpu.VMEM((B_HEADS, Q_BLOCK, K_BLOCK), BF16),
                        pltpu.VMEM((B_HEADS, Q_BLOCK, LANES), F32),
                        pltpu.VMEM((B_HEADS, Q_BLOCK, LANES), F32)],
        compiler_params=pltpu.CompilerParams(
            dimension_semantics=("arbitrary", "arbitrary"), vmem_limit_bytes=VMEM_LIMIT),
        name="attn",
    )(h, g, w_q, w_o, bias, k_norm2, limit2, k_t, k_t, v, v)


def _rel_bias(rel_table):
    n_heads = rel_table.shape[0]
    tab = rel_table.astype(F32) * LOG2E
    span = BAND + CHUNK - 1
    n_var = MAX_REL + CHUNK
    base = jnp.concatenate(
        [tab[:, MAX_REL - (CHUNK - 1):2 * MAX_REL + 1],
         jnp.broadcast_to(tab[:, 2 * MAX_REL:], (n_heads, span - n_var))], axis=1)
    b_max = jnp.max(base, axis=1, keepdims=True)
    b_min = jnp.min(base, axis=1, keepdims=True)
    base = base - 0.5 * (b_max + b_min)
    half_range = jnp.max((0.5 * (b_max - b_min)).reshape(n_heads // 2, 2), axis=1, keepdims=True)
    room = SCORE_LIMIT - half_range
    limit2 = jnp.broadcast_to(jnp.where(room > 0, room * room, -1.0), (n_heads // 2, LANES))
    rev = base[:, ::-1]
    padded = jnp.pad(rev, ((0, 0), (0, 1)))
    rows = jnp.tile(padded, (1, CHUNK))[:, :CHUNK * span].reshape(n_heads, CHUNK, span)
    band = rows[:, :, CHUNK - 1:CHUNK - 1 + BAND]
    ninf = jnp.full((n_heads, CHUNK, CHUNK), -jnp.inf, F32)
    bias = jnp.concatenate([jnp.concatenate([band, ninf], axis=2),
                            jnp.concatenate([ninf, band], axis=2)], axis=1)
    return bias, limit2


def _mlstm_params(w_in, b_gate, g_head):
    w_q = w_in[:, 0:A_QK]
    w_k = w_in[:, A_QK:2 * A_QK]
    w_vo = w_in[:, 2 * A_QK:A_GATE_OFF]
    w_g = w_in[:, A_GATE_OFF:]
    w_fm = jnp.concatenate([w_q, w_vo, w_g], axis=1).T.astype(BF16)
    b_g = jnp.broadcast_to(b_gate.astype(F32)[:, None], (2 * A_HEADS, ROW_TILE))
    g_head_b = jnp.broadcast_to(g_head.astype(F32)[:, None], (A_V, LANES))
    return w_fm, w_k.astype(BF16), b_g, g_head_b


def kernel(x, a_w_in, a_b_gate, a_g_head, a_w_out, b_w_q, b_rel_bias, b_w_o, kv_norm_g, w_kv,
           norm_mix_g, norm_ffn_g, ffn_w_gate, ffn_w_up, ffn_w_down, final_norm_g):
    bsz, seq, d = x.shape
    assert d == D_MODEL and seq % ROW_TILE == 0
    depth = norm_mix_g.shape[0]
    n_a = a_w_in.shape[0]

    def row(v):
        return v.reshape(1, -1).astype(F32)

    h = x
    k_t = v_sh = k_norm2 = None
    for l in range(depth):
        if l == n_a:
            k_t, v_sh, k_norm2 = _kv_proj(h, row(kv_norm_g), w_kv[:, :D_MODEL].T.astype(BF16),
                                          w_kv[:, D_MODEL:].astype(BF16))
            k_norm2 = jnp.max(k_norm2, axis=1)
        if l < n_a:
            h = _mlstm_layer(h, row(norm_mix_g[l]), *_mlstm_params(a_w_in[l], a_b_gate[l], a_g_head[l]),
                             a_w_out[l].astype(BF16))
        else:
            j = l - n_a
            bias, limit2 = _rel_bias(b_rel_bias[j])
            h = _attn_layer(h, row(norm_mix_g[l]), b_w_q[j].astype(BF16), b_w_o[j].astype(BF16),
                            bias, k_norm2, limit2, k_t, v_sh)
        h = _ffn(h.reshape(bsz * seq, d), row(norm_ffn_g[l]), ffn_w_gate[l].astype(BF16),
                 ffn_w_up[l].astype(BF16), ffn_w_down[l].astype(BF16), row(final_norm_g),
                 final_norm=(l == depth - 1)).reshape(bsz, seq, d)
    return h
```

```python
import functools

import jax
import jax.numpy as jnp
from jax import lax
from jax.experimental import pallas as pl
from jax.experimental.pallas import tpu as pltpu

F32 = jnp.float32
BF16 = jnp.bfloat16

D_MODEL = 1024
CHUNK = 64
A_HEADS = 8
A_DQK = 64
A_DV = 128
A_QK = A_HEADS * A_DQK
A_V = A_HEADS * A_DV
B_HEADS = 16
B_DH = 64
LEFT_CHUNKS = 8
LEFT = LEFT_CHUNKS * CHUNK
BAND = LEFT + CHUNK
MAX_REL = 256
D_FF = 2816
EPS = 1e-6
LOG2E = 1.4426950408889634
SCORE_LIMIT = 96.0
NEG_INIT = -1e30

LANES = 128
MXU_COLS = 256

ROW_TILE = 512
FFN_ROWS = 1024
A_ROWS = 1024
A_SUPER = 128
FF_TILE = 256
VMEM_LIMIT = 56 * 1024 * 1024

A_GATE_OFF = 2 * A_QK + 2 * A_V


def _rmsnorm(x, g):
    return x * lax.rsqrt(jnp.mean(x * x, axis=-1, keepdims=True) + EPS) * g


def _dot(a, b):
    return jnp.dot(a, b, preferred_element_type=F32)


def _dot_nt(a, b):
    return lax.dot_general(a, b, (((1,), (1,)), ((), ())), preferred_element_type=F32)


def _dot_tn(a, b):
    return lax.dot_general(a, b, (((0,), (0,)), ((), ())), preferred_element_type=F32)


def _const_spec(shape):
    nd = len(shape)
    return pl.BlockSpec(shape, lambda *_: (0,) * nd, pipeline_mode=pl.Buffered(1))


def _ffn_kernel(x_ref, g_ref, wg_ref, wu_ref, wd_ref, gf_ref, o_ref, *, final_norm):
    x = x_ref[...]
    xn = _rmsnorm(x, g_ref[...]).astype(BF16)
    o_ref[...] = x
    for j in range(D_FF // FF_TILE):
        cols = slice(j * FF_TILE, (j + 1) * FF_TILE)
        hg = _dot(xn, wg_ref[:, cols])
        hu = _dot(xn, wu_ref[:, cols])
        act = (hg * jax.nn.sigmoid(hg)) * hu
        o_ref[...] += _dot(act.astype(BF16), wd_ref[cols, :])
    if final_norm:
        o_ref[...] = _rmsnorm(o_ref[...], gf_ref[...])


def _ffn(h2d, g, wg, wu, wd, gf, final_norm):
    n_tok = h2d.shape[0]
    row = pl.BlockSpec((FFN_ROWS, D_MODEL), lambda i: (i, 0))
    return pl.pallas_call(
        functools.partial(_ffn_kernel, final_norm=final_norm),
        grid=(n_tok // FFN_ROWS,),
        in_specs=[row, _const_spec((1, D_MODEL)), _const_spec((D_MODEL, D_FF)),
                  _const_spec((D_MODEL, D_FF)), _const_spec((D_FF, D_MODEL)),
                  _const_spec((1, D_MODEL))],
        out_specs=row,
        out_shape=jax.ShapeDtypeStruct(h2d.shape, F32),
        compiler_params=pltpu.CompilerParams(
            dimension_semantics=("arbitrary",), vmem_limit_bytes=VMEM_LIMIT),
        name="ffn",
    )(h2d, g, wg, wu, wd, gf)


A_CEXT = A_DV + 16
A_STEPS = A_ROWS // A_SUPER


def _log_sigmoid(x):
    return jnp.minimum(x, 0.0) - jnp.log1p(jnp.exp(-jnp.abs(x)))


def _mlstm_kernel(x_ref, g_ref, wfm_ref, wk_ref, bg_ref, ghb_ref, wout_ref, o_ref,
                  qt_s, k_s, vt_s, ogt_s, hst_s, imb_s, bt_s, imbt_s, cmxt_s, c_s, m_s):
    t = pl.program_id(1)

    @pl.when(t == 0)
    def _():
        c_s[...] = jnp.zeros_like(c_s)
        m_s[...] = jnp.full_like(m_s, NEG_INIT)

    x = x_ref[0]
    xn = _rmsnorm(x, g_ref[...]).astype(BF16)
    gates = _dot_nt(wfm_ref[A_QK + 2 * A_V:A_QK + 2 * A_V + 2 * A_HEADS, :], xn) + bg_ref[...]
    i_log = gates[0:A_HEADS]
    f_log = _log_sigmoid(gates[A_HEADS:2 * A_HEADS])

    pos = lax.broadcasted_iota(jnp.int32, (A_HEADS, A_ROWS), 1) & (A_SUPER - 1)
    b_cum = f_log
    shift = 1
    while shift < A_SUPER:
        b_cum = b_cum + jnp.where(pos >= shift, pltpu.roll(b_cum, shift, axis=1), 0.0)
        shift *= 2
    imb = i_log - b_cum
    cmx = imb
    shift = 1
    while shift < A_SUPER:
        cmx = jnp.where(pos >= shift, jnp.maximum(cmx, pltpu.roll(cmx, shift, axis=1)), cmx)
        shift *= 2
    bt_s[...] = b_cum
    imbt_s[...] = imb
    cmxt_s[...] = cmx
    pad = jnp.zeros((LANES - A_HEADS, A_ROWS), F32)
    imb_s[...] = jnp.concatenate([imb * LOG2E, pad], axis=0).T

    qt_s[...] = _dot_nt(wfm_ref[0:A_QK, :], xn).astype(BF16)
    vt_s[...] = _dot_nt(wfm_ref[A_QK:A_QK + A_V, :], xn).astype(BF16)
    head_gain = jnp.concatenate([ghb_ref[...]] * (A_ROWS // LANES), axis=1)
    ogt_s[...] = jax.nn.sigmoid(_dot_nt(wfm_ref[A_QK + A_V:A_QK + 2 * A_V, :], xn)) * head_gain
    k_s[...] = (_dot(xn, wk_ref[...]) * (A_DQK ** -0.5)).astype(BF16)

    row_i = lax.broadcasted_iota(jnp.int32, (A_SUPER, A_SUPER), 0)
    col_i = lax.broadcasted_iota(jnp.int32, (A_SUPER, A_SUPER), 1)
    causal = col_i >= row_i
    first_rows = row_i < A_DQK
    first_lanes = col_i < A_DQK
    ext_row = lax.broadcasted_iota(jnp.int32, (A_CEXT - A_DV, LANES), 0) == 0
    lane_row = lax.broadcasted_iota(jnp.int32, (1, LANES), 1) < A_DQK

    def pair_queries(p, r):
        qp = qt_s[p * LANES:(p + 1) * LANES, r]
        zero = jnp.zeros_like(qp)
        return jnp.concatenate([jnp.where(first_rows, qp, zero), jnp.where(first_rows, zero, qp)], axis=1)

    def pair_step(c, p):
        r = slice(c * A_SUPER, (c + 1) * A_SUPER)
        q_pair = pair_queries(p, r)
        kp = k_s[r, p * LANES:(p + 1) * LANES]
        st_pair = _dot(kp, q_pair)
        qc_pair = _dot(c_s[p].astype(BF16), q_pair)
        lhs, scale = [], []
        for half in range(2):
            h = 2 * p + half
            hr = slice(h * A_DV, (h + 1) * A_DV)
            vt_h = vt_s[hr, r]
            b_row = bt_s[h:h + 1, r]
            imb_row = imbt_s[h:h + 1, r]
            imb2_col = imb_s[r, h:h + 1]
            cmx_row = cmxt_s[h:h + 1, r]
            b_last = jnp.broadcast_to(b_row[:, A_SUPER - 1:A_SUPER], (1, LANES))
            m_prev = m_s[h]

            mrow = b_row + cmx_row
            d_rel = jnp.exp2(jnp.where(causal, imb2_col - cmx_row * LOG2E, -jnp.inf))
            st = st_pair[:, half * A_SUPER:(half + 1) * A_SUPER] * d_rel
            svt = _dot(vt_h, st.astype(BF16))
            rs = jnp.sum(st, axis=0, keepdims=True)

            inter = b_row + m_prev
            m_j = jnp.maximum(inter, mrow)
            f_intra = jnp.exp(mrow - m_j)
            w_inter = jnp.exp(inter - m_j)
            qc = qc_pair[:, half * A_SUPER:(half + 1) * A_SUPER]
            num = f_intra * svt + w_inter * qc[0:A_DV]
            den = f_intra * rs + w_inter * qc[A_DV:A_DV + 1]
            inv = 1.0 / jnp.maximum(jnp.abs(den), jnp.exp(-m_j))
            inv_rms = lax.rsqrt(inv * inv * jnp.mean(num * num, axis=0, keepdims=True) + EPS)
            hst_s[hr, r] = (ogt_s[hr, r] * (num * (inv * inv_rms))).astype(BF16)

            a_row = b_last + imb_row
            amax = jnp.broadcast_to(jnp.max(a_row, axis=1, keepdims=True), (1, LANES))
            wa = jnp.exp(a_row - amax)
            vw = (vt_h.astype(F32) * wa).astype(BF16)
            ext = jnp.where(ext_row, jnp.broadcast_to(wa, (A_CEXT - A_DV, LANES)), 0.0).astype(BF16)
            lhs.append(jnp.concatenate([vw, ext], axis=0))

            m_new = jnp.maximum(b_last + m_prev, amax)
            scale.append((jnp.exp(b_last + m_prev - m_new), jnp.exp(amax - m_new)))
            m_s[h] = m_new
        k_zero = jnp.zeros_like(kp)
        k_split = jnp.concatenate([jnp.where(first_lanes, kp, k_zero),
                                   jnp.where(first_lanes, k_zero, kp)], axis=0)
        up = _dot(jnp.concatenate(lhs, axis=1), k_split)
        decay = jnp.where(lane_row, scale[0][0], scale[1][0])
        grow = jnp.where(lane_row, scale[0][1], scale[1][1])
        c_s[p] = decay * c_s[p] + grow * up

    @pl.when(t >= 0)
    def _():
        for c in range(A_STEPS):
            for p in range(A_HEADS // 2):
                pair_step(c, p)

    o_ref[0] = x + _dot_tn(hst_s[...], wout_ref[...])


def _mlstm_layer(h, g, w_fm, w_k, b_g, g_head_b, w_out):
    bsz, seq, _ = h.shape
    row = pl.BlockSpec((1, A_ROWS, D_MODEL), lambda b, t: (b, t, 0))
    return pl.pallas_call(
        _mlstm_kernel,
        grid=(bsz, seq // A_ROWS),
        in_specs=[row, _const_spec((1, D_MODEL)), _const_spec((A_QK + 2 * A_V + 2 * A_HEADS, D_MODEL)),
                  _const_spec((D_MODEL, A_QK)), _const_spec((2 * A_HEADS, A_ROWS)),
                  _const_spec((A_V, LANES)), _const_spec((A_V, D_MODEL))],
        out_specs=row,
        out_shape=jax.ShapeDtypeStruct(h.shape, F32),
        scratch_shapes=[
            pltpu.VMEM((A_QK, A_ROWS), BF16),
            pltpu.VMEM((A_ROWS, A_QK), BF16),
            pltpu.VMEM((A_V, A_ROWS), BF16),
            pltpu.VMEM((A_V, A_ROWS), F32),
            pltpu.VMEM((A_V, A_ROWS), BF16),
            pltpu.VMEM((A_ROWS, LANES), F32),
            pltpu.VMEM((A_HEADS, A_ROWS), F32),
            pltpu.VMEM((A_HEADS, A_ROWS), F32),
            pltpu.VMEM((A_HEADS, A_ROWS), F32),
            pltpu.VMEM((A_HEADS // 2, A_CEXT, LANES), F32),
            pltpu.VMEM((A_HEADS, 1, LANES), F32),
        ],
        compiler_params=pltpu.CompilerParams(
            dimension_semantics=("arbitrary", "arbitrary"), vmem_limit_bytes=VMEM_LIMIT),
        name="mlstm",
    )(h, g, w_fm, w_k, b_g, g_head_b, w_out)


def _kv_kernel(x_ref, g_ref, wkt_ref, wv_ref, kt_ref, v_ref, kn_ref):
    xn = _rmsnorm(x_ref[0], g_ref[...]).astype(BF16)
    kt = _dot_nt(wkt_ref[...], xn).astype(BF16)
    kt_ref[0] = kt
    v_ref[0] = _dot(xn, wv_ref[...]).astype(BF16)
    kf = kt.astype(F32)
    sq = kf * kf
    for p in range(B_HEADS // 2):
        norm2 = jnp.sum(sq[p * LANES:(p + 1) * LANES], axis=0, keepdims=True)
        kn_ref[0, 0, p:p + 1, :] = jnp.broadcast_to(jnp.max(norm2, axis=1, keepdims=True), (1, LANES))


def _kv_proj(h, g, w_kt, w_v):
    bsz, seq, _ = h.shape
    row = pl.BlockSpec((1, ROW_TILE, D_MODEL), lambda b, t: (b, t, 0))
    col = pl.BlockSpec((1, D_MODEL, ROW_TILE), lambda b, t: (b, 0, t))
    return pl.pallas_call(
        _kv_kernel,
        grid=(bsz, seq // ROW_TILE),
        in_specs=[row, _const_spec((1, D_MODEL)), _const_spec((D_MODEL, D_MODEL)),
                  _const_spec((D_MODEL, D_MODEL))],
        out_specs=[col, row, pl.BlockSpec((1, 1, B_HEADS // 2, LANES), lambda b, t: (b, t, 0, 0))],
        out_shape=[jax.ShapeDtypeStruct((bsz, D_MODEL, seq), BF16),
                   jax.ShapeDtypeStruct((bsz, seq, D_MODEL), BF16),
                   jax.ShapeDtypeStruct((bsz, seq // ROW_TILE, B_HEADS // 2, LANES), F32)],
        compiler_params=pltpu.CompilerParams(
            dimension_semantics=("arbitrary", "arbitrary"), vmem_limit_bytes=VMEM_LIMIT),
        name="kv_proj",
    )(h, g, w_kt, w_v)


Q_BLOCK = 2 * CHUNK
K_BLOCK = LEFT + Q_BLOCK


def _attn_kernel(x_ref, g_ref, wq_ref, wo_ref, bias_ref, kn_ref, lim_ref,
                 ktp_ref, ktc_ref, vp_ref, vc_ref, o_ref,
                 q_s, a_s, s_s, p_s, m_s, l_s):
    t = pl.program_id(1)
    x = x_ref[0]
    xn = _rmsnorm(x, g_ref[...]).astype(BF16)
    q = _dot(xn, wq_ref[...]) * (B_DH ** -0.5 * LOG2E)
    q_s[...] = q.astype(BF16)

    excess = None
    for p in range(B_HEADS // 2):
        qp = q[:, p * LANES:(p + 1) * LANES]
        q_norm2 = jnp.max(jnp.sum(qp * qp, axis=1, keepdims=True))
        e = q_norm2 * kn_ref[0, p:p + 1, :] - lim_ref[p:p + 1, :]
        excess = e if excess is None else jnp.maximum(excess, e)
    bounded = jnp.max(excess) <= 0.0

    lane = lax.broadcasted_iota(jnp.int32, (Q_BLOCK, LANES), 1)
    lo = lane < B_DH
    n_tiles = K_BLOCK // LANES
    n_blocks = ROW_TILE // Q_BLOCK
    n_pairs = B_HEADS // 2
    left_tiles = LEFT // LANES

    def band(i, first_tile):
        first = left_tiles - i if first_tile else 0
        prev = None if first_tile else slice(i * Q_BLOCK, ROW_TILE)
        return first, (prev, slice(0, (i + 1) * Q_BLOCK)), slice(first * LANES, K_BLOCK)

    def band_keys(p, keys):
        pl_ = slice(p * LANES, (p + 1) * LANES)
        prev, cur = keys
        if prev is None:
            return ktc_ref[0, pl_, cur]
        return jnp.concatenate([ktp_ref[0, pl_, prev], ktc_ref[0, pl_, cur]], axis=1)

    def band_values(p, keys):
        pl_ = slice(p * LANES, (p + 1) * LANES)
        prev, cur = keys
        if prev is None:
            return vc_ref[0, cur, pl_]
        return jnp.concatenate([vp_ref[0, prev, pl_], vc_ref[0, cur, pl_]], axis=0)

    def scores(i, p, first_tile):
        pl_ = slice(p * LANES, (p + 1) * LANES)
        _, keys, cols = band(i, first_tile)
        qp = q_s[i * Q_BLOCK:(i + 1) * Q_BLOCK, pl_]
        kp = band_keys(p, keys)
        for half in range(2):
            h = 2 * p + half
            own = lo if half == 0 else jnp.logical_not(lo)
            s = _dot(jnp.where(own, qp, jnp.zeros_like(qp)), kp) + bias_ref[h, :, cols]
            s_s[h, :, cols] = s
            m_s[h] = jnp.broadcast_to(jnp.max(s, axis=1, keepdims=True), (Q_BLOCK, LANES))

    def scores_bounded(i, p, first_tile):
        pl_ = slice(p * LANES, (p + 1) * LANES)
        first, keys, cols = band(i, first_tile)
        qp = q_s[i * Q_BLOCK:(i + 1) * Q_BLOCK, pl_]
        kp = band_keys(p, keys)
        for half in range(2):
            h = 2 * p + half
            own = lo if half == 0 else jnp.logical_not(lo)
            s = _dot(jnp.where(own, qp, jnp.zeros_like(qp)), kp) + bias_ref[h, :, cols]
            acc = jnp.zeros((Q_BLOCK, LANES), F32)
            for j in range(n_tiles - first):
                e = jnp.exp2(s[:, j * LANES:(j + 1) * LANES])
                acc = acc + e
                p_s[h, :, (first + j) * LANES:(first + j + 1) * LANES] = e.astype(BF16)
            l_s[h] = jnp.broadcast_to(jnp.sum(acc, axis=1, keepdims=True), (Q_BLOCK, LANES))

    def exps(i, p, first_tile):
        first, _, _ = band(i, first_tile)
        for h in (2 * p, 2 * p + 1):
            m = m_s[h]
            acc = jnp.zeros((Q_BLOCK, LANES), F32)
            for j in range(first, n_tiles):
                tl = slice(j * LANES, (j + 1) * LANES)
                e = jnp.exp2(s_s[h, :, tl] - m)
                acc = acc + e
                p_s[h, :, tl] = e.astype(BF16)
            l_s[h] = jnp.broadcast_to(jnp.sum(acc, axis=1, keepdims=True), (Q_BLOCK, LANES))

    def values(i, p, first_tile):
        pl_ = slice(p * LANES, (p + 1) * LANES)
        _, keys, cols = band(i, first_tile)
        vp = band_values(p, keys)
        o0 = _dot(p_s[2 * p, :, cols], vp) / l_s[2 * p]
        o1 = _dot(p_s[2 * p + 1, :, cols], vp) / l_s[2 * p + 1]
        a_s[i * Q_BLOCK:(i + 1) * Q_BLOCK, pl_] = jnp.where(lo, o0, o1).astype(BF16)

    def block(i, first_tile, stages):
        for stage in stages:
            for p in range(n_pairs):
                stage(i, p, first_tile)

    first = t == 0
    for i in range(n_blocks):
        for stages, use in (((scores_bounded, values), bounded),
                            ((scores, exps, values), jnp.logical_not(bounded))):
            pl.when(jnp.logical_and(use, first))(functools.partial(block, i, True, stages))
            pl.when(jnp.logical_and(use, jnp.logical_not(first)))(functools.partial(block, i, False, stages))

    o_ref[0] = x + _dot(a_s[...], wo_ref[...])


def _attn_layer(h, g, w_q, w_o, bias, k_norm2, limit2, k_t, v):
    bsz, seq, _ = h.shape
    assert LEFT == ROW_TILE
    row = pl.BlockSpec((1, ROW_TILE, D_MODEL), lambda b, t: (b, t, 0))
    kt_prev = pl.BlockSpec((1, D_MODEL, ROW_TILE), lambda b, t: (b, 0, jnp.maximum(t - 1, 0)))
    kt_cur = pl.BlockSpec((1, D_MODEL, ROW_TILE), lambda b, t: (b, 0, t))
    v_prev = pl.BlockSpec((1, ROW_TILE, D_MODEL), lambda b, t: (b, jnp.maximum(t - 1, 0), 0))
    return pl.pallas_call(
        _attn_kernel,
        grid=(bsz, seq // ROW_TILE),
        in_specs=[row, _const_spec((1, D_MODEL)), _const_spec((D_MODEL, D_MODEL)),
                  _const_spec((D_MODEL, D_MODEL)), _const_spec((B_HEADS, Q_BLOCK, K_BLOCK)),
                  pl.BlockSpec((1, B_HEADS // 2, LANES), lambda b, t: (b, 0, 0)),
                  _const_spec((B_HEADS // 2, LANES)), kt_prev, kt_cur, v_prev, row],
        out_specs=row,
        out_shape=jax.ShapeDtypeStruct(h.shape, F32),
        scratch_shapes=[pltpu.VMEM((ROW_TILE, D_MODEL), BF16),
                        pltpu.VMEM((ROW_TILE, D_MODEL), BF16),
                        pltpu.VMEM((B_HEADS, Q_BLOCK, K_BLOCK), F32),
                        pltpu.VMEM((B_HEADS, Q_BLOCK, K_BLOCK), BF16),
                        pltpu.VMEM((B_HEADS, Q_BLOCK, LANES), F32),
                        pltpu.VMEM((B_HEADS, Q_BLOCK, LANES), F32)],
        compiler_params=pltpu.CompilerParams(
            dimension_semantics=("arbitrary", "arbitrary"), vmem_limit_bytes=VMEM_LIMIT),
        name="attn",
    )(h, g, w_q, w_o, bias, k_norm2, limit2, k_t, k_t, v, v)


def _rel_bias(rel_table):
    n_heads = rel_table.shape[0]
    tab = rel_table.astype(F32) * LOG2E
    span = BAND + CHUNK - 1
    n_var = MAX_REL + CHUNK
    base = jnp.concatenate(
        [tab[:, MAX_REL - (CHUNK - 1):2 * MAX_REL + 1],
         jnp.broadcast_to(tab[:, 2 * MAX_REL:], (n_heads, span - n_var))], axis=1)
    b_max = jnp.max(base, axis=1, keepdims=True)
    b_min = jnp.min(base, axis=1, keepdims=True)
    base = base - 0.5 * (b_max + b_min)
    half_range = jnp.max((0.5 * (b_max - b_min)).reshape(n_heads // 2, 2), axis=1, keepdims=True)
    room = SCORE_LIMIT - half_range
    limit2 = jnp.broadcast_to(jnp.where(room > 0, room * room, -1.0), (n_heads // 2, LANES))
    rev = base[:, ::-1]
    padded = jnp.pad(rev, ((0, 0), (0, 1)))
    rows = jnp.tile(padded, (1, CHUNK))[:, :CHUNK * span].reshape(n_heads, CHUNK, span)
    band = rows[:, :, CHUNK - 1:CHUNK - 1 + BAND]
    ninf = jnp.full((n_heads, CHUNK, CHUNK), -jnp.inf, F32)
    bias = jnp.concatenate([jnp.concatenate([band, ninf], axis=2),
                            jnp.concatenate([ninf, band], axis=2)], axis=1)
    return bias, limit2


def _mlstm_params(w_in, b_gate, g_head):
    w_q = w_in[:, 0:A_QK]
    w_k = w_in[:, A_QK:2 * A_QK]
    w_vo = w_in[:, 2 * A_QK:A_GATE_OFF]
    w_g = w_in[:, A_GATE_OFF:]
    w_fm = jnp.concatenate([w_q, w_vo, w_g], axis=1).T.astype(BF16)
    b_g = jnp.broadcast_to(b_gate.astype(F32)[:, None], (2 * A_HEADS, A_ROWS))
    g_head_b = jnp.broadcast_to(g_head.astype(F32)[:, None], (A_V, LANES))
    return w_fm, w_k.astype(BF16), b_g, g_head_b


def kernel(x, a_w_in, a_b_gate, a_g_head, a_w_out, b_w_q, b_rel_bias, b_w_o, kv_norm_g, w_kv,
           norm_mix_g, norm_ffn_g, ffn_w_gate, ffn_w_up, ffn_w_down, final_norm_g):
    bsz, seq, d = x.shape
    assert d == D_MODEL and seq % max(ROW_TILE, A_ROWS) == 0
    depth = norm_mix_g.shape[0]
    n_a = a_w_in.shape[0]

    def row(v):
        return v.reshape(1, -1).astype(F32)

    h = x
    k_t = v_sh = k_norm2 = None
    for l in range(depth):
        if l == n_a:
            k_t, v_sh, k_norm2 = _kv_proj(h, row(kv_norm_g), w_kv[:, :D_MODEL].T.astype(BF16),
                                          w_kv[:, D_MODEL:].astype(BF16))
            k_norm2 = jnp.max(k_norm2, axis=1)
        if l < n_a:
            h = _mlstm_layer(h, row(norm_mix_g[l]), *_mlstm_params(a_w_in[l], a_b_gate[l], a_g_head[l]),
                             a_w_out[l].astype(BF16))
        else:
            j = l - n_a
            bias, limit2 = _rel_bias(b_rel_bias[j])
            h = _attn_layer(h, row(norm_mix_g[l]), b_w_q[j].astype(BF16), b_w_o[j].astype(BF16),
                            bias, k_norm2, limit2, k_t, v_sh)
        h = _ffn(h.reshape(bsz * seq, d), row(norm_ffn_g[l]), ffn_w_gate[l].astype(BF16),
                 ffn_w_up[l].astype(BF16), ffn_w_down[l].astype(BF16), row(final_norm_g),
                 final_norm=(l == depth - 1)).reshape(bsz, seq, d)
    return h
```

```python
import functools

import jax
import jax.numpy as jnp
from jax import lax
from jax.experimental import pallas as pl
from jax.experimental.pallas import tpu as pltpu

F32 = jnp.float32
BF16 = jnp.bfloat16

D_MODEL = 1024
CHUNK = 64
A_HEADS = 8
A_DQK = 64
A_DV = 128
A_QK = A_HEADS * A_DQK
A_V = A_HEADS * A_DV
B_HEADS = 16
B_DH = 64
LEFT_CHUNKS = 8
LEFT = LEFT_CHUNKS * CHUNK
BAND = LEFT + CHUNK
MAX_REL = 256
D_FF = 2816
EPS = 1e-6
LOG2E = 1.4426950408889634
SCORE_LIMIT = 96.0
NEG_INIT = -1e30

LANES = 128
MXU_COLS = 256

ROW_TILE = 512
FFN_ROWS = 1024
A_ROWS = 1024
KV_ROWS = 1024
A_SUPER = 128
FF_TILE = 256
VMEM_LIMIT = 56 * 1024 * 1024

A_GATE_OFF = 2 * A_QK + 2 * A_V


def _rmsnorm(x, g):
    return x * lax.rsqrt(jnp.mean(x * x, axis=-1, keepdims=True) + EPS) * g


def _dot(a, b):
    return jnp.dot(a, b, preferred_element_type=F32)


def _dot_nt(a, b):
    return lax.dot_general(a, b, (((1,), (1,)), ((), ())), preferred_element_type=F32)


def _dot_tn(a, b):
    return lax.dot_general(a, b, (((0,), (0,)), ((), ())), preferred_element_type=F32)


def _const_spec(shape):
    nd = len(shape)
    return pl.BlockSpec(shape, lambda *_: (0,) * nd, pipeline_mode=pl.Buffered(1))


def _ffn_kernel(x_ref, g_ref, wg_ref, wu_ref, wd_ref, gf_ref, o_ref, *, final_norm):
    x = x_ref[...]
    xn = _rmsnorm(x, g_ref[...]).astype(BF16)
    o_ref[...] = x
    for j in range(D_FF // FF_TILE):
        cols = slice(j * FF_TILE, (j + 1) * FF_TILE)
        hg = _dot(xn, wg_ref[:, cols])
        hu = _dot(xn, wu_ref[:, cols])
        act = (hg * jax.nn.sigmoid(hg)) * hu
        o_ref[...] += _dot(act.astype(BF16), wd_ref[cols, :])
    if final_norm:
        o_ref[...] = _rmsnorm(o_ref[...], gf_ref[...])


def _ffn(h2d, g, wg, wu, wd, gf, final_norm):
    n_tok = h2d.shape[0]
    row = pl.BlockSpec((FFN_ROWS, D_MODEL), lambda i: (i, 0))
    return pl.pallas_call(
        functools.partial(_ffn_kernel, final_norm=final_norm),
        grid=(n_tok // FFN_ROWS,),
        in_specs=[row, _const_spec((1, D_MODEL)), _const_spec((D_MODEL, D_FF)),
                  _const_spec((D_MODEL, D_FF)), _const_spec((D_FF, D_MODEL)),
                  _const_spec((1, D_MODEL))],
        out_specs=row,
        out_shape=jax.ShapeDtypeStruct(h2d.shape, F32),
        compiler_params=pltpu.CompilerParams(
            dimension_semantics=("arbitrary",), vmem_limit_bytes=VMEM_LIMIT),
        name="ffn",
    )(h2d, g, wg, wu, wd, gf)


A_CEXT = A_DV + 16
A_STEPS = A_ROWS // A_SUPER


def _log_sigmoid(x):
    return jnp.minimum(x, 0.0) - jnp.log1p(jnp.exp(-jnp.abs(x)))


def _mlstm_kernel(x_ref, g_ref, wfm_ref, wk_ref, bg_ref, ghb_ref, wout_ref, o_ref,
                  qt_s, k_s, vt_s, ogt_s, hst_s, imb_s, bt_s, imbt_s, cmxt_s, c_s, m_s):
    t = pl.program_id(1)

    @pl.when(t == 0)
    def _():
        c_s[...] = jnp.zeros_like(c_s)
        m_s[...] = jnp.full_like(m_s, NEG_INIT)

    x = x_ref[0]
    xn = _rmsnorm(x, g_ref[...]).astype(BF16)
    gates = _dot_nt(wfm_ref[A_QK + 2 * A_V:A_QK + 2 * A_V + 2 * A_HEADS, :], xn) + bg_ref[...]
    i_log = gates[0:A_HEADS]
    f_log = _log_sigmoid(gates[A_HEADS:2 * A_HEADS])

    pos = lax.broadcasted_iota(jnp.int32, (A_HEADS, A_ROWS), 1) & (A_SUPER - 1)
    b_cum = f_log
    shift = 1
    while shift < A_SUPER:
        b_cum = b_cum + jnp.where(pos >= shift, pltpu.roll(b_cum, shift, axis=1), 0.0)
        shift *= 2
    imb = i_log - b_cum
    cmx = imb
    shift = 1
    while shift < A_SUPER:
        cmx = jnp.where(pos >= shift, jnp.maximum(cmx, pltpu.roll(cmx, shift, axis=1)), cmx)
        shift *= 2
    bt_s[...] = b_cum
    imbt_s[...] = imb
    cmxt_s[...] = cmx
    pad = jnp.zeros((LANES - A_HEADS, A_ROWS), F32)
    imb_s[...] = jnp.concatenate([imb * LOG2E, pad], axis=0).T

    qt_s[...] = _dot_nt(wfm_ref[0:A_QK, :], xn).astype(BF16)
    vt_s[...] = _dot_nt(wfm_ref[A_QK:A_QK + A_V, :], xn).astype(BF16)
    head_gain = jnp.concatenate([ghb_ref[...]] * (A_ROWS // LANES), axis=1)
    ogt_s[...] = jax.nn.sigmoid(_dot_nt(wfm_ref[A_QK + A_V:A_QK + 2 * A_V, :], xn)) * head_gain
    k_s[...] = (_dot(xn, wk_ref[...]) * (A_DQK ** -0.5)).astype(BF16)

    row_i = lax.broadcasted_iota(jnp.int32, (A_SUPER, A_SUPER), 0)
    col_i = lax.broadcasted_iota(jnp.int32, (A_SUPER, A_SUPER), 1)
    causal = col_i >= row_i
    first_rows = row_i < A_DQK
    first_lanes = col_i < A_DQK
    ext_row = lax.broadcasted_iota(jnp.int32, (A_CEXT - A_DV, LANES), 0) == 0
    lane_row = lax.broadcasted_iota(jnp.int32, (1, LANES), 1) < A_DQK

    def pair_queries(p, r):
        qp = qt_s[p * LANES:(p + 1) * LANES, r]
        zero = jnp.zeros_like(qp)
        return jnp.concatenate([jnp.where(first_rows, qp, zero), jnp.where(first_rows, zero, qp)], axis=1)

    def pair_step(c, p):
        r = slice(c * A_SUPER, (c + 1) * A_SUPER)
        q_pair = pair_queries(p, r)
        kp = k_s[r, p * LANES:(p + 1) * LANES]
        st_pair = _dot(kp, q_pair)
        qc_pair = _dot(c_s[p].astype(BF16), q_pair)
        lhs, scale = [], []
        for half in range(2):
            h = 2 * p + half
            hr = slice(h * A_DV, (h + 1) * A_DV)
            vt_h = vt_s[hr, r]
            b_row = bt_s[h:h + 1, r]
            imb_row = imbt_s[h:h + 1, r]
            imb2_col = imb_s[r, h:h + 1]
            cmx_row = cmxt_s[h:h + 1, r]
            b_last = jnp.broadcast_to(b_row[:, A_SUPER - 1:A_SUPER], (1, LANES))
            m_prev = m_s[h]

            mrow = b_row + cmx_row
            d_rel = jnp.exp2(jnp.where(causal, imb2_col - cmx_row * LOG2E, -jnp.inf))
            st = st_pair[:, half * A_SUPER:(half + 1) * A_SUPER] * d_rel
            svt = _dot(vt_h, st.astype(BF16))
            rs = jnp.sum(st, axis=0, keepdims=True)

            inter = b_row + m_prev
            m_j = jnp.maximum(inter, mrow)
            f_intra = jnp.exp(mrow - m_j)
            w_inter = jnp.exp(inter - m_j)
            qc = qc_pair[:, half * A_SUPER:(half + 1) * A_SUPER]
            num = f_intra * svt + w_inter * qc[0:A_DV]
            den = f_intra * rs + w_inter * qc[A_DV:A_DV + 1]
            inv = 1.0 / jnp.maximum(jnp.abs(den), jnp.exp(-m_j))
            inv_rms = lax.rsqrt(inv * inv * jnp.mean(num * num, axis=0, keepdims=True) + EPS)
            hst_s[hr, r] = (ogt_s[hr, r] * (num * (inv * inv_rms))).astype(BF16)

            a_row = b_last + imb_row
            amax = jnp.broadcast_to(jnp.max(a_row, axis=1, keepdims=True), (1, LANES))
            wa = jnp.exp(a_row - amax)
            vw = (vt_h.astype(F32) * wa).astype(BF16)
            ext = jnp.where(ext_row, jnp.broadcast_to(wa, (A_CEXT - A_DV, LANES)), 0.0).astype(BF16)
            lhs.append(jnp.concatenate([vw, ext], axis=0))

            m_new = jnp.maximum(b_last + m_prev, amax)
            scale.append((jnp.exp(b_last + m_prev - m_new), jnp.exp(amax - m_new)))
            m_s[h] = m_new
        k_zero = jnp.zeros_like(kp)
        k_split = jnp.concatenate([jnp.where(first_lanes, kp, k_zero),
                                   jnp.where(first_lanes, k_zero, kp)], axis=0)
        up = _dot(jnp.concatenate(lhs, axis=1), k_split)
        decay = jnp.where(lane_row, scale[0][0], scale[1][0])
        grow = jnp.where(lane_row, scale[0][1], scale[1][1])
        c_s[p] = decay * c_s[p] + grow * up

    @pl.when(t >= 0)
    def _():
        for c in range(A_STEPS):
            for p in range(A_HEADS // 2):
                pair_step(c, p)

    o_ref[0] = x + _dot_tn(hst_s[...], wout_ref[...])


def _mlstm_layer(h, g, w_fm, w_k, b_g, g_head_b, w_out):
    bsz, seq, _ = h.shape
    row = pl.BlockSpec((1, A_ROWS, D_MODEL), lambda b, t: (b, t, 0))
    return pl.pallas_call(
        _mlstm_kernel,
        grid=(bsz, seq // A_ROWS),
        in_specs=[row, _const_spec((1, D_MODEL)), _const_spec((A_QK + 2 * A_V + 2 * A_HEADS, D_MODEL)),
                  _const_spec((D_MODEL, A_QK)), _const_spec((2 * A_HEADS, A_ROWS)),
                  _const_spec((A_V, LANES)), _const_spec((A_V, D_MODEL))],
        out_specs=row,
        out_shape=jax.ShapeDtypeStruct(h.shape, F32),
        scratch_shapes=[
            pltpu.VMEM((A_QK, A_ROWS), BF16),
            pltpu.VMEM((A_ROWS, A_QK), BF16),
            pltpu.VMEM((A_V, A_ROWS), BF16),
            pltpu.VMEM((A_V, A_ROWS), F32),
            pltpu.VMEM((A_V, A_ROWS), BF16),
            pltpu.VMEM((A_ROWS, LANES), F32),
            pltpu.VMEM((A_HEADS, A_ROWS), F32),
            pltpu.VMEM((A_HEADS, A_ROWS), F32),
            pltpu.VMEM((A_HEADS, A_ROWS), F32),
            pltpu.VMEM((A_HEADS // 2, A_CEXT, LANES), F32),
            pltpu.VMEM((A_HEADS, 1, LANES), F32),
        ],
        compiler_params=pltpu.CompilerParams(
            dimension_semantics=("arbitrary", "arbitrary"), vmem_limit_bytes=VMEM_LIMIT),
        name="mlstm",
    )(h, g, w_fm, w_k, b_g, g_head_b, w_out)


def _kv_kernel(x_ref, g_ref, wkt_ref, wv_ref, kt_ref, v_ref, kn_ref):
    xn = _rmsnorm(x_ref[0], g_ref[...]).astype(BF16)
    kt = _dot_nt(wkt_ref[...], xn).astype(BF16)
    kt_ref[0] = kt
    v_ref[0] = _dot(xn, wv_ref[...]).astype(BF16)
    kf = kt.astype(F32)
    sq = kf * kf
    for p in range(B_HEADS // 2):
        norm2 = jnp.sum(sq[p * LANES:(p + 1) * LANES], axis=0, keepdims=True)
        kn_ref[0, 0, p:p + 1, :] = jnp.broadcast_to(jnp.max(norm2, axis=1, keepdims=True), (1, LANES))


def _kv_proj(h, g, w_kt, w_v):
    bsz, seq, _ = h.shape
    row = pl.BlockSpec((1, KV_ROWS, D_MODEL), lambda b, t: (b, t, 0))
    col = pl.BlockSpec((1, D_MODEL, KV_ROWS), lambda b, t: (b, 0, t))
    return pl.pallas_call(
        _kv_kernel,
        grid=(bsz, seq // KV_ROWS),
        in_specs=[row, _const_spec((1, D_MODEL)), _const_spec((D_MODEL, D_MODEL)),
                  _const_spec((D_MODEL, D_MODEL))],
        out_specs=[col, row, pl.BlockSpec((1, 1, B_HEADS // 2, LANES), lambda b, t: (b, t, 0, 0))],
        out_shape=[jax.ShapeDtypeStruct((bsz, D_MODEL, seq), BF16),
                   jax.ShapeDtypeStruct((bsz, seq, D_MODEL), BF16),
                   jax.ShapeDtypeStruct((bsz, seq // KV_ROWS, B_HEADS // 2, LANES), F32)],
        compiler_params=pltpu.CompilerParams(
            dimension_semantics=("arbitrary", "arbitrary"), vmem_limit_bytes=VMEM_LIMIT),
        name="kv_proj",
    )(h, g, w_kt, w_v)


Q_BLOCK = 2 * CHUNK
K_BLOCK = LEFT + Q_BLOCK


def _attn_kernel(x_ref, g_ref, wq_ref, wo_ref, bias_ref, kn_ref, lim_ref,
                 ktp_ref, ktc_ref, vp_ref, vc_ref, o_ref,
                 q_s, a_s, s_s, p_s, m_s, l_s):
    t = pl.program_id(1)
    x = x_ref[0]
    xn = _rmsnorm(x, g_ref[...]).astype(BF16)
    q = _dot(xn, wq_ref[...]) * (B_DH ** -0.5 * LOG2E)
    q_s[...] = q.astype(BF16)

    excess = None
    for p in range(B_HEADS // 2):
        qp = q[:, p * LANES:(p + 1) * LANES]
        q_norm2 = jnp.max(jnp.sum(qp * qp, axis=1, keepdims=True))
        e = q_norm2 * kn_ref[0, p:p + 1, :] - lim_ref[p:p + 1, :]
        excess = e if excess is None else jnp.maximum(excess, e)
    bounded = jnp.max(excess) <= 0.0

    lane = lax.broadcasted_iota(jnp.int32, (Q_BLOCK, LANES), 1)
    lo = lane < B_DH
    n_tiles = K_BLOCK // LANES
    n_blocks = ROW_TILE // Q_BLOCK
    n_pairs = B_HEADS // 2
    left_tiles = LEFT // LANES

    def band(i, first_tile):
        first = left_tiles - i if first_tile else 0
        prev = None if first_tile else slice(i * Q_BLOCK, ROW_TILE)
        return first, (prev, slice(0, (i + 1) * Q_BLOCK)), slice(first * LANES, K_BLOCK)

    def band_keys(p, keys):
        pl_ = slice(p * LANES, (p + 1) * LANES)
        prev, cur = keys
        if prev is None:
            return ktc_ref[0, pl_, cur]
        return jnp.concatenate([ktp_ref[0, pl_, prev], ktc_ref[0, pl_, cur]], axis=1)

    def band_values(p, keys):
        pl_ = slice(p * LANES, (p + 1) * LANES)
        prev, cur = keys
        if prev is None:
            return vc_ref[0, cur, pl_]
        return jnp.concatenate([vp_ref[0, prev, pl_], vc_ref[0, cur, pl_]], axis=0)

    def scores(i, p, first_tile):
        pl_ = slice(p * LANES, (p + 1) * LANES)
        _, keys, cols = band(i, first_tile)
        qp = q_s[i * Q_BLOCK:(i + 1) * Q_BLOCK, pl_]
        kp = band_keys(p, keys)
        for half in range(2):
            h = 2 * p + half
            own = lo if half == 0 else jnp.logical_not(lo)
            s = _dot(jnp.where(own, qp, jnp.zeros_like(qp)), kp) + bias_ref[h, :, cols]
            s_s[h, :, cols] = s
            m_s[h] = jnp.broadcast_to(jnp.max(s, axis=1, keepdims=True), (Q_BLOCK, LANES))

    def scores_bounded(i, p, first_tile):
        pl_ = slice(p * LANES, (p + 1) * LANES)
        first, keys, cols = band(i, first_tile)
        qp = q_s[i * Q_BLOCK:(i + 1) * Q_BLOCK, pl_]
        kp = band_keys(p, keys)
        for half in range(2):
            h = 2 * p + half
            own = lo if half == 0 else jnp.logical_not(lo)
            s = _dot(jnp.where(own, qp, jnp.zeros_like(qp)), kp) + bias_ref[h, :, cols]
            acc = jnp.zeros((Q_BLOCK, LANES), F32)
            for j in range(n_tiles - first):
                e = jnp.exp2(s[:, j * LANES:(j + 1) * LANES])
                acc = acc + e
                p_s[h, :, (first + j) * LANES:(first + j + 1) * LANES] = e.astype(BF16)
            l_s[h] = jnp.broadcast_to(jnp.sum(acc, axis=1, keepdims=True), (Q_BLOCK, LANES))

    def exps(i, p, first_tile):
        first, _, _ = band(i, first_tile)
        for h in (2 * p, 2 * p + 1):
            m = m_s[h]
            acc = jnp.zeros((Q_BLOCK, LANES), F32)
            for j in range(first, n_tiles):
                tl = slice(j * LANES, (j + 1) * LANES)
                e = jnp.exp2(s_s[h, :, tl] - m)
                acc = acc + e
                p_s[h, :, tl] = e.astype(BF16)
            l_s[h] = jnp.broadcast_to(jnp.sum(acc, axis=1, keepdims=True), (Q_BLOCK, LANES))

    def values(i, p, first_tile):
        pl_ = slice(p * LANES, (p + 1) * LANES)
        _, keys, cols = band(i, first_tile)
        vp = band_values(p, keys)
        o0 = _dot(p_s[2 * p, :, cols], vp) / l_s[2 * p]
        o1 = _dot(p_s[2 * p + 1, :, cols], vp) / l_s[2 * p + 1]
        a_s[i * Q_BLOCK:(i + 1) * Q_BLOCK, pl_] = jnp.where(lo, o0, o1).astype(BF16)

    def block(i, first_tile, stages):
        for stage in stages:
            for p in range(n_pairs):
                stage(i, p, first_tile)

    first = t == 0
    for i in range(n_blocks):
        for stages, use in (((scores_bounded, values), bounded),
                            ((scores, exps, values), jnp.logical_not(bounded))):
            pl.when(jnp.logical_and(use, first))(functools.partial(block, i, True, stages))
            pl.when(jnp.logical_and(use, jnp.logical_not(first)))(functools.partial(block, i, False, stages))

    o_ref[0] = x + _dot(a_s[...], wo_ref[...])


def _attn_layer(h, g, w_q, w_o, bias, k_norm2, limit2, k_t, v):
    bsz, seq, _ = h.shape
    assert LEFT == ROW_TILE
    row = pl.BlockSpec((1, ROW_TILE, D_MODEL), lambda b, t: (b, t, 0))
    kt_prev = pl.BlockSpec((1, D_MODEL, ROW_TILE), lambda b, t: (b, 0, jnp.maximum(t - 1, 0)))
    kt_cur = pl.BlockSpec((1, D_MODEL, ROW_TILE), lambda b, t: (b, 0, t))
    v_prev = pl.BlockSpec((1, ROW_TILE, D_MODEL), lambda b, t: (b, jnp.maximum(t - 1, 0), 0))
    return pl.pallas_call(
        _attn_kernel,
        grid=(bsz, seq // ROW_TILE),
        in_specs=[row, _const_spec((1, D_MODEL)), _const_spec((D_MODEL, D_MODEL)),
                  _const_spec((D_MODEL, D_MODEL)), _const_spec((B_HEADS, Q_BLOCK, K_BLOCK)),
                  pl.BlockSpec((1, B_HEADS // 2, LANES), lambda b, t: (b, 0, 0)),
                  _const_spec((B_HEADS // 2, LANES)), kt_prev, kt_cur, v_prev, row],
        out_specs=row,
        out_shape=jax.ShapeDtypeStruct(h.shape, F32),
        scratch_shapes=[pltpu.VMEM((ROW_TILE, D_MODEL), BF16),
                        pltpu.VMEM((ROW_TILE, D_MODEL), BF16),
                        pltpu.VMEM((B_HEADS, Q_BLOCK, K_BLOCK), F32),
                        pltpu.VMEM((B_HEADS, Q_BLOCK, K_BLOCK), BF16),
                        pltpu.VMEM((B_HEADS, Q_BLOCK, LANES), F32),
                        pltpu.VMEM((B_HEADS, Q_BLOCK, LANES), F32)],
        compiler_params=pltpu.CompilerParams(
            dimension_semantics=("arbitrary", "arbitrary"), vmem_limit_bytes=VMEM_LIMIT),
        name="attn",
    )(h, g, w_q, w_o, bias, k_norm2, limit2, k_t, k_t, v, v)


def _rel_bias(rel_table):
    n_heads = rel_table.shape[0]
    tab = rel_table.astype(F32) * LOG2E
    span = BAND + CHUNK - 1
    n_var = MAX_REL + CHUNK
    base = jnp.concatenate(
        [tab[:, MAX_REL - (CHUNK - 1):2 * MAX_REL + 1],
         jnp.broadcast_to(tab[:, 2 * MAX_REL:], (n_heads, span - n_var))], axis=1)
    b_max = jnp.max(base, axis=1, keepdims=True)
    b_min = jnp.min(base, axis=1, keepdims=True)
    base = base - 0.5 * (b_max + b_min)
    half_range = jnp.max((0.5 * (b_max - b_min)).reshape(n_heads // 2, 2), axis=1, keepdims=True)
    room = SCORE_LIMIT - half_range
    limit2 = jnp.broadcast_to(jnp.where(room > 0, room * room, -1.0), (n_heads // 2, LANES))
    rev = base[:, ::-1]
    padded = jnp.pad(rev, ((0, 0), (0, 1)))
    rows = jnp.tile(padded, (1, CHUNK))[:, :CHUNK * span].reshape(n_heads, CHUNK, span)
    band = rows[:, :, CHUNK - 1:CHUNK - 1 + BAND]
    ninf = jnp.full((n_heads, CHUNK, CHUNK), -jnp.inf, F32)
    bias = jnp.concatenate([jnp.concatenate([band, ninf], axis=2),
                            jnp.concatenate([ninf, band], axis=2)], axis=1)
    return bias, limit2


def _mlstm_params(w_in, b_gate, g_head):
    w_q = w_in[:, 0:A_QK]
    w_k = w_in[:, A_QK:2 * A_QK]
    w_vo = w_in[:, 2 * A_QK:A_GATE_OFF]
    w_g = w_in[:, A_GATE_OFF:]
    w_fm = jnp.concatenate([w_q, w_vo, w_g], axis=1).T.astype(BF16)
    b_g = jnp.broadcast_to(b_gate.astype(F32)[:, None], (2 * A_HEADS, A_ROWS))
    g_head_b = jnp.broadcast_to(g_head.astype(F32)[:, None], (A_V, LANES))
    return w_fm, w_k.astype(BF16), b_g, g_head_b


def kernel(x, a_w_in, a_b_gate, a_g_head, a_w_out, b_w_q, b_rel_bias, b_w_o, kv_norm_g, w_kv,
           norm_mix_g, norm_ffn_g, ffn_w_gate, ffn_w_up, ffn_w_down, final_norm_g):
    bsz, seq, d = x.shape
    assert d == D_MODEL and seq % max(ROW_TILE, A_ROWS, KV_ROWS) == 0
    depth = norm_mix_g.shape[0]
    n_a = a_w_in.shape[0]

    def row(v):
        return v.reshape(1, -1).astype(F32)

    h = x
    k_t = v_sh = k_norm2 = None
    for l in range(depth):
        if l == n_a:
            k_t, v_sh, k_norm2 = _kv_proj(h, row(kv_norm_g), w_kv[:, :D_MODEL].T.astype(BF16),
                                          w_kv[:, D_MODEL:].astype(BF16))
            k_norm2 = jnp.max(k_norm2, axis=1)
        if l < n_a:
            h = _mlstm_layer(h, row(norm_mix_g[l]), *_mlstm_params(a_w_in[l], a_b_gate[l], a_g_head[l]),
                             a_w_out[l].astype(BF16))
        else:
            j = l - n_a
            bias, limit2 = _rel_bias(b_rel_bias[j])
            h = _attn_layer(h, row(norm_mix_g[l]), b_w_q[j].astype(BF16), b_w_o[j].astype(BF16),
                            bias, k_norm2, limit2, k_t, v_sh)
        h = _ffn(h.reshape(bsz * seq, d), row(norm_ffn_g[l]), ffn_w_gate[l].astype(BF16),
                 ffn_w_up[l].astype(BF16), ffn_w_down[l].astype(BF16), row(final_norm_g),
                 final_norm=(l == depth - 1)).reshape(bsz, seq, d)
    return h
```

```python
import functools

import jax
import jax.numpy as jnp
from jax import lax
from jax.experimental import pallas as pl
from jax.experimental.pallas import tpu as pltpu

F32 = jnp.float32
BF16 = jnp.bfloat16

D_MODEL = 1024
CHUNK = 64
A_HEADS = 8
A_DQK = 64
A_DV = 128
A_QK = A_HEADS * A_DQK
A_V = A_HEADS * A_DV
B_HEADS = 16
B_DH = 64
LEFT_CHUNKS = 8
LEFT = LEFT_CHUNKS * CHUNK
BAND = LEFT + CHUNK
MAX_REL = 256
D_FF = 2816
EPS = 1e-6
LOG2E = 1.4426950408889634
SCORE_LIMIT = 96.0
NEG_INIT = -1e30

LANES = 128
MXU_COLS = 256

ROW_TILE = 512
FFN_ROWS = 1024
A_ROWS = 1024
KV_ROWS = 1024
A_SUPER = 128
FF_TILE = 256
VMEM_LIMIT = 56 * 1024 * 1024

A_GATE_OFF = 2 * A_QK + 2 * A_V


def _rmsnorm(x, g):
    return x * lax.rsqrt(jnp.mean(x * x, axis=-1, keepdims=True) + EPS) * g


def _dot(a, b):
    return jnp.dot(a, b, preferred_element_type=F32)


def _dot_nt(a, b):
    return lax.dot_general(a, b, (((1,), (1,)), ((), ())), preferred_element_type=F32)


def _dot_tn(a, b):
    return lax.dot_general(a, b, (((0,), (0,)), ((), ())), preferred_element_type=F32)


def _const_spec(shape):
    nd = len(shape)
    return pl.BlockSpec(shape, lambda *_: (0,) * nd, pipeline_mode=pl.Buffered(1))


def _ffn_kernel(x_ref, g_ref, wg_ref, wu_ref, wd_ref, gf_ref, o_ref, act_s, *, final_norm):
    x = x_ref[...]
    xn = _rmsnorm(x, g_ref[...]).astype(BF16)
    for j in range(D_FF // FF_TILE):
        cols = slice(j * FF_TILE, (j + 1) * FF_TILE)
        hg = _dot(xn, wg_ref[:, cols])
        hu = _dot(xn, wu_ref[:, cols])
        act_s[:, cols] = ((hg * jax.nn.sigmoid(hg)) * hu).astype(BF16)
    out = x + _dot(act_s[...], wd_ref[...])
    o_ref[...] = _rmsnorm(out, gf_ref[...]) if final_norm else out


def _ffn(h2d, g, wg, wu, wd, gf, final_norm):
    n_tok = h2d.shape[0]
    row = pl.BlockSpec((FFN_ROWS, D_MODEL), lambda i: (i, 0))
    return pl.pallas_call(
        functools.partial(_ffn_kernel, final_norm=final_norm),
        grid=(n_tok // FFN_ROWS,),
        in_specs=[row, _const_spec((1, D_MODEL)), _const_spec((D_MODEL, D_FF)),
                  _const_spec((D_MODEL, D_FF)), _const_spec((D_FF, D_MODEL)),
                  _const_spec((1, D_MODEL))],
        out_specs=row,
        out_shape=jax.ShapeDtypeStruct(h2d.shape, F32),
        scratch_shapes=[pltpu.VMEM((FFN_ROWS, D_FF), BF16)],
        compiler_params=pltpu.CompilerParams(
            dimension_semantics=("arbitrary",), vmem_limit_bytes=VMEM_LIMIT),
        name="ffn",
    )(h2d, g, wg, wu, wd, gf)


A_CEXT = A_DV + 16
A_STEPS = A_ROWS // A_SUPER


def _log_sigmoid(x):
    return jnp.minimum(x, 0.0) - jnp.log1p(jnp.exp(-jnp.abs(x)))


def _mlstm_kernel(x_ref, g_ref, wfm_ref, wk_ref, bg_ref, ghb_ref, wout_ref, o_ref,
                  qt_s, k_s, vt_s, ogt_s, hst_s, imb_s, bt_s, imbt_s, cmxt_s, c_s, m_s):
    t = pl.program_id(1)

    @pl.when(t == 0)
    def _():
        c_s[...] = jnp.zeros_like(c_s)
        m_s[...] = jnp.full_like(m_s, NEG_INIT)

    x = x_ref[0]
    xn = _rmsnorm(x, g_ref[...]).astype(BF16)
    gates = _dot_nt(wfm_ref[A_QK + 2 * A_V:A_QK + 2 * A_V + 2 * A_HEADS, :], xn) + bg_ref[...]
    i_log = gates[0:A_HEADS]
    f_log = _log_sigmoid(gates[A_HEADS:2 * A_HEADS])

    pos = lax.broadcasted_iota(jnp.int32, (A_HEADS, A_ROWS), 1) & (A_SUPER - 1)
    b_cum = f_log
    shift = 1
    while shift < A_SUPER:
        b_cum = b_cum + jnp.where(pos >= shift, pltpu.roll(b_cum, shift, axis=1), 0.0)
        shift *= 2
    imb = i_log - b_cum
    cmx = imb
    shift = 1
    while shift < A_SUPER:
        cmx = jnp.where(pos >= shift, jnp.maximum(cmx, pltpu.roll(cmx, shift, axis=1)), cmx)
        shift *= 2
    bt_s[...] = b_cum
    imbt_s[...] = imb
    cmxt_s[...] = cmx
    pad = jnp.zeros((LANES - A_HEADS, A_ROWS), F32)
    imb_s[...] = jnp.concatenate([imb * LOG2E, pad], axis=0).T

    qt_s[...] = _dot_nt(wfm_ref[0:A_QK, :], xn).astype(BF16)
    vt_s[...] = _dot_nt(wfm_ref[A_QK:A_QK + A_V, :], xn).astype(BF16)
    head_gain = jnp.concatenate([ghb_ref[...]] * (A_ROWS // LANES), axis=1)
    ogt_s[...] = jax.nn.sigmoid(_dot_nt(wfm_ref[A_QK + A_V:A_QK + 2 * A_V, :], xn)) * head_gain
    k_s[...] = (_dot(xn, wk_ref[...]) * (A_DQK ** -0.5)).astype(BF16)

    row_i = lax.broadcasted_iota(jnp.int32, (A_SUPER, A_SUPER), 0)
    col_i = lax.broadcasted_iota(jnp.int32, (A_SUPER, A_SUPER), 1)
    causal = col_i >= row_i
    first_rows = row_i < A_DQK
    first_lanes = col_i < A_DQK
    ext_row = lax.broadcasted_iota(jnp.int32, (A_CEXT - A_DV, LANES), 0) == 0
    lane_row = lax.broadcasted_iota(jnp.int32, (1, LANES), 1) < A_DQK

    def pair_queries(p, r):
        qp = qt_s[p * LANES:(p + 1) * LANES, r]
        zero = jnp.zeros_like(qp)
        return jnp.concatenate([jnp.where(first_rows, qp, zero), jnp.where(first_rows, zero, qp)], axis=1)

    def pair_step(c, p):
        r = slice(c * A_SUPER, (c + 1) * A_SUPER)
        q_pair = pair_queries(p, r)
        kp = k_s[r, p * LANES:(p + 1) * LANES]
        st_pair = _dot(kp, q_pair)
        qc_pair = _dot(c_s[p].astype(BF16), q_pair)
        lhs, scale = [], []
        for half in range(2):
            h = 2 * p + half
            hr = slice(h * A_DV, (h + 1) * A_DV)
            vt_h = vt_s[hr, r]
            b_row = bt_s[h:h + 1, r]
            imb_row = imbt_s[h:h + 1, r]
            imb2_col = imb_s[r, h:h + 1]
            cmx_row = cmxt_s[h:h + 1, r]
            b_last = jnp.broadcast_to(b_row[:, A_SUPER - 1:A_SUPER], (1, LANES))
            m_prev = m_s[h]

            mrow = b_row + cmx_row
            d_rel = jnp.exp2(jnp.where(causal, imb2_col - cmx_row * LOG2E, -jnp.inf))
            st = st_pair[:, half * A_SUPER:(half + 1) * A_SUPER] * d_rel
            svt = _dot(vt_h, st.astype(BF16))
            rs = jnp.sum(st, axis=0, keepdims=True)

            inter = b_row + m_prev
            m_j = jnp.maximum(inter, mrow)
            f_intra = jnp.exp(mrow - m_j)
            w_inter = jnp.exp(inter - m_j)
            qc = qc_pair[:, half * A_SUPER:(half + 1) * A_SUPER]
            num = f_intra * svt + w_inter * qc[0:A_DV]
            den = f_intra * rs + w_inter * qc[A_DV:A_DV + 1]
            inv = 1.0 / jnp.maximum(jnp.abs(den), jnp.exp(-m_j))
            inv_rms = lax.rsqrt(inv * inv * jnp.mean(num * num, axis=0, keepdims=True) + EPS)
            hst_s[hr, r] = (ogt_s[hr, r] * (num * (inv * inv_rms))).astype(BF16)

            a_row = b_last + imb_row
            amax = jnp.broadcast_to(jnp.max(a_row, axis=1, keepdims=True), (1, LANES))
            wa = jnp.exp(a_row - amax)
            vw = (vt_h.astype(F32) * wa).astype(BF16)
            ext = jnp.where(ext_row, jnp.broadcast_to(wa, (A_CEXT - A_DV, LANES)), 0.0).astype(BF16)
            lhs.append(jnp.concatenate([vw, ext], axis=0))

            m_new = jnp.maximum(b_last + m_prev, amax)
            scale.append((jnp.exp(b_last + m_prev - m_new), jnp.exp(amax - m_new)))
            m_s[h] = m_new
        k_zero = jnp.zeros_like(kp)
        k_split = jnp.concatenate([jnp.where(first_lanes, kp, k_zero),
                                   jnp.where(first_lanes, k_zero, kp)], axis=0)
        up = _dot(jnp.concatenate(lhs, axis=1), k_split)
        decay = jnp.where(lane_row, scale[0][0], scale[1][0])
        grow = jnp.where(lane_row, scale[0][1], scale[1][1])
        c_s[p] = decay * c_s[p] + grow * up

    @pl.when(t >= 0)
    def _():
        for c in range(A_STEPS):
            for p in range(A_HEADS // 2):
                pair_step(c, p)

    o_ref[0] = x + _dot_tn(hst_s[...], wout_ref[...])


def _mlstm_layer(h, g, w_fm, w_k, b_g, g_head_b, w_out):
    bsz, seq, _ = h.shape
    row = pl.BlockSpec((1, A_ROWS, D_MODEL), lambda b, t: (b, t, 0))
    return pl.pallas_call(
        _mlstm_kernel,
        grid=(bsz, seq // A_ROWS),
        in_specs=[row, _const_spec((1, D_MODEL)), _const_spec((A_QK + 2 * A_V + 2 * A_HEADS, D_MODEL)),
                  _const_spec((D_MODEL, A_QK)), _const_spec((2 * A_HEADS, A_ROWS)),
                  _const_spec((A_V, LANES)), _const_spec((A_V, D_MODEL))],
        out_specs=row,
        out_shape=jax.ShapeDtypeStruct(h.shape, F32),
        scratch_shapes=[
            pltpu.VMEM((A_QK, A_ROWS), BF16),
            pltpu.VMEM((A_ROWS, A_QK), BF16),
            pltpu.VMEM((A_V, A_ROWS), BF16),
            pltpu.VMEM((A_V, A_ROWS), F32),
            pltpu.VMEM((A_V, A_ROWS), BF16),
            pltpu.VMEM((A_ROWS, LANES), F32),
            pltpu.VMEM((A_HEADS, A_ROWS), F32),
            pltpu.VMEM((A_HEADS, A_ROWS), F32),
            pltpu.VMEM((A_HEADS, A_ROWS), F32),
            pltpu.VMEM((A_HEADS // 2, A_CEXT, LANES), F32),
            pltpu.VMEM((A_HEADS, 1, LANES), F32),
        ],
        compiler_params=pltpu.CompilerParams(
            dimension_semantics=("arbitrary", "arbitrary"), vmem_limit_bytes=VMEM_LIMIT),
        name="mlstm",
    )(h, g, w_fm, w_k, b_g, g_head_b, w_out)


def _kv_kernel(x_ref, g_ref, wkt_ref, wv_ref, kt_ref, v_ref, kn_ref):
    xn = _rmsnorm(x_ref[0], g_ref[...]).astype(BF16)
    kt = _dot_nt(wkt_ref[...], xn).astype(BF16)
    kt_ref[0] = kt
    v_ref[0] = _dot(xn, wv_ref[...]).astype(BF16)
    kf = kt.astype(F32)
    sq = kf * kf
    for p in range(B_HEADS // 2):
        norm2 = jnp.sum(sq[p * LANES:(p + 1) * LANES], axis=0, keepdims=True)
        kn_ref[0, 0, p:p + 1, :] = jnp.broadcast_to(jnp.max(norm2, axis=1, keepdims=True), (1, LANES))


def _kv_proj(h, g, w_kt, w_v):
    bsz, seq, _ = h.shape
    row = pl.BlockSpec((1, KV_ROWS, D_MODEL), lambda b, t: (b, t, 0))
    col = pl.BlockSpec((1, D_MODEL, KV_ROWS), lambda b, t: (b, 0, t))
    return pl.pallas_call(
        _kv_kernel,
        grid=(bsz, seq // KV_ROWS),
        in_specs=[row, _const_spec((1, D_MODEL)), _const_spec((D_MODEL, D_MODEL)),
                  _const_spec((D_MODEL, D_MODEL))],
        out_specs=[col, row, pl.BlockSpec((1, 1, B_HEADS // 2, LANES), lambda b, t: (b, t, 0, 0))],
        out_shape=[jax.ShapeDtypeStruct((bsz, D_MODEL, seq), BF16),
                   jax.ShapeDtypeStruct((bsz, seq, D_MODEL), BF16),
                   jax.ShapeDtypeStruct((bsz, seq // KV_ROWS, B_HEADS // 2, LANES), F32)],
        compiler_params=pltpu.CompilerParams(
            dimension_semantics=("arbitrary", "arbitrary"), vmem_limit_bytes=VMEM_LIMIT),
        name="kv_proj",
    )(h, g, w_kt, w_v)


Q_BLOCK = 2 * CHUNK
K_BLOCK = LEFT + Q_BLOCK


def _attn_kernel(x_ref, g_ref, wq_ref, wo_ref, bias_ref, kn_ref, lim_ref,
                 ktp_ref, ktc_ref, vp_ref, vc_ref, o_ref,
                 q_s, a_s, s_s, p_s, m_s, l_s):
    t = pl.program_id(1)
    x = x_ref[0]
    xn = _rmsnorm(x, g_ref[...]).astype(BF16)
    q = _dot(xn, wq_ref[...]) * (B_DH ** -0.5 * LOG2E)
    q_s[...] = q.astype(BF16)

    excess = None
    for p in range(B_HEADS // 2):
        qp = q[:, p * LANES:(p + 1) * LANES]
        q_norm2 = jnp.max(jnp.sum(qp * qp, axis=1, keepdims=True))
        e = q_norm2 * kn_ref[0, p:p + 1, :] - lim_ref[p:p + 1, :]
        excess = e if excess is None else jnp.maximum(excess, e)
    bounded = jnp.max(excess) <= 0.0

    lane = lax.broadcasted_iota(jnp.int32, (Q_BLOCK, LANES), 1)
    lo = lane < B_DH
    n_tiles = K_BLOCK // LANES
    n_blocks = ROW_TILE // Q_BLOCK
    n_pairs = B_HEADS // 2
    left_tiles = LEFT // LANES

    def band(i, first_tile):
        first = left_tiles - i if first_tile else 0
        prev = None if first_tile else slice(i * Q_BLOCK, ROW_TILE)
        return first, (prev, slice(0, (i + 1) * Q_BLOCK)), slice(first * LANES, K_BLOCK)

    def band_keys(p, keys):
        pl_ = slice(p * LANES, (p + 1) * LANES)
        prev, cur = keys
        if prev is None:
            return ktc_ref[0, pl_, cur]
        return jnp.concatenate([ktp_ref[0, pl_, prev], ktc_ref[0, pl_, cur]], axis=1)

    def band_values(p, keys):
        pl_ = slice(p * LANES, (p + 1) * LANES)
        prev, cur = keys
        if prev is None:
            return vc_ref[0, cur, pl_]
        return jnp.concatenate([vp_ref[0, prev, pl_], vc_ref[0, cur, pl_]], axis=0)

    def scores(i, p, first_tile):
        pl_ = slice(p * LANES, (p + 1) * LANES)
        _, keys, cols = band(i, first_tile)
        qp = q_s[i * Q_BLOCK:(i + 1) * Q_BLOCK, pl_]
        kp = band_keys(p, keys)
        for half in range(2):
            h = 2 * p + half
            own = lo if half == 0 else jnp.logical_not(lo)
            s = _dot(jnp.where(own, qp, jnp.zeros_like(qp)), kp) + bias_ref[h, :, cols]
            s_s[h, :, cols] = s
            m_s[h] = jnp.broadcast_to(jnp.max(s, axis=1, keepdims=True), (Q_BLOCK, LANES))

    def scores_bounded(i, p, first_tile):
        pl_ = slice(p * LANES, (p + 1) * LANES)
        first, keys, cols = band(i, first_tile)
        qp = q_s[i * Q_BLOCK:(i + 1) * Q_BLOCK, pl_]
        kp = band_keys(p, keys)
        for half in range(2):
            h = 2 * p + half
            own = lo if half == 0 else jnp.logical_not(lo)
            s = _dot(jnp.where(own, qp, jnp.zeros_like(qp)), kp) + bias_ref[h, :, cols]
            acc = jnp.zeros((Q_BLOCK, LANES), F32)
            for j in range(n_tiles - first):
                e = jnp.exp2(s[:, j * LANES:(j + 1) * LANES])
                acc = acc + e
                p_s[h, :, (first + j) * LANES:(first + j + 1) * LANES] = e.astype(BF16)
            l_s[h] = jnp.broadcast_to(jnp.sum(acc, axis=1, keepdims=True), (Q_BLOCK, LANES))

    def exps(i, p, first_tile):
        first, _, _ = band(i, first_tile)
        for h in (2 * p, 2 * p + 1):
            m = m_s[h]
            acc = jnp.zeros((Q_BLOCK, LANES), F32)
            for j in range(first, n_tiles):
                tl = slice(j * LANES, (j + 1) * LANES)
                e = jnp.exp2(s_s[h, :, tl] - m)
                acc = acc + e
                p_s[h, :, tl] = e.astype(BF16)
            l_s[h] = jnp.broadcast_to(jnp.sum(acc, axis=1, keepdims=True), (Q_BLOCK, LANES))

    def values(i, p, first_tile):
        pl_ = slice(p * LANES, (p + 1) * LANES)
        _, keys, cols = band(i, first_tile)
        vp = band_values(p, keys)
        o0 = _dot(p_s[2 * p, :, cols], vp) / l_s[2 * p]
        o1 = _dot(p_s[2 * p + 1, :, cols], vp) / l_s[2 * p + 1]
        a_s[i * Q_BLOCK:(i + 1) * Q_BLOCK, pl_] = jnp.where(lo, o0, o1).astype(BF16)

    def block(i, first_tile, stages):
        for stage in stages:
            for p in range(n_pairs):
                stage(i, p, first_tile)

    first = t == 0
    for i in range(n_blocks):
        for stages, use in (((scores_bounded, values), bounded),
                            ((scores, exps, values), jnp.logical_not(bounded))):
            pl.when(jnp.logical_and(use, first))(functools.partial(block, i, True, stages))
            pl.when(jnp.logical_and(use, jnp.logical_not(first)))(functools.partial(block, i, False, stages))

    o_ref[0] = x + _dot(a_s[...], wo_ref[...])


def _attn_layer(h, g, w_q, w_o, bias, k_norm2, limit2, k_t, v):
    bsz, seq, _ = h.shape
    assert LEFT == ROW_TILE
    row = pl.BlockSpec((1, ROW_TILE, D_MODEL), lambda b, t: (b, t, 0))
    kt_prev = pl.BlockSpec((1, D_MODEL, ROW_TILE), lambda b, t: (b, 0, jnp.maximum(t - 1, 0)))
    kt_cur = pl.BlockSpec((1, D_MODEL, ROW_TILE), lambda b, t: (b, 0, t))
    v_prev = pl.BlockSpec((1, ROW_TILE, D_MODEL), lambda b, t: (b, jnp.maximum(t - 1, 0), 0))
    return pl.pallas_call(
        _attn_kernel,
        grid=(bsz, seq // ROW_TILE),
        in_specs=[row, _const_spec((1, D_MODEL)), _const_spec((D_MODEL, D_MODEL)),
                  _const_spec((D_MODEL, D_MODEL)), _const_spec((B_HEADS, Q_BLOCK, K_BLOCK)),
                  pl.BlockSpec((1, B_HEADS // 2, LANES), lambda b, t: (b, 0, 0)),
                  _const_spec((B_HEADS // 2, LANES)), kt_prev, kt_cur, v_prev, row],
        out_specs=row,
        out_shape=jax.ShapeDtypeStruct(h.shape, F32),
        scratch_shapes=[pltpu.VMEM((ROW_TILE, D_MODEL), BF16),
                        pltpu.VMEM((ROW_TILE, D_MODEL), BF16),
                        pltpu.VMEM((B_HEADS, Q_BLOCK, K_BLOCK), F32),
                        pltpu.VMEM((B_HEADS, Q_BLOCK, K_BLOCK), BF16),
                        pltpu.VMEM((B_HEADS, Q_BLOCK, LANES), F32),
                        pltpu.VMEM((B_HEADS, Q_BLOCK, LANES), F32)],
        compiler_params=pltpu.CompilerParams(
            dimension_semantics=("arbitrary", "arbitrary"), vmem_limit_bytes=VMEM_LIMIT),
        name="attn",
    )(h, g, w_q, w_o, bias, k_norm2, limit2, k_t, k_t, v, v)


def _rel_bias(rel_table):
    n_heads = rel_table.shape[0]
    tab = rel_table.astype(F32) * LOG2E
    span = BAND + CHUNK - 1
    n_var = MAX_REL + CHUNK
    base = jnp.concatenate(
        [tab[:, MAX_REL - (CHUNK - 1):2 * MAX_REL + 1],
         jnp.broadcast_to(tab[:, 2 * MAX_REL:], (n_heads, span - n_var))], axis=1)
    b_max = jnp.max(base, axis=1, keepdims=True)
    b_min = jnp.min(base, axis=1, keepdims=True)
    base = base - 0.5 * (b_max + b_min)
    half_range = jnp.max((0.5 * (b_max - b_min)).reshape(n_heads // 2, 2), axis=1, keepdims=True)
    room = SCORE_LIMIT - half_range
    limit2 = jnp.broadcast_to(jnp.where(room > 0, room * room, -1.0), (n_heads // 2, LANES))
    rev = base[:, ::-1]
    padded = jnp.pad(rev, ((0, 0), (0, 1)))
    rows = jnp.tile(padded, (1, CHUNK))[:, :CHUNK * span].reshape(n_heads, CHUNK, span)
    band = rows[:, :, CHUNK - 1:CHUNK - 1 + BAND]
    ninf = jnp.full((n_heads, CHUNK, CHUNK), -jnp.inf, F32)
    bias = jnp.concatenate([jnp.concatenate([band, ninf], axis=2),
                            jnp.concatenate([ninf, band], axis=2)], axis=1)
    return bias, limit2


def _mlstm_params(w_in, b_gate, g_head):
    w_q = w_in[:, 0:A_QK]
    w_k = w_in[:, A_QK:2 * A_QK]
    w_vo = w_in[:, 2 * A_QK:A_GATE_OFF]
    w_g = w_in[:, A_GATE_OFF:]
    w_fm = jnp.concatenate([w_q, w_vo, w_g], axis=1).T.astype(BF16)
    b_g = jnp.broadcast_to(b_gate.astype(F32)[:, None], (2 * A_HEADS, A_ROWS))
    g_head_b = jnp.broadcast_to(g_head.astype(F32)[:, None], (A_V, LANES))
    return w_fm, w_k.astype(BF16), b_g, g_head_b


def kernel(x, a_w_in, a_b_gate, a_g_head, a_w_out, b_w_q, b_rel_bias, b_w_o, kv_norm_g, w_kv,
           norm_mix_g, norm_ffn_g, ffn_w_gate, ffn_w_up, ffn_w_down, final_norm_g):
    bsz, seq, d = x.shape
    assert d == D_MODEL and seq % max(ROW_TILE, A_ROWS, KV_ROWS) == 0
    depth = norm_mix_g.shape[0]
    n_a = a_w_in.shape[0]

    def row(v):
        return v.reshape(1, -1).astype(F32)

    h = x
    k_t = v_sh = k_norm2 = None
    for l in range(depth):
        if l == n_a:
            k_t, v_sh, k_norm2 = _kv_proj(h, row(kv_norm_g), w_kv[:, :D_MODEL].T.astype(BF16),
                                          w_kv[:, D_MODEL:].astype(BF16))
            k_norm2 = jnp.max(k_norm2, axis=1)
        if l < n_a:
            h = _mlstm_layer(h, row(norm_mix_g[l]), *_mlstm_params(a_w_in[l], a_b_gate[l], a_g_head[l]),
                             a_w_out[l].astype(BF16))
        else:
            j = l - n_a
            bias, limit2 = _rel_bias(b_rel_bias[j])
            h = _attn_layer(h, row(norm_mix_g[l]), b_w_q[j].astype(BF16), b_w_o[j].astype(BF16),
                            bias, k_norm2, limit2, k_t, v_sh)
        h = _ffn(h.reshape(bsz * seq, d), row(norm_ffn_g[l]), ffn_w_gate[l].astype(BF16),
                 ffn_w_up[l].astype(BF16), ffn_w_down[l].astype(BF16), row(final_norm_g),
                 final_norm=(l == depth - 1)).reshape(bsz, seq, d)
    return h
```

```python
import functools

import jax
import jax.numpy as jnp
from jax import lax
from jax.experimental import pallas as pl
from jax.experimental.pallas import tpu as pltpu

F32 = jnp.float32
BF16 = jnp.bfloat16

D_MODEL = 1024
CHUNK = 64
A_HEADS = 8
A_DQK = 64
A_DV = 128
A_QK = A_HEADS * A_DQK
A_V = A_HEADS * A_DV
B_HEADS = 16
B_DH = 64
LEFT_CHUNKS = 8
LEFT = LEFT_CHUNKS * CHUNK
BAND = LEFT + CHUNK
MAX_REL = 256
D_FF = 2816
EPS = 1e-6
LOG2E = 1.4426950408889634
SCORE_LIMIT = 96.0
NEG_INIT = -1e30

LANES = 128
BF16_SUBLANES = 16
VMEM_BYTES = 64 * 1024 * 1024

ROW_TILE = 512
FFN_ROWS = 1024
A_ROWS = 1024
KV_ROWS = 1024
A_SUPER = 128
FF_TILE = 256
VMEM_LIMIT = VMEM_BYTES * 7 // 8

A_GATE_OFF = 2 * A_QK + 2 * A_V


def _rmsnorm(x, g):
    return x * lax.rsqrt(jnp.mean(x * x, axis=-1, keepdims=True) + EPS) * g


def _dot(a, b):
    return jnp.dot(a, b, preferred_element_type=F32)


def _dot_nt(a, b):
    return lax.dot_general(a, b, (((1,), (1,)), ((), ())), preferred_element_type=F32)


def _dot_tn(a, b):
    return lax.dot_general(a, b, (((0,), (0,)), ((), ())), preferred_element_type=F32)


def _const_spec(shape):
    nd = len(shape)
    return pl.BlockSpec(shape, lambda *_: (0,) * nd, pipeline_mode=pl.Buffered(1))


def _ffn_kernel(x_ref, g_ref, wg_ref, wu_ref, wd_ref, gf_ref, o_ref, act_s, *, final_norm):
    x = x_ref[...]
    xn = _rmsnorm(x, g_ref[...]).astype(BF16)
    for j in range(D_FF // FF_TILE):
        cols = slice(j * FF_TILE, (j + 1) * FF_TILE)
        hg = _dot(xn, wg_ref[:, cols])
        hu = _dot(xn, wu_ref[:, cols])
        act_s[:, cols] = ((hg * jax.nn.sigmoid(hg)) * hu).astype(BF16)
    out = x + _dot(act_s[...], wd_ref[...])
    o_ref[...] = _rmsnorm(out, gf_ref[...]) if final_norm else out


def _ffn(h2d, g, wg, wu, wd, gf, final_norm):
    n_tok = h2d.shape[0]
    row = pl.BlockSpec((FFN_ROWS, D_MODEL), lambda i: (i, 0))
    return pl.pallas_call(
        functools.partial(_ffn_kernel, final_norm=final_norm),
        grid=(n_tok // FFN_ROWS,),
        in_specs=[row, _const_spec((1, D_MODEL)), _const_spec((D_MODEL, D_FF)),
                  _const_spec((D_MODEL, D_FF)), _const_spec((D_FF, D_MODEL)),
                  _const_spec((1, D_MODEL))],
        out_specs=row,
        out_shape=jax.ShapeDtypeStruct(h2d.shape, F32),
        scratch_shapes=[pltpu.VMEM((FFN_ROWS, D_FF), BF16)],
        compiler_params=pltpu.CompilerParams(
            dimension_semantics=("arbitrary",), vmem_limit_bytes=VMEM_LIMIT),
        name="ffn",
    )(h2d, g, wg, wu, wd, gf)


A_CEXT = A_DV + BF16_SUBLANES
A_STEPS = A_ROWS // A_SUPER


def _log_sigmoid(x):
    return jnp.minimum(x, 0.0) - jnp.log1p(jnp.exp(-jnp.abs(x)))


def _mlstm_kernel(x_ref, g_ref, wfm_ref, wk_ref, bg_ref, ghb_ref, wout_ref, o_ref,
                  qt_s, k_s, vt_s, ogt_s, hst_s, imb_s, bt_s, imbt_s, cmxt_s, c_s, m_s):
    t = pl.program_id(1)

    @pl.when(t == 0)
    def _():
        c_s[...] = jnp.zeros_like(c_s)
        m_s[...] = jnp.full_like(m_s, NEG_INIT)

    x = x_ref[0]
    xn = _rmsnorm(x, g_ref[...]).astype(BF16)
    gates = _dot_nt(wfm_ref[A_QK + 2 * A_V:A_QK + 2 * A_V + 2 * A_HEADS, :], xn) + bg_ref[...]
    i_log = gates[0:A_HEADS]
    f_log = _log_sigmoid(gates[A_HEADS:2 * A_HEADS])

    pos = lax.broadcasted_iota(jnp.int32, (A_HEADS, A_ROWS), 1) & (A_SUPER - 1)
    b_cum = f_log
    shift = 1
    while shift < A_SUPER:
        b_cum = b_cum + jnp.where(pos >= shift, pltpu.roll(b_cum, shift, axis=1), 0.0)
        shift *= 2
    imb = i_log - b_cum
    cmx = imb
    shift = 1
    while shift < A_SUPER:
        cmx = jnp.where(pos >= shift, jnp.maximum(cmx, pltpu.roll(cmx, shift, axis=1)), cmx)
        shift *= 2
    bt_s[...] = b_cum
    imbt_s[...] = imb
    cmxt_s[...] = cmx
    pad = jnp.zeros((LANES - A_HEADS, A_ROWS), F32)
    imb_s[...] = jnp.concatenate([imb * LOG2E, pad], axis=0).T

    qt_s[...] = _dot_nt(wfm_ref[0:A_QK, :], xn).astype(BF16)
    vt_s[...] = _dot_nt(wfm_ref[A_QK:A_QK + A_V, :], xn).astype(BF16)
    head_gain = jnp.concatenate([ghb_ref[...]] * (A_ROWS // LANES), axis=1)
    ogt_s[...] = jax.nn.sigmoid(_dot_nt(wfm_ref[A_QK + A_V:A_QK + 2 * A_V, :], xn)) * head_gain
    k_s[...] = (_dot(xn, wk_ref[...]) * (A_DQK ** -0.5)).astype(BF16)

    row_i = lax.broadcasted_iota(jnp.int32, (A_SUPER, A_SUPER), 0)
    col_i = lax.broadcasted_iota(jnp.int32, (A_SUPER, A_SUPER), 1)
    causal = col_i >= row_i
    first_rows = row_i < A_DQK
    first_lanes = col_i < A_DQK
    ext_row = lax.broadcasted_iota(jnp.int32, (A_CEXT - A_DV, LANES), 0) == 0
    lane_row = lax.broadcasted_iota(jnp.int32, (1, LANES), 1) < A_DQK

    def pair_queries(p, r):
        qp = qt_s[p * LANES:(p + 1) * LANES, r]
        zero = jnp.zeros_like(qp)
        return jnp.concatenate([jnp.where(first_rows, qp, zero), jnp.where(first_rows, zero, qp)], axis=1)

    def pair_step(c, p):
        r = slice(c * A_SUPER, (c + 1) * A_SUPER)
        q_pair = pair_queries(p, r)
        kp = k_s[r, p * LANES:(p + 1) * LANES]
        st_pair = _dot(kp, q_pair)
        qc_pair = _dot(c_s[p].astype(BF16), q_pair)
        lhs, scale = [], []
        for half in range(2):
            h = 2 * p + half
            hr = slice(h * A_DV, (h + 1) * A_DV)
            vt_h = vt_s[hr, r]
            b_row = bt_s[h:h + 1, r]
            imb_row = imbt_s[h:h + 1, r]
            imb2_col = imb_s[r, h:h + 1]
            cmx_row = cmxt_s[h:h + 1, r]
            b_last = jnp.broadcast_to(b_row[:, A_SUPER - 1:A_SUPER], (1, LANES))
            m_prev = m_s[h]

            mrow = b_row + cmx_row
            d_rel = jnp.exp2(jnp.where(causal, imb2_col - cmx_row * LOG2E, -jnp.inf))
            st = st_pair[:, half * A_SUPER:(half + 1) * A_SUPER] * d_rel
            svt = _dot(vt_h, st.astype(BF16))
            rs = jnp.sum(st, axis=0, keepdims=True)

            inter = b_row + m_prev
            m_j = jnp.maximum(inter, mrow)
            f_intra = jnp.exp(mrow - m_j)
            w_inter = jnp.exp(inter - m_j)
            qc = qc_pair[:, half * A_SUPER:(half + 1) * A_SUPER]
            num = f_intra * svt + w_inter * qc[0:A_DV]
            den = f_intra * rs + w_inter * qc[A_DV:A_DV + 1]
            inv = 1.0 / jnp.maximum(jnp.abs(den), jnp.exp(-m_j))
            inv_rms = lax.rsqrt(inv * inv * jnp.mean(num * num, axis=0, keepdims=True) + EPS)
            hst_s[hr, r] = (ogt_s[hr, r] * (num * (inv * inv_rms))).astype(BF16)

            a_row = b_last + imb_row
            amax = jnp.broadcast_to(jnp.max(a_row, axis=1, keepdims=True), (1, LANES))
            wa = jnp.exp(a_row - amax)
            vw = (vt_h.astype(F32) * wa).astype(BF16)
            ext = jnp.where(ext_row, jnp.broadcast_to(wa, (A_CEXT - A_DV, LANES)), 0.0).astype(BF16)
            lhs.append(jnp.concatenate([vw, ext], axis=0))

            m_new = jnp.maximum(b_last + m_prev, amax)
            scale.append((jnp.exp(b_last + m_prev - m_new), jnp.exp(amax - m_new)))
            m_s[h] = m_new
        k_zero = jnp.zeros_like(kp)
        k_split = jnp.concatenate([jnp.where(first_lanes, kp, k_zero),
                                   jnp.where(first_lanes, k_zero, kp)], axis=0)
        up = _dot(jnp.concatenate(lhs, axis=1), k_split)
        decay = jnp.where(lane_row, scale[0][0], scale[1][0])
        grow = jnp.where(lane_row, scale[0][1], scale[1][1])
        c_s[p] = decay * c_s[p] + grow * up

    @pl.when(t >= 0)
    def _():
        for c in range(A_STEPS):
            for p in range(A_HEADS // 2):
                pair_step(c, p)

    o_ref[0] = x + _dot_tn(hst_s[...], wout_ref[...])


def _mlstm_layer(h, g, w_fm, w_k, b_g, g_head_b, w_out):
    bsz, seq, _ = h.shape
    row = pl.BlockSpec((1, A_ROWS, D_MODEL), lambda b, t: (b, t, 0))
    return pl.pallas_call(
        _mlstm_kernel,
        grid=(bsz, seq // A_ROWS),
        in_specs=[row, _const_spec((1, D_MODEL)), _const_spec((A_QK + 2 * A_V + 2 * A_HEADS, D_MODEL)),
                  _const_spec((D_MODEL, A_QK)), _const_spec((2 * A_HEADS, A_ROWS)),
                  _const_spec((A_V, LANES)), _const_spec((A_V, D_MODEL))],
        out_specs=row,
        out_shape=jax.ShapeDtypeStruct(h.shape, F32),
        scratch_shapes=[
            pltpu.VMEM((A_QK, A_ROWS), BF16),
            pltpu.VMEM((A_ROWS, A_QK), BF16),
            pltpu.VMEM((A_V, A_ROWS), BF16),
            pltpu.VMEM((A_V, A_ROWS), F32),
            pltpu.VMEM((A_V, A_ROWS), BF16),
            pltpu.VMEM((A_ROWS, LANES), F32),
            pltpu.VMEM((A_HEADS, A_ROWS), F32),
            pltpu.VMEM((A_HEADS, A_ROWS), F32),
            pltpu.VMEM((A_HEADS, A_ROWS), F32),
            pltpu.VMEM((A_HEADS // 2, A_CEXT, LANES), F32),
            pltpu.VMEM((A_HEADS, 1, LANES), F32),
        ],
        compiler_params=pltpu.CompilerParams(
            dimension_semantics=("arbitrary", "arbitrary"), vmem_limit_bytes=VMEM_LIMIT),
        name="mlstm",
    )(h, g, w_fm, w_k, b_g, g_head_b, w_out)


def _kv_kernel(x_ref, g_ref, wkt_ref, wv_ref, kt_ref, v_ref, kn_ref):
    xn = _rmsnorm(x_ref[0], g_ref[...]).astype(BF16)
    kt = _dot_nt(wkt_ref[...], xn).astype(BF16)
    kt_ref[0] = kt
    v_ref[0] = _dot(xn, wv_ref[...]).astype(BF16)
    kf = kt.astype(F32)
    sq = kf * kf
    for p in range(B_HEADS // 2):
        norm2 = jnp.sum(sq[p * LANES:(p + 1) * LANES], axis=0, keepdims=True)
        kn_ref[0, 0, p:p + 1, :] = jnp.broadcast_to(jnp.max(norm2, axis=1, keepdims=True), (1, LANES))


def _kv_proj(h, g, w_kt, w_v):
    bsz, seq, _ = h.shape
    row = pl.BlockSpec((1, KV_ROWS, D_MODEL), lambda b, t: (b, t, 0))
    col = pl.BlockSpec((1, D_MODEL, KV_ROWS), lambda b, t: (b, 0, t))
    return pl.pallas_call(
        _kv_kernel,
        grid=(bsz, seq // KV_ROWS),
        in_specs=[row, _const_spec((1, D_MODEL)), _const_spec((D_MODEL, D_MODEL)),
                  _const_spec((D_MODEL, D_MODEL))],
        out_specs=[col, row, pl.BlockSpec((1, 1, B_HEADS // 2, LANES), lambda b, t: (b, t, 0, 0))],
        out_shape=[jax.ShapeDtypeStruct((bsz, D_MODEL, seq), BF16),
                   jax.ShapeDtypeStruct((bsz, seq, D_MODEL), BF16),
                   jax.ShapeDtypeStruct((bsz, seq // KV_ROWS, B_HEADS // 2, LANES), F32)],
        compiler_params=pltpu.CompilerParams(
            dimension_semantics=("arbitrary", "arbitrary"), vmem_limit_bytes=VMEM_LIMIT),
        name="kv_proj",
    )(h, g, w_kt, w_v)


Q_BLOCK = 2 * CHUNK
K_BLOCK = LEFT + Q_BLOCK


def _attn_kernel(x_ref, g_ref, wq_ref, wo_ref, bias_ref, kn_ref, lim_ref,
                 ktp_ref, ktc_ref, vp_ref, vc_ref, o_ref,
                 q_s, a_s, s_s, p_s, m_s, l_s):
    t = pl.program_id(1)
    x = x_ref[0]
    xn = _rmsnorm(x, g_ref[...]).astype(BF16)
    q = _dot(xn, wq_ref[...]) * (B_DH ** -0.5 * LOG2E)
    q_s[...] = q.astype(BF16)

    excess = None
    for p in range(B_HEADS // 2):
        qp = q[:, p * LANES:(p + 1) * LANES]
        q_norm2 = jnp.max(jnp.sum(qp * qp, axis=1, keepdims=True))
        e = q_norm2 * kn_ref[0, p:p + 1, :] - lim_ref[p:p + 1, :]
        excess = e if excess is None else jnp.maximum(excess, e)
    bounded = jnp.max(excess) <= 0.0

    lane = lax.broadcasted_iota(jnp.int32, (Q_BLOCK, LANES), 1)
    lo = lane < B_DH
    n_tiles = K_BLOCK // LANES
    n_blocks = ROW_TILE // Q_BLOCK
    n_pairs = B_HEADS // 2
    left_tiles = LEFT // LANES

    def band(i, first_tile):
        first = left_tiles - i if first_tile else 0
        prev = None if first_tile else slice(i * Q_BLOCK, ROW_TILE)
        return first, (prev, slice(0, (i + 1) * Q_BLOCK)), slice(first * LANES, K_BLOCK)

    def band_keys(p, keys):
        pl_ = slice(p * LANES, (p + 1) * LANES)
        prev, cur = keys
        if prev is None:
            return ktc_ref[0, pl_, cur]
        return jnp.concatenate([ktp_ref[0, pl_, prev], ktc_ref[0, pl_, cur]], axis=1)

    def band_values(p, keys):
        pl_ = slice(p * LANES, (p + 1) * LANES)
        prev, cur = keys
        if prev is None:
            return vc_ref[0, cur, pl_]
        return jnp.concatenate([vp_ref[0, prev, pl_], vc_ref[0, cur, pl_]], axis=0)

    def scores(i, p, first_tile):
        pl_ = slice(p * LANES, (p + 1) * LANES)
        _, keys, cols = band(i, first_tile)
        qp = q_s[i * Q_BLOCK:(i + 1) * Q_BLOCK, pl_]
        kp = band_keys(p, keys)
        for half in range(2):
            h = 2 * p + half
            own = lo if half == 0 else jnp.logical_not(lo)
            s = _dot(jnp.where(own, qp, jnp.zeros_like(qp)), kp) + bias_ref[h, :, cols]
            s_s[h, :, cols] = s
            m_s[h] = jnp.broadcast_to(jnp.max(s, axis=1, keepdims=True), (Q_BLOCK, LANES))

    def scores_bounded(i, p, first_tile):
        pl_ = slice(p * LANES, (p + 1) * LANES)
        first, keys, cols = band(i, first_tile)
        qp = q_s[i * Q_BLOCK:(i + 1) * Q_BLOCK, pl_]
        kp = band_keys(p, keys)
        for half in range(2):
            h = 2 * p + half
            own = lo if half == 0 else jnp.logical_not(lo)
            s = _dot(jnp.where(own, qp, jnp.zeros_like(qp)), kp) + bias_ref[h, :, cols]
            acc = jnp.zeros((Q_BLOCK, LANES), F32)
            for j in range(n_tiles - first):
                e = jnp.exp2(s[:, j * LANES:(j + 1) * LANES])
                acc = acc + e
                p_s[h, :, (first + j) * LANES:(first + j + 1) * LANES] = e.astype(BF16)
            l_s[h] = jnp.broadcast_to(jnp.sum(acc, axis=1, keepdims=True), (Q_BLOCK, LANES))

    def exps(i, p, first_tile):
        first, _, _ = band(i, first_tile)
        for h in (2 * p, 2 * p + 1):
            m = m_s[h]
            acc = jnp.zeros((Q_BLOCK, LANES), F32)
            for j in range(first, n_tiles):
                tl = slice(j * LANES, (j + 1) * LANES)
                e = jnp.exp2(s_s[h, :, tl] - m)
                acc = acc + e
                p_s[h, :, tl] = e.astype(BF16)
            l_s[h] = jnp.broadcast_to(jnp.sum(acc, axis=1, keepdims=True), (Q_BLOCK, LANES))

    def values(i, p, first_tile):
        pl_ = slice(p * LANES, (p + 1) * LANES)
        _, keys, cols = band(i, first_tile)
        vp = band_values(p, keys)
        o0 = _dot(p_s[2 * p, :, cols], vp) / l_s[2 * p]
        o1 = _dot(p_s[2 * p + 1, :, cols], vp) / l_s[2 * p + 1]
        a_s[i * Q_BLOCK:(i + 1) * Q_BLOCK, pl_] = jnp.where(lo, o0, o1).astype(BF16)

    def block(i, first_tile, stages):
        for stage in stages:
            for p in range(n_pairs):
                stage(i, p, first_tile)

    first = t == 0
    for i in range(n_blocks):
        for stages, use in (((scores_bounded, values), bounded),
                            ((scores, exps, values), jnp.logical_not(bounded))):
            pl.when(jnp.logical_and(use, first))(functools.partial(block, i, True, stages))
            pl.when(jnp.logical_and(use, jnp.logical_not(first)))(functools.partial(block, i, False, stages))

    o_ref[0] = x + _dot(a_s[...], wo_ref[...])


def _attn_layer(h, g, w_q, w_o, bias, k_norm2, limit2, k_t, v):
    bsz, seq, _ = h.shape
    assert LEFT == ROW_TILE
    row = pl.BlockSpec((1, ROW_TILE, D_MODEL), lambda b, t: (b, t, 0))
    kt_prev = pl.BlockSpec((1, D_MODEL, ROW_TILE), lambda b, t: (b, 0, jnp.maximum(t - 1, 0)))
    kt_cur = pl.BlockSpec((1, D_MODEL, ROW_TILE), lambda b, t: (b, 0, t))
    v_prev = pl.BlockSpec((1, ROW_TILE, D_MODEL), lambda b, t: (b, jnp.maximum(t - 1, 0), 0))
    return pl.pallas_call(
        _attn_kernel,
        grid=(bsz, seq // ROW_TILE),
        in_specs=[row, _const_spec((1, D_MODEL)), _const_spec((D_MODEL, D_MODEL)),
                  _const_spec((D_MODEL, D_MODEL)), _const_spec((B_HEADS, Q_BLOCK, K_BLOCK)),
                  pl.BlockSpec((1, B_HEADS // 2, LANES), lambda b, t: (b, 0, 0)),
                  _const_spec((B_HEADS // 2, LANES)), kt_prev, kt_cur, v_prev, row],
        out_specs=row,
        out_shape=jax.ShapeDtypeStruct(h.shape, F32),
        scratch_shapes=[pltpu.VMEM((ROW_TILE, D_MODEL), BF16),
                        pltpu.VMEM((ROW_TILE, D_MODEL), BF16),
                        pltpu.VMEM((B_HEADS, Q_BLOCK, K_BLOCK), F32),
                        pltpu.VMEM((B_HEADS, Q_BLOCK, K_BLOCK), BF16),
                        pltpu.VMEM((B_HEADS, Q_BLOCK, LANES), F32),
                        pltpu.VMEM((B_HEADS, Q_BLOCK, LANES), F32)],
        compiler_params=pltpu.CompilerParams(
            dimension_semantics=("arbitrary", "arbitrary"), vmem_limit_bytes=VMEM_LIMIT),
        name="attn",
    )(h, g, w_q, w_o, bias, k_norm2, limit2, k_t, k_t, v, v)


def _rel_bias(rel_table):
    n_heads = rel_table.shape[0]
    tab = rel_table.astype(F32) * LOG2E
    span = BAND + CHUNK - 1
    n_var = MAX_REL + CHUNK
    base = jnp.concatenate(
        [tab[:, MAX_REL - (CHUNK - 1):2 * MAX_REL + 1],
         jnp.broadcast_to(tab[:, 2 * MAX_REL:], (n_heads, span - n_var))], axis=1)
    b_max = jnp.max(base, axis=1, keepdims=True)
    b_min = jnp.min(base, axis=1, keepdims=True)
    base = base - 0.5 * (b_max + b_min)
    half_range = jnp.max((0.5 * (b_max - b_min)).reshape(n_heads // 2, 2), axis=1, keepdims=True)
    room = SCORE_LIMIT - half_range
    limit2 = jnp.broadcast_to(jnp.where(room > 0, room * room, -1.0), (n_heads // 2, LANES))
    rev = base[:, ::-1]
    padded = jnp.pad(rev, ((0, 0), (0, 1)))
    rows = jnp.tile(padded, (1, CHUNK))[:, :CHUNK * span].reshape(n_heads, CHUNK, span)
    band = rows[:, :, CHUNK - 1:CHUNK - 1 + BAND]
    ninf = jnp.full((n_heads, CHUNK, CHUNK), -jnp.inf, F32)
    bias = jnp.concatenate([jnp.concatenate([band, ninf], axis=2),
                            jnp.concatenate([ninf, band], axis=2)], axis=1)
    return bias, limit2


def _mlstm_params(w_in, b_gate, g_head):
    w_q = w_in[:, 0:A_QK]
    w_k = w_in[:, A_QK:2 * A_QK]
    w_vo = w_in[:, 2 * A_QK:A_GATE_OFF]
    w_g = w_in[:, A_GATE_OFF:]
    w_fm = jnp.concatenate([w_q, w_vo, w_g], axis=1).T.astype(BF16)
    b_g = jnp.broadcast_to(b_gate.astype(F32)[:, None], (2 * A_HEADS, A_ROWS))
    g_head_b = jnp.broadcast_to(g_head.astype(F32)[:, None], (A_V, LANES))
    return w_fm, w_k.astype(BF16), b_g, g_head_b


def kernel(x, a_w_in, a_b_gate, a_g_head, a_w_out, b_w_q, b_rel_bias, b_w_o, kv_norm_g, w_kv,
           norm_mix_g, norm_ffn_g, ffn_w_gate, ffn_w_up, ffn_w_down, final_norm_g):
    bsz, seq, d = x.shape
    assert d == D_MODEL and seq % max(ROW_TILE, A_ROWS, KV_ROWS) == 0
    depth = norm_mix_g.shape[0]
    n_a = a_w_in.shape[0]

    def row(v):
        return v.reshape(1, -1).astype(F32)

    h = x
    k_t = v_sh = k_norm2 = None
    for l in range(depth):
        if l == n_a:
            k_t, v_sh, k_norm2 = _kv_proj(h, row(kv_norm_g), w_kv[:, :D_MODEL].T.astype(BF16),
                                          w_kv[:, D_MODEL:].astype(BF16))
            k_norm2 = jnp.max(k_norm2, axis=1)
        if l < n_a:
            h = _mlstm_layer(h, row(norm_mix_g[l]), *_mlstm_params(a_w_in[l], a_b_gate[l], a_g_head[l]),
                             a_w_out[l].astype(BF16))
        else:
            j = l - n_a
            bias, limit2 = _rel_bias(b_rel_bias[j])
            h = _attn_layer(h, row(norm_mix_g[l]), b_w_q[j].astype(BF16), b_w_o[j].astype(BF16),
                            bias, k_norm2, limit2, k_t, v_sh)
        h = _ffn(h.reshape(bsz * seq, d), row(norm_ffn_g[l]), ffn_w_gate[l].astype(BF16),
                 ffn_w_up[l].astype(BF16), ffn_w_down[l].astype(BF16), row(final_norm_g),
                 final_norm=(l == depth - 1)).reshape(bsz, seq, d)
    return h
```
